```python
import math
import jax, jax.numpy as jnp
from jax import lax
import numpy as np


D_MODEL = 1024
BATCH = 4
SEQ = 8192
DEPTH = 2

GRID_W = 64
CTX_LEN = 256
N_MIXERS = 2
N_DA_LAYERS = (DEPTH + N_MIXERS - 1) // N_MIXERS
N_GLA_LAYERS = DEPTH // N_MIXERS

DA_HEADS = 8
DA_HEAD_DIM = D_MODEL // (2 * DA_HEADS)
DA_Q_BLOCK = 128
ROPE_BASE = 10000.0
ROPE_PAIRS_AXIS = DA_HEAD_DIM // 4

GLA_HEADS = 4
GLA_DK = D_MODEL // 2
GLA_DV = D_MODEL
GLA_DK_HEAD = GLA_DK // GLA_HEADS
GLA_DV_HEAD = GLA_DV // GLA_HEADS
GLA_GATE_RANK = 16
GLA_TAU = 16.0
GLA_CHUNK = 64
GLA_IN = 2 * GLA_DK + 2 * GLA_DV + 2 * GLA_GATE_RANK

N_EXPERTS = 32
TOP_K = 4
D_EXPERT = D_MODEL
SWIGLU_ALPHA = 1.702
SWIGLU_LIMIT = 7.0
MOE_BLOCK = 256

DEEPNORM_ALPHA = (2.0 * DEPTH) ** 0.25
DEEPNORM_BETA = (8.0 * DEPTH) ** -0.25
NORM_EPS = 1e-5

kernel_name = 'hybrid_diffattn_gla_moe_dit'

F32 = jnp.float32


def layer_norm(x, g, b):
    xf = x.astype(F32)
    mu = jnp.mean(xf, -1, keepdims=True)
    xc = xf - mu
    y = xc * lax.rsqrt(jnp.mean(xc * xc, -1, keepdims=True) + NORM_EPS)
    return (y * g + b).astype(x.dtype)


def rms_norm(x, w):
    xf = x.astype(F32)
    y = xf * lax.rsqrt(jnp.mean(xf * xf, -1, keepdims=True) + NORM_EPS)
    return (y * w).astype(x.dtype)


def lambda_init(layer_idx):
    return 0.8 - 0.6 * math.exp(-0.3 * layer_idx)


def axial_rope(L):
    rows = L // GRID_W
    row = jnp.repeat(jnp.arange(rows), GRID_W).astype(F32)
    col = jnp.tile(jnp.arange(GRID_W), rows).astype(F32)
    inv = ROPE_BASE ** (-jnp.arange(ROPE_PAIRS_AXIS, dtype=F32) / ROPE_PAIRS_AXIS)
    ang = jnp.concatenate([row[:, None] * inv, col[:, None] * inv], -1)
    return jnp.cos(ang), jnp.sin(ang)


def apply_rope(x, cos, sin):
    half = x.shape[-1] // 2
    xf = x.astype(F32)
    x1, x2 = xf[..., :half], xf[..., half:]
    cos = cos[None, :, None, None, :]
    sin = sin[None, :, None, None, :]
    return jnp.concatenate([x1 * cos - x2 * sin, x1 * sin + x2 * cos], -1).astype(x.dtype)


def diff_attend(q, k, v, lam):
    s = jnp.einsum('bqhmd,bkhmd->bhmqk', q, k).astype(F32) * (DA_HEAD_DIM ** -0.5)
    p = jax.nn.softmax(s, axis=-1)
    a = p[:, :, 0] - lam * p[:, :, 1]
    return jnp.einsum('bhqk,bkhe->bqhe', a.astype(v.dtype), v)


def diff_attention_mixer(t_lat, t_ctx, w_in, w_out, lam_vecs, subln_w, lam_init, need_ctx):
    def qkv(t):
        bsz, L, _ = t.shape
        q, k, v = jnp.split(t @ w_in, 3, axis=-1)
        return (q.reshape(bsz, L, DA_HEADS, 2, DA_HEAD_DIM),
                k.reshape(bsz, L, DA_HEADS, 2, DA_HEAD_DIM),
                v.reshape(bsz, L, DA_HEADS, 2 * DA_HEAD_DIM))

    def out(o):
        bsz, L = o.shape[:2]
        o = rms_norm(o, subln_w) * (1.0 - lam_init)
        return o.reshape(bsz, L, D_MODEL) @ w_out

    lv = lam_vecs.astype(F32)
    lam = jnp.exp(jnp.sum(lv[0] * lv[1])) - jnp.exp(jnp.sum(lv[2] * lv[3])) + lam_init
    bsz, L, _ = t_lat.shape
    q, k, v = qkv(t_lat)
    cos, sin = axial_rope(L)
    q = apply_rope(q, cos, sin)
    k = apply_rope(k, cos, sin)
    qc, kc, vc = qkv(t_ctx)
    k_all = jnp.concatenate([kc, k], axis=1)
    v_all = jnp.concatenate([vc, v], axis=1)
    nb = L // DA_Q_BLOCK
    q_blocks = jnp.moveaxis(q.reshape(bsz, nb, DA_Q_BLOCK, DA_HEADS, 2, DA_HEAD_DIM), 1, 0)
    o = lax.map(lambda qb: diff_attend(qb, k_all, v_all, lam), q_blocks)
    o = jnp.moveaxis(o, 0, 1).reshape(bsz, L, DA_HEADS, 2 * DA_HEAD_DIM)
    y_lat = out(o)
    y_ctx = out(diff_attend(qc, kc, vc, lam)) if need_ctx else None
    return y_lat, y_ctx


def gla_project(t, w_in, w_gate, b_gate):
    bsz, L, _ = t.shape
    cuts = [GLA_DK, 2 * GLA_DK, 2 * GLA_DK + GLA_DV, 2 * GLA_DK + 2 * GLA_DV,
            2 * GLA_DK + 2 * GLA_DV + GLA_GATE_RANK]
    q, k, v, r, zf, zb = jnp.split(t @ w_in, cuts, axis=-1)

    def heads(a, dh):
        return a.reshape(bsz, L, GLA_HEADS, dh).transpose(0, 2, 1, 3).astype(F32)

    def log_gate(z, d):
        g = jax.nn.log_sigmoid((z @ w_gate[d] + b_gate[d]).astype(F32)) / GLA_TAU
        return heads(g, GLA_DK_HEAD)

    return (heads(q, GLA_DK_HEAD) * (GLA_DK_HEAD ** -0.5), heads(k, GLA_DK_HEAD),
            heads(v, GLA_DV_HEAD), r, log_gate(zf, 0), log_gate(zb, 1))


def gla_chunk_scan(q, k, v, g, s0):
    bsz, h, L, dk = q.shape
    dv = v.shape[-1]
    n = L // GLA_CHUNK
    q, k, g = (a.reshape(bsz, h, n, GLA_CHUNK, dk) for a in (q, k, g))
    v = v.reshape(bsz, h, n, GLA_CHUNK, dv)
    b = jnp.cumsum(g, axis=3)
    b_last = b[:, :, :, -1:, :]
    q_in = q * jnp.exp(b)
    k_in = k * jnp.exp(-b)
    k_state = k * jnp.exp(b_last - b)
    mask = jnp.tril(jnp.ones((GLA_CHUNK, GLA_CHUNK), bool))
    att = jnp.where(mask, jnp.einsum('bhncd,bhnsd->bhncs', q_in, k_in), 0.0)
    o = jnp.einsum('bhncs,bhnse->bhnce', att, v)
    ds = jnp.einsum('bhncd,bhnce->bhnde', k_state, v)
    decay = jnp.exp(b_last[:, :, :, 0, :])

    def step(s, inp):
        dec, d = inp
        return dec[..., None] * s + d, s

    s_final, s_prev = lax.scan(step, s0, (jnp.moveaxis(decay, 2, 0), jnp.moveaxis(ds, 2, 0)))
    o = o + jnp.einsum('bhncd,bhnde->bhnce', q_in, jnp.moveaxis(s_prev, 0, 2))
    return o.reshape(bsz, h, L, dv), s_final


def gla_final_state(k, v, g):
    bc = jnp.cumsum(g, axis=2)
    w = jnp.exp(bc[:, :, -1:, :] - bc)
    return jnp.einsum('bhld,bhle->bhde', k * w, v)


def gla_output(o, r, norm_w, w_out):
    bsz, _, L, _ = o.shape
    o = rms_norm(o, norm_w).transpose(0, 2, 1, 3).reshape(bsz, L, GLA_DV).astype(r.dtype)
    return (o * jax.nn.silu(r)) @ w_out


def gla_mixer(t_lat, t_ctx, w_in, w_gate, b_gate, norm_w, w_out, need_ctx):
    def flip(a):
        return jnp.flip(a, axis=2)

    q, k, v, r, gf, gb = gla_project(t_lat, w_in, w_gate, b_gate)
    qc, kc, vc, rc, gcf, gcb = gla_project(t_ctx, w_in, w_gate, b_gate)
    if need_ctx:
        s0 = jnp.zeros((kc.shape[0], GLA_HEADS, GLA_DK_HEAD, GLA_DV_HEAD), F32)
        ocf, s_f = gla_chunk_scan(qc, kc, vc, gcf, s0)
        ocb, s_b = gla_chunk_scan(flip(qc), flip(kc), flip(vc), flip(gcb), s0)
        y_ctx = gla_output(ocf + flip(ocb), rc, norm_w, w_out)
    else:
        s_f = gla_final_state(kc, vc, gcf)
        s_b = gla_final_state(flip(kc), flip(vc), flip(gcb))
        y_ctx = None
    of, _ = gla_chunk_scan(q, k, v, gf, s_f)
    ob, _ = gla_chunk_scan(flip(q), flip(k), flip(v), flip(gb), s_b)
    y_lat = gla_output(of + flip(ob), r, norm_w, w_out)
    return y_lat, y_ctx


def moe_ffn(h, router_w, router_b, w_gu, b_gu, w_down, b_down):
    T, D = h.shape
    logits = (h @ router_w + router_b).astype(F32)
    top_logit, top_idx = lax.top_k(logits, TOP_K)
    gates = jax.nn.softmax(top_logit, axis=-1)
    n_assign = T * TOP_K
    flat_e = top_idx.reshape(-1)
    order = jnp.argsort(flat_e)
    sorted_e = flat_e[order]
    sorted_tok = order // TOP_K
    sorted_gate = gates.reshape(-1)[order]
    counts = jnp.bincount(flat_e, length=N_EXPERTS)
    padded = (counts + MOE_BLOCK - 1) // MOE_BLOCK * MOE_BLOCK
    start = jnp.cumsum(counts) - counts
    padded_end = jnp.cumsum(padded)
    padded_start = padded_end - padded
    dest = padded_start[sorted_e] + jnp.arange(n_assign) - start[sorted_e]
    n_blocks = -(-n_assign // MOE_BLOCK) + N_EXPERTS
    n_rows = n_blocks * MOE_BLOCK
    row_tok = jnp.full((n_rows,), T, jnp.int32).at[dest].set(sorted_tok.astype(jnp.int32))
    row_gate = jnp.zeros((n_rows,), F32).at[dest].set(sorted_gate)
    block_expert = jnp.minimum(
        jnp.searchsorted(padded_end, jnp.arange(n_blocks) * MOE_BLOCK, side='right'), N_EXPERTS - 1)
    h_pad = jnp.concatenate([h, jnp.zeros((1, D), h.dtype)], axis=0)
    xb = h_pad[row_tok].reshape(n_blocks, MOE_BLOCK, D)

    def expert_block(args):
        xe, e = args
        gu = xe @ w_gu[e] + b_gu[e]
        glu, lin = jnp.split(gu, 2, axis=-1)
        glu = jnp.minimum(glu, SWIGLU_LIMIT)
        lin = jnp.clip(lin, -SWIGLU_LIMIT, SWIGLU_LIMIT)
        act = glu * jax.nn.sigmoid(SWIGLU_ALPHA * glu) * (lin + 1.0)
        return act @ w_down[e] + b_down[e]

    yb = lax.map(expert_block, (xb, block_expert)).reshape(n_rows, D)
    yb = yb * row_gate[:, None].astype(yb.dtype)
    out = jnp.zeros((T + 1, D), yb.dtype).at[row_tok].add(yb)
    return out[:T]


def setup_inputs(seed: int = 0) -> dict:
    key = jax.random.key(seed)
    ks = jax.random.split(key, 23)
    D = D_MODEL

    def nrm(k, shape, scale):
        return jax.random.normal(k, shape, F32) * scale

    return {
        'x': nrm(ks[0], (BATCH, SEQ, D), 1.0),
        'c': nrm(ks[1], (BATCH, D), 1.0),
        'ctx': nrm(ks[2], (BATCH, CTX_LEN, D), 1.0),
        'c_ctx': nrm(ks[3], (D,), 1.0),
        'ada_w': nrm(ks[4], (DEPTH, D, 6 * D), D ** -0.5),
        'ada_b': nrm(ks[5], (DEPTH, 6 * D), 0.02),
        'ln_g': 1.0 + nrm(ks[6], (DEPTH, 2, D), 0.02),
        'ln_b': nrm(ks[7], (DEPTH, 2, D), 0.02),
        'da_w_in': nrm(ks[8], (N_DA_LAYERS, D, 3 * D), D ** -0.5),
        'da_w_out': nrm(ks[9], (N_DA_LAYERS, D, D), D ** -0.5 * DEEPNORM_BETA),
        'da_lambda': nrm(ks[10], (N_DA_LAYERS, 4, DA_HEAD_DIM), 0.1),
        'da_subln_w': 1.0 + nrm(ks[11], (N_DA_LAYERS, 2 * DA_HEAD_DIM), 0.02),
        'gla_w_in': nrm(ks[12], (N_GLA_LAYERS, D, GLA_IN), D ** -0.5),
        'gla_w_gate': nrm(ks[13], (N_GLA_LAYERS, 2, GLA_GATE_RANK, GLA_DK), GLA_GATE_RANK ** -0.5),
        'gla_b_gate': nrm(ks[14], (N_GLA_LAYERS, 2, GLA_DK), 0.1),
        'gla_norm_w': 1.0 + nrm(ks[15], (N_GLA_LAYERS, GLA_DV_HEAD), 0.02),
        'gla_w_out': nrm(ks[16], (N_GLA_LAYERS, GLA_DV, D), GLA_DV ** -0.5 * DEEPNORM_BETA),
        'router_w': nrm(ks[17], (DEPTH, D, N_EXPERTS), D ** -0.5),
        'router_b': nrm(ks[18], (DEPTH, N_EXPERTS), 0.01),
        'moe_w_gu': nrm(ks[19], (DEPTH, N_EXPERTS, D, 2 * D_EXPERT), D ** -0.5),
        'moe_b_gu': nrm(ks[20], (DEPTH, N_EXPERTS, 2 * D_EXPERT), 0.01),
        'moe_w_down': nrm(ks[21], (DEPTH, N_EXPERTS, D_EXPERT, D), D_EXPERT ** -0.5 * DEEPNORM_BETA),
        'moe_b_down': nrm(ks[22], (DEPTH, N_EXPERTS, D), 0.01),
    }


def reference(x, c, ctx, c_ctx, ada_w, ada_b, ln_g, ln_b, da_w_in, da_w_out, da_lambda,
              da_subln_w, gla_w_in, gla_w_gate, gla_b_gate, gla_norm_w, gla_w_out,
              router_w, router_b, moe_w_gu, moe_b_gu, moe_w_down, moe_b_down):
    bsz, seq, _ = x.shape
    n_ctx = ctx.shape[1]
    silu_c = jax.nn.silu(c)
    silu_cc = jax.nn.silu(c_ctx)
    for i in range(DEPTH):
        last = i == DEPTH - 1
        sh1, sc1, g1, sh2, sc2, g2 = jnp.split((silu_c @ ada_w[i] + ada_b[i])[:, None, :], 6, axis=-1)
        csh1, csc1, cg1, csh2, csc2, cg2 = jnp.split(silu_cc @ ada_w[i] + ada_b[i], 6, axis=-1)
        t_lat = x * (1.0 + sc1) + sh1
        t_ctx = ctx * (1.0 + csc1) + csh1
        j = i // N_MIXERS
        if i % N_MIXERS == 0:
            y_lat, y_ctx = diff_attention_mixer(t_lat, t_ctx, da_w_in[j], da_w_out[j], da_lambda[j],
                                                da_subln_w[j], lambda_init(i), not last)
        else:
            y_lat, y_ctx = gla_mixer(t_lat, t_ctx, gla_w_in[j], gla_w_gate[j], gla_b_gate[j],
                                     gla_norm_w[j], gla_w_out[j], not last)
        x = layer_norm(DEEPNORM_ALPHA * x + g1 * y_lat, ln_g[i, 0], ln_b[i, 0])
        u_lat = (x * (1.0 + sc2) + sh2).reshape(-1, D_MODEL)
        if last:
            f_lat = moe_ffn(u_lat, router_w[i], router_b[i], moe_w_gu[i], moe_b_gu[i],
                            moe_w_down[i], moe_b_down[i])
        else:
            ctx = layer_norm(DEEPNORM_ALPHA * ctx + cg1 * y_ctx, ln_g[i, 0], ln_b[i, 0])
            u_ctx = (ctx * (1.0 + csc2) + csh2).reshape(-1, D_MODEL)
            f = moe_ffn(jnp.concatenate([u_lat, u_ctx], axis=0), router_w[i], router_b[i],
                        moe_w_gu[i], moe_b_gu[i], moe_w_down[i], moe_b_down[i])
            f_lat = f[:bsz * seq]
            ctx = layer_norm(DEEPNORM_ALPHA * ctx + cg2 * f[bsz * seq:].reshape(bsz, n_ctx, D_MODEL),
                             ln_g[i, 1], ln_b[i, 1])
        x = layer_norm(DEEPNORM_ALPHA * x + g2 * f_lat.reshape(bsz, seq, D_MODEL), ln_g[i, 1], ln_b[i, 1])
    return x
```

```python
import functools
import math

import numpy as np
import jax
import jax.numpy as jnp
from jax import lax
from jax.experimental import pallas as pl
from jax.experimental.pallas import tpu as pltpu

F32 = jnp.float32
BF16 = jnp.bfloat16

D_MODEL = 1024
DEPTH = 2
GRID_W = 64

DA_HEADS = 8
DA_HEAD_DIM = 64
DA_HEAD_W = 2 * DA_HEAD_DIM
ROPE_BASE = 10000.0
ROPE_PAIRS_AXIS = DA_HEAD_DIM // 4

GLA_HEADS = 4
GLA_DK = D_MODEL // 2
GLA_DV = D_MODEL
GLA_DK_HEAD = GLA_DK // GLA_HEADS
GLA_DV_HEAD = GLA_DV // GLA_HEADS
GLA_GATE_RANK = 16
GLA_TAU = 16.0
GLA_CHUNK = 64

N_EXPERTS = 32
TOP_K = 4
SWIGLU_ALPHA = 1.702
SWIGLU_LIMIT = 7.0
MOE_BLOCK = 256

DEEPNORM_ALPHA = (2.0 * DEPTH) ** 0.25
NORM_EPS = 1e-5

LANES = 128
TM = 256
VMEM_LIMIT = 48 * 1024 * 1024

MOD_SC1, MOD_SH1, MOD_G1, MOD_SC2, MOD_SH2, MOD_G2 = range(6)


def _cparams(n_axes):
    return pltpu.CompilerParams(dimension_semantics=("arbitrary",) * n_axes,
                                vmem_limit_bytes=VMEM_LIMIT)


def _lambda_init(layer_idx):
    return 0.8 - 0.6 * math.exp(-0.3 * layer_idx)


def _ada_kernel(c_ref, w_ref, b_ref, o_ref):
    c = c_ref[...]
    s = c * jax.nn.sigmoid(c)
    o_ref[0] = jnp.dot(s, w_ref[0], preferred_element_type=F32) + b_ref[0]


def _ada_mods(cc, ada_w, ada_b):
    nt = 1536
    n6 = 6 * D_MODEL
    return pl.pallas_call(
        _ada_kernel,
        grid=(DEPTH, n6 // nt),
        in_specs=[pl.BlockSpec((8, D_MODEL), lambda l, j: (0, 0)),
                  pl.BlockSpec((1, D_MODEL, nt), lambda l, j: (l, 0, j)),
                  pl.BlockSpec((1, 1, nt), lambda l, j: (l, 0, j))],
        out_specs=pl.BlockSpec((1, 8, nt), lambda l, j: (l, 0, j)),
        out_shape=jax.ShapeDtypeStruct((DEPTH, 8, n6), F32),
        compiler_params=_cparams(2),
        name="ada_mods",
    )(cc, ada_w, ada_b.reshape(DEPTH, 1, n6))


def _proj_da_kernel(x_ref, mod_ref, cos_ref, sin_ref, wqk_ref, wv_ref, q_ref, k_ref, v_ref):
    m = mod_ref[0]
    t = (x_ref[0] * m[MOD_SC1:MOD_SC1 + 1] + m[MOD_SH1:MOD_SH1 + 1]).astype(BF16)
    cos = cos_ref[...]
    sin = sin_ref[...]
    q_scale = DA_HEAD_DIM ** -0.5
    for j in range(DA_HEADS):
        y2 = jnp.dot(t, wqk_ref[:, j * 256:(j + 1) * 256], preferred_element_type=F32)
        for hh in range(2):
            y = y2[:, hh * LANES:(hh + 1) * LANES]
            y = y * cos + pltpu.roll(y, 64, 1) * sin
            col = (2 * j + hh) * LANES
            if col < D_MODEL:
                q_ref[0, :, col:col + LANES] = (y * q_scale).astype(BF16)
            else:
                k_ref[0, :, col - D_MODEL:col - D_MODEL + LANES] = y.astype(BF16)
    v_ref[0] = jnp.dot(t, wv_ref[...], preferred_element_type=F32).astype(BF16)


def _proj_da(xa, mods, cos, sin, wqk, wv):
    B, LT, D = xa.shape
    nb = LT // TM
    nbl = nb - 1
    out = jax.ShapeDtypeStruct((B, LT, D), BF16)
    blk = pl.BlockSpec((1, TM, D), lambda b, i: (b, i, 0))
    return pl.pallas_call(
        _proj_da_kernel,
        grid=(B, nb),
        in_specs=[blk,
                  pl.BlockSpec((1, 8, D), lambda b, i: (2 * b + i // nbl, 0, 0)),
                  pl.BlockSpec((TM, LANES), lambda b, i: (i, 0)),
                  pl.BlockSpec((TM, LANES), lambda b, i: (i, 0)),
                  pl.BlockSpec((D, 2 * D), lambda b, i: (0, 0)),
                  pl.BlockSpec((D, D), lambda b, i: (0, 0))],
        out_specs=[blk, blk, blk],
        out_shape=[out, out, out],
        compiler_params=_cparams(2),
        name="da_proj",
    )(xa, mods, cos, sin, wqk, wv)


def _attn_kernel(lam_ref, q_ref, k_ref, v_ref, sw_ref, o_ref, vext_sc, m_sc, acc_sc, *,
                 nb, lam_init):
    i = pl.program_id(2)
    tq = q_ref.shape[1]
    tk = TM

    @pl.when(i == 0)
    def _():
        vext_sc[:, :DA_HEAD_W] = v_ref[0]
        vext_sc[:, DA_HEAD_W:] = jnp.ones((vext_sc.shape[0], DA_HEAD_W), BF16)

    q = q_ref[0]
    lane = lax.broadcasted_iota(jnp.int32, (1, DA_HEAD_W), 1)
    map0 = (lane % 64) < 32
    zero = jnp.zeros_like(q)
    qs = jnp.concatenate([jnp.where(map0, q, zero), jnp.where(map0, zero, q)], axis=0)
    m_sc[...] = jnp.full(m_sc.shape, -jnp.inf, F32)
    acc_sc[...] = jnp.zeros(acc_sc.shape, F32)

    def step(kb):
        off = kb * tk if isinstance(kb, int) else pl.multiple_of(kb * tk, tk)
        k = k_ref[0, pl.ds(off, tk), :]
        s = lax.dot_general(qs, k, (((1,), (1,)), ((), ())), preferred_element_type=F32)
        m_prev = m_sc[...]
        m_new = jnp.maximum(m_prev, jnp.max(s, axis=1, keepdims=True))
        alpha = jnp.exp(m_prev - m_new)
        p = jnp.exp(s - m_new)
        pv = jnp.dot(p.astype(BF16), vext_sc[pl.ds(off, tk), :], preferred_element_type=F32)
        acc_sc[...] = alpha * acc_sc[...] + pv
        m_sc[...] = m_new

    @pl.when(i < nb - 1)
    def _():
        def body(kb, carry):
            step(kb)
            return carry
        lax.fori_loop(0, nb, body, 0)

    @pl.when(i == nb - 1)
    def _():
        step(nb - 1)

    acc = acc_sc[...]
    o0 = acc[:tq, :DA_HEAD_W] / acc[:tq, DA_HEAD_W:DA_HEAD_W + 1]
    o1 = acc[tq:, :DA_HEAD_W] / acc[tq:, DA_HEAD_W:DA_HEAD_W + 1]
    o = o0 - lam_ref[0] * o1
    o = o * lax.rsqrt(jnp.mean(o * o, axis=-1, keepdims=True) + NORM_EPS)
    o_ref[0] = (o * sw_ref[...] * (1.0 - lam_init)).astype(BF16)


def _diff_attention(lam, q, k, v, subln_w, lam_init):
    B, LT, D = q.shape
    nb = LT // TM
    kern = functools.partial(_attn_kernel, nb=nb, lam_init=lam_init)
    grid_spec = pltpu.PrefetchScalarGridSpec(
        num_scalar_prefetch=1,
        grid=(B, DA_HEADS, nb),
        in_specs=[pl.BlockSpec((1, TM, DA_HEAD_W), lambda b, h, i, lam: (b, i, h)),
                  pl.BlockSpec((1, LT, DA_HEAD_W), lambda b, h, i, lam: (b, 0, h)),
                  pl.BlockSpec((1, LT, DA_HEAD_W), lambda b, h, i, lam: (b, 0, h)),
                  pl.BlockSpec((1, DA_HEAD_W), lambda b, h, i, lam: (0, 0))],
        out_specs=pl.BlockSpec((1, TM, DA_HEAD_W), lambda b, h, i, lam: (b, i, h)),
        scratch_shapes=[pltpu.VMEM((LT, 2 * DA_HEAD_W), BF16),
                        pltpu.VMEM((2 * TM, 1), F32),
                        pltpu.VMEM((2 * TM, 2 * DA_HEAD_W), F32)],
    )
    return pl.pallas_call(
        kern,
        grid_spec=grid_spec,
        out_shape=jax.ShapeDtypeStruct((B, LT, D), BF16),
        compiler_params=_cparams(3),
        name="diff_attn",
    )(lam, q, k, v, subln_w.reshape(1, DA_HEAD_W))


def _post_mixer(pre, w_ref, x_ref, mod_ref, ln_ref, rw_ref, rb_ref, x1_ref, u_ref, lg_ref):
    m = mod_ref[0]
    y = jnp.dot(pre, w_ref[...], preferred_element_type=F32)
    z = DEEPNORM_ALPHA * x_ref[0] + m[MOD_G1:MOD_G1 + 1] * y
    mu = jnp.mean(z, axis=-1, keepdims=True)
    zc = z - mu
    x1 = zc * lax.rsqrt(jnp.mean(zc * zc, axis=-1, keepdims=True) + NORM_EPS)
    x1 = x1 * ln_ref[0:1] + ln_ref[1:2]
    x1_ref[0] = x1
    u = x1 * m[MOD_SC2:MOD_SC2 + 1] + m[MOD_SH2:MOD_SH2 + 1]
    u_ref[0] = u.astype(BF16)
    lg_ref[0] = jnp.dot(u, rw_ref[...], preferred_element_type=F32,
                        precision=lax.Precision.HIGHEST) + rb_ref[...]


def _out_da_kernel(o_ref, w_ref, x_ref, mod_ref, ln_ref, rw_ref, rb_ref, x1_ref, u_ref, lg_ref):
    _post_mixer(o_ref[0], w_ref, x_ref, mod_ref, ln_ref, rw_ref, rb_ref, x1_ref, u_ref, lg_ref)


def _out_gla_kernel(of_ref, ob_ref, r_ref, nw_ref, w_ref, x_ref, mod_ref, ln_ref, rw_ref, rb_ref,
                    x1_ref, u_ref, lg_ref):
    parts = []
    for h in range(GLA_HEADS):
        sl = slice(h * GLA_DV_HEAD, (h + 1) * GLA_DV_HEAD)
        o = of_ref[0, :, sl] + ob_ref[0, :, sl]
        o = o * lax.rsqrt(jnp.mean(o * o, axis=-1, keepdims=True) + NORM_EPS) * nw_ref[...]
        r = r_ref[0, :, sl]
        parts.append((o * (r * jax.nn.sigmoid(r))).astype(BF16))
    pre = jnp.concatenate(parts, axis=1)
    _post_mixer(pre, w_ref, x_ref, mod_ref, ln_ref, rw_ref, rb_ref, x1_ref, u_ref, lg_ref)


def _mixer_out(kind, acts, w_out, xa, mods, lnp, rw, rb, nb_out, norm_w=None):
    B, LT, D = xa.shape
    nbl = LT // TM - 1
    blk = pl.BlockSpec((1, TM, D), lambda b, i: (b, i, 0))
    common_specs = [pl.BlockSpec((D, D), lambda b, i: (0, 0)),
                    blk,
                    pl.BlockSpec((1, 8, D), lambda b, i: (2 * b + i // nbl, 0, 0)),
                    pl.BlockSpec((2, D), lambda b, i: (0, 0)),
                    pl.BlockSpec((D, LANES), lambda b, i: (0, 0)),
                    pl.BlockSpec((1, LANES), lambda b, i: (0, 0))]
    lout = nb_out * TM
    out_shape = [jax.ShapeDtypeStruct((B, lout, D), F32),
                 jax.ShapeDtypeStruct((B, lout, D), BF16),
                 jax.ShapeDtypeStruct((B, lout, LANES), F32)]
    out_specs = [blk, blk, pl.BlockSpec((1, TM, LANES), lambda b, i: (b, i, 0))]
    if kind == "da":
        kern = _out_da_kernel
        in_specs = [blk] + common_specs
        args = list(acts)
    else:
        kern = _out_gla_kernel
        in_specs = [blk, blk, blk, pl.BlockSpec((1, GLA_DV_HEAD), lambda b, i: (0, 0))] + common_specs
        args = list(acts) + [norm_w.reshape(1, GLA_DV_HEAD)]
    return pl.pallas_call(
        kern,
        grid=(B, nb_out),
        in_specs=in_specs,
        out_specs=out_specs,
        out_shape=out_shape,
        compiler_params=_cparams(2),
        name="mixer_out_" + kind,
    )(*args, w_out, xa, mods, lnp, rw, rb)


def _proj_gla_kernel(x_ref, mod_ref, w_ref, wz_ref, wg_ref, bg_ref,
                     q_ref, k_ref, v_ref, r_ref, g_ref):
    m = mod_ref[0]
    t = (x_ref[0] * m[MOD_SC1:MOD_SC1 + 1] + m[MOD_SH1:MOD_SH1 + 1]).astype(BF16)
    c0, c1, c2, c3 = GLA_DK, 2 * GLA_DK, 2 * GLA_DK + GLA_DV, 2 * GLA_DK + 2 * GLA_DV
    q_ref[0] = jnp.dot(t, w_ref[:, :c0], preferred_element_type=F32) * (GLA_DK_HEAD ** -0.5)
    k_ref[0] = jnp.dot(t, w_ref[:, c0:c1], preferred_element_type=F32)
    v_ref[0] = jnp.dot(t, w_ref[:, c1:c2], preferred_element_type=F32).astype(BF16)
    r_ref[0] = jnp.dot(t, w_ref[:, c2:c3], preferred_element_type=F32)
    z = jnp.dot(t, wz_ref[...], preferred_element_type=F32)
    gl = jnp.dot(z, wg_ref[...], preferred_element_type=F32) + bg_ref[...]
    log_sig = jnp.minimum(gl, 0.0) - jnp.log1p(jnp.exp(-jnp.abs(gl)))
    g_ref[0] = log_sig * (1.0 / GLA_TAU)


def _proj_gla(xa, mods, w_main, wz, wg, bg):
    B, LT, D = xa.shape
    nb = LT // TM
    nbl = nb - 1
    blk = lambda w: pl.BlockSpec((1, TM, w), lambda b, i: (b, i, 0))
    return pl.pallas_call(
        _proj_gla_kernel,
        grid=(B, nb),
        in_specs=[blk(D),
                  pl.BlockSpec((1, 8, D), lambda b, i: (2 * b + i // nbl, 0, 0)),
                  pl.BlockSpec(w_main.shape, lambda b, i: (0, 0)),
                  pl.BlockSpec(wz.shape, lambda b, i: (0, 0)),
                  pl.BlockSpec(wg.shape, lambda b, i: (0, 0)),
                  pl.BlockSpec(bg.shape, lambda b, i: (0, 0))],
        out_specs=[blk(GLA_DK), blk(GLA_DK), blk(GLA_DV), blk(GLA_DV), blk(2 * GLA_DK)],
        out_shape=[jax.ShapeDtypeStruct((B, LT, GLA_DK), F32),
                   jax.ShapeDtypeStruct((B, LT, GLA_DK), F32),
                   jax.ShapeDtypeStruct((B, LT, GLA_DV), BF16),
                   jax.ShapeDtypeStruct((B, LT, GLA_DV), F32),
                   jax.ShapeDtypeStruct((B, LT, 2 * GLA_DK), F32)],
        compiler_params=_cparams(2),
        name="gla_proj",
    )(xa, mods, w_main, wz, wg, bg)


def _gla_scan_kernel(q_ref, k_ref, v_ref, g_ref, o_ref, st_sc, *, reverse):
    j = pl.program_id(1)

    @pl.when(j == 0)
    def _():
        st_sc[...] = jnp.zeros(st_sc.shape, F32)

    C = GLA_CHUNK
    row = lax.broadcasted_iota(jnp.int32, (C, C), 0)
    col = lax.broadcasted_iota(jnp.int32, (C, C), 1)
    keep = (col >= row) if reverse else (col <= row)
    tri = keep.astype(F32)
    n_chunks = TM // C
    order = range(n_chunks - 1, -1, -1) if reverse else range(n_chunks)
    for c in order:
        rows = slice(c * C, (c + 1) * C)
        for h in range(GLA_HEADS):
            ks = slice(h * GLA_DK_HEAD, (h + 1) * GLA_DK_HEAD)
            vs = slice(h * GLA_DV_HEAD, (h + 1) * GLA_DV_HEAD)
            g = g_ref[0, rows, ks]
            b = jnp.dot(tri, g, preferred_element_type=F32, precision=lax.Precision.HIGHEST)
            tot = b[0:1] if reverse else b[C - 1:C]
            q = q_ref[0, rows, ks]
            k = k_ref[0, rows, ks]
            q_in = (q * jnp.exp(b)).astype(BF16)
            k_in = (k * jnp.exp(-b)).astype(BF16)
            k_st = (k * jnp.exp(tot - b)).astype(BF16)
            att = lax.dot_general(q_in, k_in, (((1,), (1,)), ((), ())), preferred_element_type=F32)
            att = jnp.where(keep, att, 0.0).astype(BF16)
            v = v_ref[0, rows, vs]
            st = st_sc[h]
            o = jnp.dot(att, v, preferred_element_type=F32)
            o = o + lax.dot_general(q_in, st.astype(BF16), (((1,), (1,)), ((), ())),
                                    preferred_element_type=F32)
            o_ref[0, rows, vs] = o
            ds = lax.dot_general(v, k_st, (((0,), (0,)), ((), ())), preferred_element_type=F32)
            st_sc[h] = st * jnp.exp(tot) + ds


def _gla_scan(q, k, v, g, reverse):
    B, LT, _ = q.shape
    nb = LT // TM
    ctx_blk = nb - 1
    if reverse:
        order = lambda j: jnp.where(j == 0, ctx_blk, ctx_blk - j)
    else:
        order = lambda j: jnp.where(j == 0, ctx_blk, j - 1)
    gcol = 1 if reverse else 0
    return pl.pallas_call(
        functools.partial(_gla_scan_kernel, reverse=reverse),
        grid=(B, nb),
        in_specs=[pl.BlockSpec((1, TM, GLA_DK), lambda b, j: (b, order(j), 0)),
                  pl.BlockSpec((1, TM, GLA_DK), lambda b, j: (b, order(j), 0)),
                  pl.BlockSpec((1, TM, GLA_DV), lambda b, j: (b, order(j), 0)),
                  pl.BlockSpec((1, TM, GLA_DK), lambda b, j: (b, order(j), gcol))],
        out_specs=pl.BlockSpec((1, TM, GLA_DV), lambda b, j: (b, order(j), 0)),
        out_shape=jax.ShapeDtypeStruct((B, LT, GLA_DV), F32),
        scratch_shapes=[pltpu.VMEM((GLA_HEADS, GLA_DV_HEAD, GLA_DK_HEAD), F32)],
        compiler_params=_cparams(2),
        name="gla_scan_bwd" if reverse else "gla_scan_fwd",
    )(q, k, v, g)


def _moe_kernel(be_ref, nu_ref, x_ref, wgu_ref, bgu_ref, wd_ref, bd_ref, gate_ref, y_ref):
    i = pl.program_id(0)

    @pl.when(i < nu_ref[0])
    def _():
        gu = jnp.dot(x_ref[...], wgu_ref[0], preferred_element_type=F32) + bgu_ref[0]
        half = gu.shape[1] // 2
        glu = jnp.minimum(gu[:, :half], SWIGLU_LIMIT)
        lin = jnp.clip(gu[:, half:], -SWIGLU_LIMIT, SWIGLU_LIMIT)
        act = glu * jax.nn.sigmoid(SWIGLU_ALPHA * glu) * (lin + 1.0)
        y = jnp.dot(act.astype(BF16), wd_ref[0], preferred_element_type=F32) + bd_ref[0]
        y_ref[...] = y * gate_ref[...]

    @pl.when(i >= nu_ref[0])
    def _():
        y_ref[...] = jnp.zeros(y_ref.shape, F32)


def _moe_experts(block_expert, n_used, xb, w_gu, b_gu, w_down, b_down, row_gate):
    n_rows, D = xb.shape
    n_blocks = n_rows // MOE_BLOCK
    grid_spec = pltpu.PrefetchScalarGridSpec(
        num_scalar_prefetch=2,
        grid=(n_blocks,),
        in_specs=[pl.BlockSpec((MOE_BLOCK, D), lambda i, be, nu: (i, 0)),
                  pl.BlockSpec((1, D, 2 * D), lambda i, be, nu: (be[i], 0, 0)),
                  pl.BlockSpec((1, 1, 2 * D), lambda i, be, nu: (be[i], 0, 0)),
                  pl.BlockSpec((1, D, D), lambda i, be, nu: (be[i], 0, 0)),
                  pl.BlockSpec((1, 1, D), lambda i, be, nu: (be[i], 0, 0)),
                  pl.BlockSpec((MOE_BLOCK, 1), lambda i, be, nu: (i, 0))],
        out_specs=pl.BlockSpec((MOE_BLOCK, D), lambda i, be, nu: (i, 0)),
    )
    return pl.pallas_call(
        _moe_kernel,
        grid_spec=grid_spec,
        out_shape=jax.ShapeDtypeStruct((n_rows, D), F32),
        compiler_params=_cparams(1),
        name="moe_experts",
    )(block_expert, n_used, xb, w_gu, b_gu.reshape(N_EXPERTS, 1, 2 * D), w_down,
      b_down.reshape(N_EXPERTS, 1, D), row_gate.reshape(n_rows, 1))


def _route(logits):
    T = logits.shape[0]
    top_logit, top_idx = lax.top_k(logits, TOP_K)
    gates = jax.nn.softmax(top_logit, axis=-1)
    n_assign = T * TOP_K
    flat_e = top_idx.reshape(-1)
    order = jnp.argsort(flat_e)
    sorted_e = flat_e[order]
    counts = jnp.bincount(flat_e, length=N_EXPERTS)
    padded = (counts + MOE_BLOCK - 1) // MOE_BLOCK * MOE_BLOCK
    start = jnp.cumsum(counts) - counts
    padded_end = jnp.cumsum(padded)
    padded_start = padded_end - padded
    dest = (padded_start[sorted_e] + jnp.arange(n_assign) - start[sorted_e]).astype(jnp.int32)
    n_blocks = -(-n_assign // MOE_BLOCK) + N_EXPERTS
    n_rows = n_blocks * MOE_BLOCK
    row_tok = jnp.full((n_rows,), T, jnp.int32).at[dest].set((order // TOP_K).astype(jnp.int32))
    row_gate = jnp.zeros((n_rows,), F32).at[dest].set(gates.reshape(-1)[order])
    block_expert = jnp.minimum(
        jnp.searchsorted(padded_end, jnp.arange(n_blocks) * MOE_BLOCK, side='right'),
        N_EXPERTS - 1).astype(jnp.int32)
    pos = jnp.zeros((n_assign,), jnp.int32).at[order].set(dest)
    n_used = (padded_end[-1:] // MOE_BLOCK).astype(jnp.int32)
    return row_tok, row_gate, block_expert, n_used, pos


def _moe_ffn(u, logits, w_gu, b_gu, w_down, b_down):
    T, D = u.shape
    row_tok, row_gate, block_expert, n_used, pos = _route(logits)
    u_pad = jnp.concatenate([u, jnp.zeros((1, D), u.dtype)], axis=0)
    xb = u_pad[row_tok]
    yb = _moe_experts(block_expert, n_used, xb, w_gu, b_gu, w_down, b_down, row_gate)
    return yb[pos].reshape(T, TOP_K, D).sum(axis=1)


def _final_ln_kernel(x_ref, f_ref, mod_ref, ln_ref, o_ref):
    m = mod_ref[0]
    z = DEEPNORM_ALPHA * x_ref[0] + m[MOD_G2:MOD_G2 + 1] * f_ref[0]
    mu = jnp.mean(z, axis=-1, keepdims=True)
    zc = z - mu
    y = zc * lax.rsqrt(jnp.mean(zc * zc, axis=-1, keepdims=True) + NORM_EPS)
    o_ref[0] = y * ln_ref[0:1] + ln_ref[1:2]


def _final_ln(x1, f, mods, lnp, nbl):
    B, L, D = x1.shape
    blk = pl.BlockSpec((1, TM, D), lambda b, i: (b, i, 0))
    return pl.pallas_call(
        _final_ln_kernel,
        grid=(B, L // TM),
        in_specs=[blk, blk,
                  pl.BlockSpec((1, 8, D), lambda b, i: (2 * b + i // nbl, 0, 0)),
                  pl.BlockSpec((2, D), lambda b, i: (0, 0))],
        out_specs=blk,
        out_shape=jax.ShapeDtypeStruct((B, L, D), F32),
        compiler_params=_cparams(2),
        name="final_ln",
    )(x1, f, mods, lnp)


def _rope_tables(S, n_ctx):
    rows = S // GRID_W
    row = jnp.repeat(jnp.arange(rows), GRID_W).astype(F32)
    col = jnp.tile(jnp.arange(GRID_W), rows).astype(F32)
    inv = ROPE_BASE ** (-jnp.arange(ROPE_PAIRS_AXIS, dtype=F32) / ROPE_PAIRS_AXIS)
    ang = jnp.concatenate([row[:, None] * inv, col[:, None] * inv], -1)
    cos, sin = jnp.cos(ang), jnp.sin(ang)
    cos = jnp.concatenate([cos, jnp.ones((n_ctx, cos.shape[1]), F32)], axis=0)
    sin = jnp.concatenate([sin, jnp.zeros((n_ctx, sin.shape[1]), F32)], axis=0)
    return (jnp.concatenate([cos, cos, cos, cos], axis=1),
            jnp.concatenate([-sin, -sin, sin, sin], axis=1))


def _qk_column_perm():
    lane = np.arange(DA_HEAD_W)
    half, mp, jj = lane // 64, (lane % 64) // 32, lane % 32
    src = mp * DA_HEAD_DIM + half * 32 + jj
    head = np.arange(DA_HEADS)[:, None] * DA_HEAD_W
    perm = (head + src[None, :]).reshape(-1)
    return np.concatenate([perm, D_MODEL + perm])


def _layer_mods(mod_rows, B):
    D = D_MODEL
    parts = mod_rows.reshape(8, 6, D)
    sh1, sc1, g1, sh2, sc2, g2 = (parts[:, n] for n in range(6))
    tab = jnp.stack([1.0 + sc1, sh1, g1, 1.0 + sc2, sh2, g2, jnp.zeros_like(g1),
                     jnp.zeros_like(g1)], axis=1)
    lat = tab[:B]
    ctx = jnp.broadcast_to(tab[B:B + 1], (B, 8, D))
    return jnp.stack([lat, ctx], axis=1).reshape(2 * B, 8, D)


def kernel(x, c, ctx, c_ctx, ada_w, ada_b, ln_g, ln_b, da_w_in, da_w_out, da_lambda, da_subln_w,
           gla_w_in, gla_w_gate, gla_b_gate, gla_norm_w, gla_w_out, router_w, router_b,
           moe_w_gu, moe_b_gu, moe_w_down, moe_b_down):
    B, S, D = x.shape
    n_ctx = ctx.shape[1]
    assert D == D_MODEL and n_ctx == TM and S % TM == 0 and S % GRID_W == 0 and B + 1 <= 8
    nbl = S // TM
    nb = nbl + 1

    cc = jnp.concatenate([c, c_ctx[None, :], jnp.zeros((8 - B - 1, D), F32)], axis=0)
    mod_all = _ada_mods(cc, ada_w, ada_b)
    xa = jnp.concatenate([x, ctx], axis=1)

    rw = [jnp.pad(router_w[i], ((0, 0), (0, LANES - N_EXPERTS))) for i in range(DEPTH)]
    rb = [jnp.pad(router_b[i], (0, LANES - N_EXPERTS)).reshape(1, LANES) for i in range(DEPTH)]
    lnp = [[jnp.stack([ln_g[i, n], ln_b[i, n]]) for n in range(2)] for i in range(DEPTH)]

    mods = _layer_mods(mod_all[0], B)
    w_in = da_w_in[0]
    wqk = w_in[:, _qk_column_perm()].astype(BF16)
    wv = w_in[:, 2 * D:].astype(BF16)
    cos, sin = _rope_tables(S, n_ctx)
    q, k, v = _proj_da(xa, mods, cos, sin, wqk, wv)
    lam_init = _lambda_init(0)
    lv = da_lambda[0].astype(F32)
    lam = (jnp.exp(jnp.sum(lv[0] * lv[1])) - jnp.exp(jnp.sum(lv[2] * lv[3])) + lam_init).reshape(1)
    o = _diff_attention(lam, q, k, v, da_subln_w[0], lam_init)
    x1, u, lg = _mixer_out("da", [o], da_w_out[0].astype(BF16), xa, mods, lnp[0][0],
                           rw[0], rb[0], nb)
    f = _moe_ffn(u.reshape(B * nb * TM, D), lg.reshape(B * nb * TM, LANES)[:, :N_EXPERTS],
                 moe_w_gu[0].astype(BF16), moe_b_gu[0], moe_w_down[0].astype(BF16), moe_b_down[0])
    xa = _final_ln(x1, f.reshape(B, nb * TM, D), mods, lnp[0][1], nbl)

    mods = _layer_mods(mod_all[1], B)
    gw = gla_w_in[0]
    c3 = 2 * GLA_DK + 2 * GLA_DV
    w_main = gw[:, :c3].astype(BF16)
    wz = jnp.pad(gw[:, c3:], ((0, 0), (0, LANES - 2 * GLA_GATE_RANK))).astype(BF16)
    wg = jnp.zeros((LANES, 2 * GLA_DK), F32)
    wg = wg.at[:GLA_GATE_RANK, :GLA_DK].set(gla_w_gate[0, 0])
    wg = wg.at[GLA_GATE_RANK:2 * GLA_GATE_RANK, GLA_DK:].set(gla_w_gate[0, 1])
    bg = gla_b_gate[0].reshape(1, 2 * GLA_DK)
    gq, gk, gv, gr, gg = _proj_gla(xa, mods, w_main, wz, wg, bg)
    of = _gla_scan(gq, gk, gv, gg, reverse=False)
    ob = _gla_scan(gq, gk, gv, gg, reverse=True)
    x1, u, lg = _mixer_out("gla", [of, ob, gr], gla_w_out[0].astype(BF16), xa, mods, lnp[1][0],
                           rw[1], rb[1], nbl, norm_w=gla_norm_w[0])
    f = _moe_ffn(u.reshape(B * S, D), lg.reshape(B * S, LANES)[:, :N_EXPERTS],
                 moe_w_gu[1].astype(BF16), moe_b_gu[1], moe_w_down[1].astype(BF16), moe_b_down[1])
    return _final_ln(x1, f.reshape(B, S, D), mods, lnp[1][1], nbl)
```

```python
import functools
import math

import numpy as np
import jax
import jax.numpy as jnp
from jax import lax
from jax.experimental import pallas as pl
from jax.experimental.pallas import tpu as pltpu

F32 = jnp.float32
BF16 = jnp.bfloat16

D_MODEL = 1024
DEPTH = 2
GRID_W = 64

DA_HEADS = 8
DA_HEAD_DIM = 64
DA_HEAD_W = 2 * DA_HEAD_DIM
ROPE_BASE = 10000.0
ROPE_PAIRS_AXIS = DA_HEAD_DIM // 4

GLA_HEADS = 4
GLA_DK = D_MODEL // 2
GLA_DV = D_MODEL
GLA_DK_HEAD = GLA_DK // GLA_HEADS
GLA_DV_HEAD = GLA_DV // GLA_HEADS
GLA_GATE_RANK = 16
GLA_TAU = 16.0
GLA_CHUNK = 64

N_EXPERTS = 32
TOP_K = 4
SWIGLU_ALPHA = 1.702
SWIGLU_LIMIT = 7.0
MOE_BLOCK = 256

DEEPNORM_ALPHA = (2.0 * DEPTH) ** 0.25
NORM_EPS = 1e-5

LANES = 128
TM = 256
ATT_TK_MAX = 2816
VMEM_LIMIT = 48 * 1024 * 1024

MOD_SC1, MOD_SH1, MOD_G1, MOD_SC2, MOD_SH2, MOD_G2 = range(6)


def _cparams(n_axes):
    return pltpu.CompilerParams(dimension_semantics=("arbitrary",) * n_axes,
                                vmem_limit_bytes=VMEM_LIMIT)


def _lambda_init(layer_idx):
    return 0.8 - 0.6 * math.exp(-0.3 * layer_idx)


def _ada_kernel(c_ref, w_ref, b_ref, o_ref):
    c = c_ref[...]
    s = c * jax.nn.sigmoid(c)
    o_ref[0] = jnp.dot(s, w_ref[0], preferred_element_type=F32) + b_ref[0]


def _ada_mods(cc, ada_w, ada_b):
    nt = 1536
    n6 = 6 * D_MODEL
    return pl.pallas_call(
        _ada_kernel,
        grid=(DEPTH, n6 // nt),
        in_specs=[pl.BlockSpec((8, D_MODEL), lambda l, j: (0, 0)),
                  pl.BlockSpec((1, D_MODEL, nt), lambda l, j: (l, 0, j)),
                  pl.BlockSpec((1, 1, nt), lambda l, j: (l, 0, j))],
        out_specs=pl.BlockSpec((1, 8, nt), lambda l, j: (l, 0, j)),
        out_shape=jax.ShapeDtypeStruct((DEPTH, 8, n6), F32),
        compiler_params=_cparams(2),
        name="ada_mods",
    )(cc, ada_w, ada_b.reshape(DEPTH, 1, n6))


def _proj_da_kernel(x_ref, mod_ref, cos_ref, sin_ref, wqk_ref, wv_ref, q_ref, k_ref, v_ref):
    m = mod_ref[0]
    t = (x_ref[0] * m[MOD_SC1:MOD_SC1 + 1] + m[MOD_SH1:MOD_SH1 + 1]).astype(BF16)
    cos = cos_ref[...]
    sin = sin_ref[...]
    q_scale = DA_HEAD_DIM ** -0.5 * math.log2(math.e)
    for j in range(DA_HEADS):
        y2 = jnp.dot(t, wqk_ref[:, j * 256:(j + 1) * 256], preferred_element_type=F32)
        for hh in range(2):
            y = y2[:, hh * LANES:(hh + 1) * LANES]
            y = y * cos + pltpu.roll(y, 64, 1) * sin
            col = (2 * j + hh) * LANES
            if col < D_MODEL:
                q_ref[0, :, col:col + LANES] = (y * q_scale).astype(BF16)
            else:
                k_ref[0, :, col - D_MODEL:col - D_MODEL + LANES] = y.astype(BF16)
    v_ref[0] = jnp.dot(t, wv_ref[...], preferred_element_type=F32).astype(BF16)


def _proj_da(xa, mods, cos, sin, wqk, wv):
    B, LT, D = xa.shape
    nb = LT // TM
    nbl = nb - 1
    out = jax.ShapeDtypeStruct((B, LT, D), BF16)
    blk = pl.BlockSpec((1, TM, D), lambda b, i: (b, i, 0))
    return pl.pallas_call(
        _proj_da_kernel,
        grid=(B, nb),
        in_specs=[blk,
                  pl.BlockSpec((1, 8, D), lambda b, i: (2 * b + i // nbl, 0, 0)),
                  pl.BlockSpec((TM, LANES), lambda b, i: (i, 0)),
                  pl.BlockSpec((TM, LANES), lambda b, i: (i, 0)),
                  pl.BlockSpec((D, 2 * D), lambda b, i: (0, 0)),
                  pl.BlockSpec((D, D), lambda b, i: (0, 0))],
        out_specs=[blk, blk, blk],
        out_shape=[out, out, out],
        compiler_params=_cparams(2),
        name="da_proj",
    )(xa, mods, cos, sin, wqk, wv)


def _attn_kernel(lam_ref, q_ref, k_ref, v_ref, sw_ref, o_ref, vext_sc, m_sc, acc_sc, s_sc, *,
                 n_lat, n_ctx, lam_init):
    i = pl.program_id(2)
    tq = q_ref.shape[1]

    @pl.when(i == 0)
    def _():
        vext_sc[:, :DA_HEAD_W] = v_ref[0]
        vext_sc[:, DA_HEAD_W:] = jnp.ones((vext_sc.shape[0], DA_HEAD_W), BF16)

    q = q_ref[0]
    lane = lax.broadcasted_iota(jnp.int32, (1, DA_HEAD_W), 1)
    map0 = (lane % 64) < 32
    zero = jnp.zeros_like(q)
    qs = jnp.concatenate([jnp.where(map0, q, zero), jnp.where(map0, zero, q)], axis=0)
    m_sc[...] = jnp.full(m_sc.shape, -jnp.inf, F32)
    acc_sc[...] = jnp.zeros(acc_sc.shape, F32)

    def scores(off, tk):
        k = k_ref[0, pl.ds(off, tk), :]
        return lax.dot_general(qs, k, (((1,), (1,)), ((), ())), preferred_element_type=F32)

    def accumulate(s, off, tk):
        m_prev = m_sc[...]
        m_new = jnp.maximum(m_prev, jnp.max(s, axis=1, keepdims=True))
        alpha = jnp.exp2(m_prev - m_new)
        p = jnp.exp2(s - pltpu.repeat(m_new, tk // LANES, 1))
        pv = jnp.dot(p.astype(BF16), vext_sc[pl.ds(off, tk), :], preferred_element_type=F32)
        acc_sc[...] = pltpu.repeat(alpha, 2, 1) * acc_sc[...] + pv
        m_sc[...] = m_new

    n_q_lat = n_lat // tq
    tk = s_sc.shape[2]
    n_steps = (n_lat + n_ctx) // tk

    @pl.when(i < n_q_lat)
    def _():
        s_sc[0] = scores(0, tk)
        for t in range(n_steps):
            if t + 1 < n_steps:
                s_sc[(t + 1) % 2] = scores((t + 1) * tk, tk)
            accumulate(s_sc[t % 2], t * tk, tk)

    @pl.when(i >= n_q_lat)
    def _():
        accumulate(scores(n_lat, n_ctx), n_lat, n_ctx)

    acc = acc_sc[...]
    o0 = acc[:tq, :DA_HEAD_W] / acc[:tq, DA_HEAD_W:DA_HEAD_W + 1]
    o1 = acc[tq:, :DA_HEAD_W] / acc[tq:, DA_HEAD_W:DA_HEAD_W + 1]
    o = o0 - lam_ref[0] * o1
    o = o * lax.rsqrt(jnp.mean(o * o, axis=-1, keepdims=True) + NORM_EPS)
    o_ref[0] = (o * sw_ref[...] * (1.0 - lam_init)).astype(BF16)


def _diff_attention(lam, q, k, v, subln_w, lam_init):
    B, LT, D = q.shape
    nb = LT // TM
    tk = max(t for t in range(TM, ATT_TK_MAX + 1, TM) if LT % t == 0)
    kern = functools.partial(_attn_kernel, n_lat=LT - TM, n_ctx=TM, lam_init=lam_init)
    grid_spec = pltpu.PrefetchScalarGridSpec(
        num_scalar_prefetch=1,
        grid=(B, DA_HEADS, nb),
        in_specs=[pl.BlockSpec((1, TM, DA_HEAD_W), lambda b, h, i, lam: (b, i, h)),
                  pl.BlockSpec((1, LT, DA_HEAD_W), lambda b, h, i, lam: (b, 0, h)),
                  pl.BlockSpec((1, LT, DA_HEAD_W), lambda b, h, i, lam: (b, 0, h)),
                  pl.BlockSpec((1, DA_HEAD_W), lambda b, h, i, lam: (0, 0))],
        out_specs=pl.BlockSpec((1, TM, DA_HEAD_W), lambda b, h, i, lam: (b, i, h)),
        scratch_shapes=[pltpu.VMEM((LT, 2 * DA_HEAD_W), BF16),
                        pltpu.VMEM((2 * TM, LANES), F32),
                        pltpu.VMEM((2 * TM, 2 * DA_HEAD_W), F32),
                        pltpu.VMEM((2, 2 * TM, tk), F32)],
    )
    return pl.pallas_call(
        kern,
        grid_spec=grid_spec,
        out_shape=jax.ShapeDtypeStruct((B, LT, D), BF16),
        compiler_params=_cparams(3),
        name="diff_attn",
    )(lam, q, k, v, subln_w.reshape(1, DA_HEAD_W))


def _post_mixer(pre, w_ref, x_ref, mod_ref, ln_ref, rw_ref, rb_ref, x1_ref, u_ref, lg_ref):
    m = mod_ref[0]
    y = jnp.dot(pre, w_ref[...], preferred_element_type=F32)
    z = DEEPNORM_ALPHA * x_ref[0] + m[MOD_G1:MOD_G1 + 1] * y
    mu = jnp.mean(z, axis=-1, keepdims=True)
    zc = z - mu
    x1 = zc * lax.rsqrt(jnp.mean(zc * zc, axis=-1, keepdims=True) + NORM_EPS)
    x1 = x1 * ln_ref[0:1] + ln_ref[1:2]
    x1_ref[0] = x1
    u = x1 * m[MOD_SC2:MOD_SC2 + 1] + m[MOD_SH2:MOD_SH2 + 1]
    u_ref[0] = u.astype(BF16)
    lg_ref[0] = jnp.dot(u, rw_ref[...], preferred_element_type=F32,
                        precision=lax.Precision.HIGHEST) + rb_ref[...]


def _out_da_kernel(o_ref, w_ref, x_ref, mod_ref, ln_ref, rw_ref, rb_ref, x1_ref, u_ref, lg_ref):
    _post_mixer(o_ref[0], w_ref, x_ref, mod_ref, ln_ref, rw_ref, rb_ref, x1_ref, u_ref, lg_ref)


def _out_gla_kernel(of_ref, ob_ref, r_ref, nw_ref, w_ref, x_ref, mod_ref, ln_ref, rw_ref, rb_ref,
                    x1_ref, u_ref, lg_ref):
    parts = []
    for h in range(GLA_HEADS):
        sl = slice(h * GLA_DV_HEAD, (h + 1) * GLA_DV_HEAD)
        o = of_ref[0, :, sl] + ob_ref[0, :, sl]
        o = o * lax.rsqrt(jnp.mean(o * o, axis=-1, keepdims=True) + NORM_EPS) * nw_ref[...]
        r = r_ref[0, :, sl]
        parts.append((o * (r * jax.nn.sigmoid(r))).astype(BF16))
    pre = jnp.concatenate(parts, axis=1)
    _post_mixer(pre, w_ref, x_ref, mod_ref, ln_ref, rw_ref, rb_ref, x1_ref, u_ref, lg_ref)


def _mixer_out(kind, acts, w_out, xa, mods, lnp, rw, rb, nb_out, norm_w=None):
    B, LT, D = xa.shape
    nbl = LT // TM - 1
    blk = pl.BlockSpec((1, TM, D), lambda b, i: (b, i, 0))
    common_specs = [pl.BlockSpec((D, D), lambda b, i: (0, 0)),
                    blk,
                    pl.BlockSpec((1, 8, D), lambda b, i: (2 * b + i // nbl, 0, 0)),
                    pl.BlockSpec((2, D), lambda b, i: (0, 0)),
                    pl.BlockSpec((D, LANES), lambda b, i: (0, 0)),
                    pl.BlockSpec((1, LANES), lambda b, i: (0, 0))]
    lout = nb_out * TM
    out_shape = [jax.ShapeDtypeStruct((B, lout, D), F32),
                 jax.ShapeDtypeStruct((B, lout, D), BF16),
                 jax.ShapeDtypeStruct((B, lout, LANES), F32)]
    out_specs = [blk, blk, pl.BlockSpec((1, TM, LANES), lambda b, i: (b, i, 0))]
    if kind == "da":
        kern = _out_da_kernel
        in_specs = [blk] + common_specs
        args = list(acts)
    else:
        kern = _out_gla_kernel
        in_specs = [blk, blk, blk, pl.BlockSpec((1, GLA_DV_HEAD), lambda b, i: (0, 0))] + common_specs
        args = list(acts) + [norm_w.reshape(1, GLA_DV_HEAD)]
    return pl.pallas_call(
        kern,
        grid=(B, nb_out),
        in_specs=in_specs,
        out_specs=out_specs,
        out_shape=out_shape,
        compiler_params=_cparams(2),
        name="mixer_out_" + kind,
    )(*args, w_out, xa, mods, lnp, rw, rb)


def _proj_gla_kernel(x_ref, mod_ref, w_ref, wz_ref, wg_ref, bg_ref,
                     q_ref, k_ref, v_ref, r_ref, g_ref):
    m = mod_ref[0]
    t = (x_ref[0] * m[MOD_SC1:MOD_SC1 + 1] + m[MOD_SH1:MOD_SH1 + 1]).astype(BF16)
    c0, c1, c2, c3 = GLA_DK, 2 * GLA_DK, 2 * GLA_DK + GLA_DV, 2 * GLA_DK + 2 * GLA_DV
    q_ref[0] = jnp.dot(t, w_ref[:, :c0], preferred_element_type=F32) * (GLA_DK_HEAD ** -0.5)
    k_ref[0] = jnp.dot(t, w_ref[:, c0:c1], preferred_element_type=F32)
    v_ref[0] = jnp.dot(t, w_ref[:, c1:c2], preferred_element_type=F32).astype(BF16)
    r_ref[0] = jnp.dot(t, w_ref[:, c2:c3], preferred_element_type=F32)
    z = jnp.dot(t, wz_ref[...], preferred_element_type=F32)
    gl = jnp.dot(z, wg_ref[...], preferred_element_type=F32) + bg_ref[...]
    log_sig = jnp.minimum(gl, 0.0) - jnp.log1p(jnp.exp(-jnp.abs(gl)))
    g_ref[0] = log_sig * (1.0 / GLA_TAU)


def _proj_gla(xa, mods, w_main, wz, wg, bg):
    B, LT, D = xa.shape
    nb = LT // TM
    nbl = nb - 1
    blk = lambda w: pl.BlockSpec((1, TM, w), lambda b, i: (b, i, 0))
    return pl.pallas_call(
        _proj_gla_kernel,
        grid=(B, nb),
        in_specs=[blk(D),
                  pl.BlockSpec((1, 8, D), lambda b, i: (2 * b + i // nbl, 0, 0)),
                  pl.BlockSpec(w_main.shape, lambda b, i: (0, 0)),
                  pl.BlockSpec(wz.shape, lambda b, i: (0, 0)),
                  pl.BlockSpec(wg.shape, lambda b, i: (0, 0)),
                  pl.BlockSpec(bg.shape, lambda b, i: (0, 0))],
        out_specs=[blk(GLA_DK), blk(GLA_DK), blk(GLA_DV), blk(GLA_DV), blk(2 * GLA_DK)],
        out_shape=[jax.ShapeDtypeStruct((B, LT, GLA_DK), F32),
                   jax.ShapeDtypeStruct((B, LT, GLA_DK), F32),
                   jax.ShapeDtypeStruct((B, LT, GLA_DV), BF16),
                   jax.ShapeDtypeStruct((B, LT, GLA_DV), F32),
                   jax.ShapeDtypeStruct((B, LT, 2 * GLA_DK), F32)],
        compiler_params=_cparams(2),
        name="gla_proj",
    )(xa, mods, w_main, wz, wg, bg)


def _gla_scan_kernel(q_ref, k_ref, v_ref, g_ref, o_ref, st_sc, *, reverse):
    j = pl.program_id(1)

    @pl.when(j == 0)
    def _():
        st_sc[...] = jnp.zeros(st_sc.shape, F32)

    C = GLA_CHUNK
    row = lax.broadcasted_iota(jnp.int32, (C, C), 0)
    col = lax.broadcasted_iota(jnp.int32, (C, C), 1)
    keep = (col >= row) if reverse else (col <= row)
    tri = keep.astype(F32)
    n_chunks = TM // C
    order = range(n_chunks - 1, -1, -1) if reverse else range(n_chunks)
    for c in order:
        rows = slice(c * C, (c + 1) * C)
        for h in range(GLA_HEADS):
            ks = slice(h * GLA_DK_HEAD, (h + 1) * GLA_DK_HEAD)
            vs = slice(h * GLA_DV_HEAD, (h + 1) * GLA_DV_HEAD)
            g = g_ref[0, rows, ks]
            b = jnp.dot(tri, g, preferred_element_type=F32, precision=lax.Precision.HIGHEST)
            tot = b[0:1] if reverse else b[C - 1:C]
            q = q_ref[0, rows, ks]
            k = k_ref[0, rows, ks]
            q_in = (q * jnp.exp(b)).astype(BF16)
            k_in = (k * jnp.exp(-b)).astype(BF16)
            k_st = (k * jnp.exp(tot - b)).astype(BF16)
            att = lax.dot_general(q_in, k_in, (((1,), (1,)), ((), ())), preferred_element_type=F32)
            att = jnp.where(keep, att, 0.0).astype(BF16)
            v = v_ref[0, rows, vs]
            st = st_sc[h]
            o = jnp.dot(att, v, preferred_element_type=F32)
            o = o + lax.dot_general(q_in, st.astype(BF16), (((1,), (1,)), ((), ())),
                                    preferred_element_type=F32)
            o_ref[0, rows, vs] = o
            ds = lax.dot_general(v, k_st, (((0,), (0,)), ((), ())), preferred_element_type=F32)
            st_sc[h] = st * jnp.exp(tot) + ds


def _gla_scan(q, k, v, g, reverse):
    B, LT, _ = q.shape
    nb = LT // TM
    ctx_blk = nb - 1
    if reverse:
        order = lambda j: jnp.where(j == 0, ctx_blk, ctx_blk - j)
    else:
        order = lambda j: jnp.where(j == 0, ctx_blk, j - 1)
    gcol = 1 if reverse else 0
    return pl.pallas_call(
        functools.partial(_gla_scan_kernel, reverse=reverse),
        grid=(B, nb),
        in_specs=[pl.BlockSpec((1, TM, GLA_DK), lambda b, j: (b, order(j), 0)),
                  pl.BlockSpec((1, TM, GLA_DK), lambda b, j: (b, order(j), 0)),
                  pl.BlockSpec((1, TM, GLA_DV), lambda b, j: (b, order(j), 0)),
                  pl.BlockSpec((1, TM, GLA_DK), lambda b, j: (b, order(j), gcol))],
        out_specs=pl.BlockSpec((1, TM, GLA_DV), lambda b, j: (b, order(j), 0)),
        out_shape=jax.ShapeDtypeStruct((B, LT, GLA_DV), F32),
        scratch_shapes=[pltpu.VMEM((GLA_HEADS, GLA_DV_HEAD, GLA_DK_HEAD), F32)],
        compiler_params=_cparams(2),
        name="gla_scan_bwd" if reverse else "gla_scan_fwd",
    )(q, k, v, g)


def _moe_kernel(be_ref, nu_ref, x_ref, wgu_ref, bgu_ref, wd_ref, bd_ref, gate_ref, y_ref):
    i = pl.program_id(0)

    @pl.when(i < nu_ref[0])
    def _():
        gu = jnp.dot(x_ref[...], wgu_ref[0], preferred_element_type=F32) + bgu_ref[0]
        half = gu.shape[1] // 2
        glu = jnp.minimum(gu[:, :half], SWIGLU_LIMIT)
        lin = jnp.clip(gu[:, half:], -SWIGLU_LIMIT, SWIGLU_LIMIT)
        act = glu * jax.nn.sigmoid(SWIGLU_ALPHA * glu) * (lin + 1.0)
        y = jnp.dot(act.astype(BF16), wd_ref[0], preferred_element_type=F32) + bd_ref[0]
        y_ref[...] = y * gate_ref[...]

    @pl.when(i >= nu_ref[0])
    def _():
        y_ref[...] = jnp.zeros(y_ref.shape, F32)


def _moe_experts(block_expert, n_used, xb, w_gu, b_gu, w_down, b_down, row_gate):
    n_rows, D = xb.shape
    n_blocks = n_rows // MOE_BLOCK
    grid_spec = pltpu.PrefetchScalarGridSpec(
        num_scalar_prefetch=2,
        grid=(n_blocks,),
        in_specs=[pl.BlockSpec((MOE_BLOCK, D), lambda i, be, nu: (i, 0)),
                  pl.BlockSpec((1, D, 2 * D), lambda i, be, nu: (be[i], 0, 0)),
                  pl.BlockSpec((1, 1, 2 * D), lambda i, be, nu: (be[i], 0, 0)),
                  pl.BlockSpec((1, D, D), lambda i, be, nu: (be[i], 0, 0)),
                  pl.BlockSpec((1, 1, D), lambda i, be, nu: (be[i], 0, 0)),
                  pl.BlockSpec((MOE_BLOCK, 1), lambda i, be, nu: (i, 0))],
        out_specs=pl.BlockSpec((MOE_BLOCK, D), lambda i, be, nu: (i, 0)),
    )
    return pl.pallas_call(
        _moe_kernel,
        grid_spec=grid_spec,
        out_shape=jax.ShapeDtypeStruct((n_rows, D), F32),
        compiler_params=_cparams(1),
        name="moe_experts",
    )(block_expert, n_used, xb, w_gu, b_gu.reshape(N_EXPERTS, 1, 2 * D), w_down,
      b_down.reshape(N_EXPERTS, 1, D), row_gate.reshape(n_rows, 1))


def _route(logits):
    T = logits.shape[0]
    top_logit, top_idx = lax.top_k(logits, TOP_K)
    gates = jax.nn.softmax(top_logit, axis=-1)
    n_assign = T * TOP_K
    flat_e = top_idx.reshape(-1)
    order = jnp.argsort(flat_e)
    sorted_e = flat_e[order]
    counts = jnp.bincount(flat_e, length=N_EXPERTS)
    padded = (counts + MOE_BLOCK - 1) // MOE_BLOCK * MOE_BLOCK
    start = jnp.cumsum(counts) - counts
    padded_end = jnp.cumsum(padded)
    padded_start = padded_end - padded
    dest = (padded_start[sorted_e] + jnp.arange(n_assign) - start[sorted_e]).astype(jnp.int32)
    n_blocks = -(-n_assign // MOE_BLOCK) + N_EXPERTS
    n_rows = n_blocks * MOE_BLOCK
    row_tok = jnp.full((n_rows,), T, jnp.int32).at[dest].set((order // TOP_K).astype(jnp.int32))
    row_gate = jnp.zeros((n_rows,), F32).at[dest].set(gates.reshape(-1)[order])
    block_expert = jnp.minimum(
        jnp.searchsorted(padded_end, jnp.arange(n_blocks) * MOE_BLOCK, side='right'),
        N_EXPERTS - 1).astype(jnp.int32)
    pos = jnp.zeros((n_assign,), jnp.int32).at[order].set(dest)
    n_used = (padded_end[-1:] // MOE_BLOCK).astype(jnp.int32)
    return row_tok, row_gate, block_expert, n_used, pos


def _moe_ffn(u, logits, w_gu, b_gu, w_down, b_down):
    T, D = u.shape
    row_tok, row_gate, block_expert, n_used, pos = _route(logits)
    u_pad = jnp.concatenate([u, jnp.zeros((1, D), u.dtype)], axis=0)
    xb = u_pad[row_tok]
    yb = _moe_experts(block_expert, n_used, xb, w_gu, b_gu, w_down, b_down, row_gate)
    return yb[pos].reshape(T, TOP_K, D).sum(axis=1)


def _final_ln_kernel(x_ref, f_ref, mod_ref, ln_ref, o_ref):
    m = mod_ref[0]
    z = DEEPNORM_ALPHA * x_ref[0] + m[MOD_G2:MOD_G2 + 1] * f_ref[0]
    mu = jnp.mean(z, axis=-1, keepdims=True)
    zc = z - mu
    y = zc * lax.rsqrt(jnp.mean(zc * zc, axis=-1, keepdims=True) + NORM_EPS)
    o_ref[0] = y * ln_ref[0:1] + ln_ref[1:2]


def _final_ln(x1, f, mods, lnp, nbl):
    B, L, D = x1.shape
    blk = pl.BlockSpec((1, TM, D), lambda b, i: (b, i, 0))
    return pl.pallas_call(
        _final_ln_kernel,
        grid=(B, L // TM),
        in_specs=[blk, blk,
                  pl.BlockSpec((1, 8, D), lambda b, i: (2 * b + i // nbl, 0, 0)),
                  pl.BlockSpec((2, D), lambda b, i: (0, 0))],
        out_specs=blk,
        out_shape=jax.ShapeDtypeStruct((B, L, D), F32),
        compiler_params=_cparams(2),
        name="final_ln",
    )(x1, f, mods, lnp)


def _rope_tables(S, n_ctx):
    rows = S // GRID_W
    row = jnp.repeat(jnp.arange(rows), GRID_W).astype(F32)
    col = jnp.tile(jnp.arange(GRID_W), rows).astype(F32)
    inv = ROPE_BASE ** (-jnp.arange(ROPE_PAIRS_AXIS, dtype=F32) / ROPE_PAIRS_AXIS)
    ang = jnp.concatenate([row[:, None] * inv, col[:, None] * inv], -1)
    cos, sin = jnp.cos(ang), jnp.sin(ang)
    cos = jnp.concatenate([cos, jnp.ones((n_ctx, cos.shape[1]), F32)], axis=0)
    sin = jnp.concatenate([sin, jnp.zeros((n_ctx, sin.shape[1]), F32)], axis=0)
    return (jnp.concatenate([cos, cos, cos, cos], axis=1),
            jnp.concatenate([-sin, -sin, sin, sin], axis=1))


def _qk_column_perm():
    lane = np.arange(DA_HEAD_W)
    half, mp, jj = lane // 64, (lane % 64) // 32, lane % 32
    src = mp * DA_HEAD_DIM + half * 32 + jj
    head = np.arange(DA_HEADS)[:, None] * DA_HEAD_W
    perm = (head + src[None, :]).reshape(-1)
    return np.concatenate([perm, D_MODEL + perm])


def _layer_mods(mod_rows, B):
    D = D_MODEL
    parts = mod_rows.reshape(8, 6, D)
    sh1, sc1, g1, sh2, sc2, g2 = (parts[:, n] for n in range(6))
    tab = jnp.stack([1.0 + sc1, sh1, g1, 1.0 + sc2, sh2, g2, jnp.zeros_like(g1),
                     jnp.zeros_like(g1)], axis=1)
    lat = tab[:B]
    ctx = jnp.broadcast_to(tab[B:B + 1], (B, 8, D))
    return jnp.stack([lat, ctx], axis=1).reshape(2 * B, 8, D)


def kernel(x, c, ctx, c_ctx, ada_w, ada_b, ln_g, ln_b, da_w_in, da_w_out, da_lambda, da_subln_w,
           gla_w_in, gla_w_gate, gla_b_gate, gla_norm_w, gla_w_out, router_w, router_b,
           moe_w_gu, moe_b_gu, moe_w_down, moe_b_down):
    B, S, D = x.shape
    n_ctx = ctx.shape[1]
    assert D == D_MODEL and n_ctx == TM and S % TM == 0 and S % GRID_W == 0 and B + 1 <= 8
    nbl = S // TM
    nb = nbl + 1

    cc = jnp.concatenate([c, c_ctx[None, :], jnp.zeros((8 - B - 1, D), F32)], axis=0)
    mod_all = _ada_mods(cc, ada_w, ada_b)
    xa = jnp.concatenate([x, ctx], axis=1)

    rw = [jnp.pad(router_w[i], ((0, 0), (0, LANES - N_EXPERTS))) for i in range(DEPTH)]
    rb = [jnp.pad(router_b[i], (0, LANES - N_EXPERTS)).reshape(1, LANES) for i in range(DEPTH)]
    lnp = [[jnp.stack([ln_g[i, n], ln_b[i, n]]) for n in range(2)] for i in range(DEPTH)]

    mods = _layer_mods(mod_all[0], B)
    w_in = da_w_in[0]
    wqk = w_in[:, _qk_column_perm()].astype(BF16)
    wv = w_in[:, 2 * D:].astype(BF16)
    cos, sin = _rope_tables(S, n_ctx)
    q, k, v = _proj_da(xa, mods, cos, sin, wqk, wv)
    lam_init = _lambda_init(0)
    lv = da_lambda[0].astype(F32)
    lam = (jnp.exp(jnp.sum(lv[0] * lv[1])) - jnp.exp(jnp.sum(lv[2] * lv[3])) + lam_init).reshape(1)
    o = _diff_attention(lam, q, k, v, da_subln_w[0], lam_init)
    x1, u, lg = _mixer_out("da", [o], da_w_out[0].astype(BF16), xa, mods, lnp[0][0],
                           rw[0], rb[0], nb)
    f = _moe_ffn(u.reshape(B * nb * TM, D), lg.reshape(B * nb * TM, LANES)[:, :N_EXPERTS],
                 moe_w_gu[0].astype(BF16), moe_b_gu[0], moe_w_down[0].astype(BF16), moe_b_down[0])
    xa = _final_ln(x1, f.reshape(B, nb * TM, D), mods, lnp[0][1], nbl)

    mods = _layer_mods(mod_all[1], B)
    gw = gla_w_in[0]
    c3 = 2 * GLA_DK + 2 * GLA_DV
    w_main = gw[:, :c3].astype(BF16)
    wz = jnp.pad(gw[:, c3:], ((0, 0), (0, LANES - 2 * GLA_GATE_RANK))).astype(BF16)
    wg = jnp.zeros((LANES, 2 * GLA_DK), F32)
    wg = wg.at[:GLA_GATE_RANK, :GLA_DK].set(gla_w_gate[0, 0])
    wg = wg.at[GLA_GATE_RANK:2 * GLA_GATE_RANK, GLA_DK:].set(gla_w_gate[0, 1])
    bg = gla_b_gate[0].reshape(1, 2 * GLA_DK)
    gq, gk, gv, gr, gg = _proj_gla(xa, mods, w_main, wz, wg, bg)
    of = _gla_scan(gq, gk, gv, gg, reverse=False)
    ob = _gla_scan(gq, gk, gv, gg, reverse=True)
    x1, u, lg = _mixer_out("gla", [of, ob, gr], gla_w_out[0].astype(BF16), xa, mods, lnp[1][0],
                           rw[1], rb[1], nbl, norm_w=gla_norm_w[0])
    f = _moe_ffn(u.reshape(B * S, D), lg.reshape(B * S, LANES)[:, :N_EXPERTS],
                 moe_w_gu[1].astype(BF16), moe_b_gu[1], moe_w_down[1].astype(BF16), moe_b_down[1])
    return _final_ln(x1, f.reshape(B, S, D), mods, lnp[1][1], nbl)
```

```python
import functools
import math

import numpy as np
import jax
import jax.numpy as jnp
from jax import lax
from jax.experimental import pallas as pl
from jax.experimental.pallas import tpu as pltpu

F32 = jnp.float32
BF16 = jnp.bfloat16

D_MODEL = 1024
DEPTH = 2
GRID_W = 64

DA_HEADS = 8
DA_HEAD_DIM = 64
DA_HEAD_W = 2 * DA_HEAD_DIM
ROPE_BASE = 10000.0
ROPE_PAIRS_AXIS = DA_HEAD_DIM // 4

GLA_HEADS = 4
GLA_DK = D_MODEL // 2
GLA_DV = D_MODEL
GLA_DK_HEAD = GLA_DK // GLA_HEADS
GLA_DV_HEAD = GLA_DV // GLA_HEADS
GLA_GATE_RANK = 16
GLA_TAU = 16.0
GLA_CHUNK = 64

N_EXPERTS = 32
TOP_K = 4
SWIGLU_ALPHA = 1.702
SWIGLU_LIMIT = 7.0
MOE_BLOCK = 256

DEEPNORM_ALPHA = (2.0 * DEPTH) ** 0.25
NORM_EPS = 1e-5

LANES = 128
TM = 256
ATT_TK_MAX = 2816
VMEM_LIMIT = 48 * 1024 * 1024
MOE_VMEM_LIMIT = 56 * 1024 * 1024

MOD_SC1, MOD_SH1, MOD_G1, MOD_SC2, MOD_SH2, MOD_G2 = range(6)


def _cparams(n_axes):
    return pltpu.CompilerParams(dimension_semantics=("arbitrary",) * n_axes,
                                vmem_limit_bytes=VMEM_LIMIT)


def _lambda_init(layer_idx):
    return 0.8 - 0.6 * math.exp(-0.3 * layer_idx)


def _ada_kernel(c_ref, w_ref, b_ref, o_ref):
    c = c_ref[...]
    s = c * jax.nn.sigmoid(c)
    o_ref[0] = jnp.dot(s, w_ref[0], preferred_element_type=F32) + b_ref[0]


def _ada_mods(cc, ada_w, ada_b):
    nt = 1536
    n6 = 6 * D_MODEL
    return pl.pallas_call(
        _ada_kernel,
        grid=(DEPTH, n6 // nt),
        in_specs=[pl.BlockSpec((8, D_MODEL), lambda l, j: (0, 0)),
                  pl.BlockSpec((1, D_MODEL, nt), lambda l, j: (l, 0, j)),
                  pl.BlockSpec((1, 1, nt), lambda l, j: (l, 0, j))],
        out_specs=pl.BlockSpec((1, 8, nt), lambda l, j: (l, 0, j)),
        out_shape=jax.ShapeDtypeStruct((DEPTH, 8, n6), F32),
        compiler_params=_cparams(2),
        name="ada_mods",
    )(cc, ada_w, ada_b.reshape(DEPTH, 1, n6))


def _proj_da_kernel(x_ref, mod_ref, cos_ref, sin_ref, wqk_ref, wv_ref, q_ref, k_ref, v_ref):
    m = mod_ref[0]
    t = (x_ref[0] * m[MOD_SC1:MOD_SC1 + 1] + m[MOD_SH1:MOD_SH1 + 1]).astype(BF16)
    cos = cos_ref[...]
    sin = sin_ref[...]
    q_scale = DA_HEAD_DIM ** -0.5 * math.log2(math.e)
    for j in range(DA_HEADS):
        y2 = jnp.dot(t, wqk_ref[:, j * 256:(j + 1) * 256], preferred_element_type=F32)
        for hh in range(2):
            y = y2[:, hh * LANES:(hh + 1) * LANES]
            y = y * cos + pltpu.roll(y, 64, 1) * sin
            col = (2 * j + hh) * LANES
            if col < D_MODEL:
                q_ref[0, :, col:col + LANES] = (y * q_scale).astype(BF16)
            else:
                k_ref[0, :, col - D_MODEL:col - D_MODEL + LANES] = y.astype(BF16)
    v_ref[0] = jnp.dot(t, wv_ref[...], preferred_element_type=F32).astype(BF16)


def _proj_da(xa, mods, cos, sin, wqk, wv):
    B, LT, D = xa.shape
    nb = LT // TM
    nbl = nb - 1
    out = jax.ShapeDtypeStruct((B, LT, D), BF16)
    blk = pl.BlockSpec((1, TM, D), lambda b, i: (b, i, 0))
    return pl.pallas_call(
        _proj_da_kernel,
        grid=(B, nb),
        in_specs=[blk,
                  pl.BlockSpec((1, 8, D), lambda b, i: (2 * b + i // nbl, 0, 0)),
                  pl.BlockSpec((TM, LANES), lambda b, i: (i, 0)),
                  pl.BlockSpec((TM, LANES), lambda b, i: (i, 0)),
                  pl.BlockSpec((D, 2 * D), lambda b, i: (0, 0)),
                  pl.BlockSpec((D, D), lambda b, i: (0, 0))],
        out_specs=[blk, blk, blk],
        out_shape=[out, out, out],
        compiler_params=_cparams(2),
        name="da_proj",
    )(xa, mods, cos, sin, wqk, wv)


def _attn_kernel(lam_ref, q_ref, k_ref, v_ref, sw_ref, o_ref, vext_sc, m_sc, acc_sc, s_sc, *,
                 n_lat, n_ctx, lam_init):
    i = pl.program_id(2)
    tq = q_ref.shape[1]

    @pl.when(i == 0)
    def _():
        vext_sc[:, :DA_HEAD_W] = v_ref[0]
        vext_sc[:, DA_HEAD_W:] = jnp.ones((vext_sc.shape[0], DA_HEAD_W), BF16)

    q = q_ref[0]
    lane = lax.broadcasted_iota(jnp.int32, (1, DA_HEAD_W), 1)
    map0 = (lane % 64) < 32
    zero = jnp.zeros_like(q)
    qs = jnp.concatenate([jnp.where(map0, q, zero), jnp.where(map0, zero, q)], axis=0)
    m_sc[...] = jnp.full(m_sc.shape, -jnp.inf, F32)
    acc_sc[...] = jnp.zeros(acc_sc.shape, F32)

    def scores(off, tk):
        k = k_ref[0, pl.ds(off, tk), :]
        return lax.dot_general(qs, k, (((1,), (1,)), ((), ())), preferred_element_type=F32)

    def accumulate(s, off, tk):
        m_prev = m_sc[...]
        m_new = jnp.maximum(m_prev, jnp.max(s, axis=1, keepdims=True))
        alpha = jnp.exp2(m_prev - m_new)
        p = jnp.exp2(s - jnp.tile(m_new, (1, tk // LANES)))
        pv = jnp.dot(p.astype(BF16), vext_sc[pl.ds(off, tk), :], preferred_element_type=F32)
        acc_sc[...] = jnp.tile(alpha, (1, 2)) * acc_sc[...] + pv
        m_sc[...] = m_new

    n_q_lat = n_lat // tq
    tk = s_sc.shape[2]
    n_steps = (n_lat + n_ctx) // tk

    @pl.when(i < n_q_lat)
    def _():
        s_sc[0] = scores(0, tk)
        for t in range(n_steps):
            if t + 1 < n_steps:
                s_sc[(t + 1) % 2] = scores((t + 1) * tk, tk)
            accumulate(s_sc[t % 2], t * tk, tk)

    @pl.when(i >= n_q_lat)
    def _():
        accumulate(scores(n_lat, n_ctx), n_lat, n_ctx)

    acc = acc_sc[...]
    o0 = acc[:tq, :DA_HEAD_W] / acc[:tq, DA_HEAD_W:DA_HEAD_W + 1]
    o1 = acc[tq:, :DA_HEAD_W] / acc[tq:, DA_HEAD_W:DA_HEAD_W + 1]
    o = o0 - lam_ref[0] * o1
    o = o * lax.rsqrt(jnp.mean(o * o, axis=-1, keepdims=True) + NORM_EPS)
    o_ref[0] = (o * sw_ref[...] * (1.0 - lam_init)).astype(BF16)


def _diff_attention(lam, q, k, v, subln_w, lam_init):
    B, LT, D = q.shape
    nb = LT // TM
    tk = max(t for t in range(TM, ATT_TK_MAX + 1, TM) if LT % t == 0)
    kern = functools.partial(_attn_kernel, n_lat=LT - TM, n_ctx=TM, lam_init=lam_init)
    grid_spec = pltpu.PrefetchScalarGridSpec(
        num_scalar_prefetch=1,
        grid=(B, DA_HEADS, nb),
        in_specs=[pl.BlockSpec((1, TM, DA_HEAD_W), lambda b, h, i, lam: (b, i, h)),
                  pl.BlockSpec((1, LT, DA_HEAD_W), lambda b, h, i, lam: (b, 0, h)),
                  pl.BlockSpec((1, LT, DA_HEAD_W), lambda b, h, i, lam: (b, 0, h)),
                  pl.BlockSpec((1, DA_HEAD_W), lambda b, h, i, lam: (0, 0))],
        out_specs=pl.BlockSpec((1, TM, DA_HEAD_W), lambda b, h, i, lam: (b, i, h)),
        scratch_shapes=[pltpu.VMEM((LT, 2 * DA_HEAD_W), BF16),
                        pltpu.VMEM((2 * TM, LANES), F32),
                        pltpu.VMEM((2 * TM, 2 * DA_HEAD_W), F32),
                        pltpu.VMEM((2, 2 * TM, tk), F32)],
    )
    return pl.pallas_call(
        kern,
        grid_spec=grid_spec,
        out_shape=jax.ShapeDtypeStruct((B, LT, D), BF16),
        compiler_params=_cparams(3),
        name="diff_attn",
    )(lam, q, k, v, subln_w.reshape(1, DA_HEAD_W))


def _route_block(logits, cnt_sc):
    lane = lax.broadcasted_iota(jnp.int32, logits.shape, 1)
    work = jnp.where(lane < N_EXPERTS, logits, -jnp.inf)
    tops, idxs, hits = [], [], []
    for _ in range(TOP_K):
        mk = jnp.max(work, axis=1, keepdims=True)
        ik = jnp.min(jnp.where(work == mk, lane, LANES), axis=1, keepdims=True)
        hit = lane == ik
        tops.append(mk)
        idxs.append(ik.astype(F32))
        hits.append(hit)
        work = jnp.where(hit, -jnp.inf, work)
    chosen = functools.reduce(jnp.logical_or, hits).astype(F32)
    n = logits.shape[0]
    row = lax.broadcasted_iota(jnp.int32, (n, n), 0)
    col = lax.broadcasted_iota(jnp.int32, (n, n), 1)
    before = jnp.dot((col < row).astype(BF16), chosen.astype(BF16), preferred_element_type=F32)
    rank_all = cnt_sc[...] + before
    cnt_sc[...] = cnt_sc[...] + jnp.sum(chosen, axis=0, keepdims=True)
    exps = [jnp.exp(t - tops[0]) for t in tops]
    denom = functools.reduce(jnp.add, exps)
    table = jnp.zeros(logits.shape, F32)
    for k in range(TOP_K):
        rk = jnp.sum(jnp.where(hits[k], rank_all, 0.0), axis=1, keepdims=True)
        table = jnp.where(lane == k, idxs[k], table)
        table = jnp.where(lane == TOP_K + k, rk, table)
        table = jnp.where(lane == 2 * TOP_K + k, exps[k] / denom, table)
    return table


def _post_mixer(pre, w_ref, x_ref, mod_ref, ln_ref, rw_ref, rb_ref, x1_ref, u_ref, rt_ref, cnt_ref,
                cnt_sc):
    @pl.when((pl.program_id(0) == 0) & (pl.program_id(1) == 0))
    def _():
        cnt_sc[...] = jnp.zeros(cnt_sc.shape, F32)

    m = mod_ref[0]
    y = jnp.dot(pre, w_ref[...], preferred_element_type=F32)
    z = DEEPNORM_ALPHA * x_ref[0] + m[MOD_G1:MOD_G1 + 1] * y
    mu = jnp.mean(z, axis=-1, keepdims=True)
    zc = z - mu
    x1 = zc * lax.rsqrt(jnp.mean(zc * zc, axis=-1, keepdims=True) + NORM_EPS)
    x1 = x1 * ln_ref[0:1] + ln_ref[1:2]
    x1_ref[0] = x1
    u = x1 * m[MOD_SC2:MOD_SC2 + 1] + m[MOD_SH2:MOD_SH2 + 1]
    u_ref[0] = u.astype(BF16)
    logits = jnp.dot(u, rw_ref[...], preferred_element_type=F32,
                     precision=lax.Precision.HIGHEST) + rb_ref[...]
    rt_ref[0] = _route_block(logits, cnt_sc)
    cnt_ref[...] = jnp.broadcast_to(cnt_sc[...], cnt_ref.shape)


def _out_da_kernel(o_ref, w_ref, x_ref, mod_ref, ln_ref, rw_ref, rb_ref, x1_ref, u_ref, rt_ref,
                   cnt_ref, cnt_sc):
    _post_mixer(o_ref[0], w_ref, x_ref, mod_ref, ln_ref, rw_ref, rb_ref, x1_ref, u_ref, rt_ref,
                cnt_ref, cnt_sc)


def _out_gla_kernel(of_ref, ob_ref, r_ref, nw_ref, w_ref, x_ref, mod_ref, ln_ref, rw_ref, rb_ref,
                    x1_ref, u_ref, rt_ref, cnt_ref, cnt_sc):
    parts = []
    for h in range(GLA_HEADS):
        sl = slice(h * GLA_DV_HEAD, (h + 1) * GLA_DV_HEAD)
        o = of_ref[0, :, sl] + ob_ref[0, :, sl]
        o = o * lax.rsqrt(jnp.mean(o * o, axis=-1, keepdims=True) + NORM_EPS) * nw_ref[...]
        r = r_ref[0, :, sl]
        parts.append((o * (r * jax.nn.sigmoid(r))).astype(BF16))
    pre = jnp.concatenate(parts, axis=1)
    _post_mixer(pre, w_ref, x_ref, mod_ref, ln_ref, rw_ref, rb_ref, x1_ref, u_ref, rt_ref, cnt_ref,
                cnt_sc)


def _mixer_out(kind, acts, w_out, xa, mods, lnp, rw, rb, nb_out, norm_w=None):
    B, LT, D = xa.shape
    nbl = LT // TM - 1
    blk = pl.BlockSpec((1, TM, D), lambda b, i: (b, i, 0))
    common_specs = [pl.BlockSpec((D, D), lambda b, i: (0, 0)),
                    blk,
                    pl.BlockSpec((1, 8, D), lambda b, i: (2 * b + i // nbl, 0, 0)),
                    pl.BlockSpec((2, D), lambda b, i: (0, 0)),
                    pl.BlockSpec((D, LANES), lambda b, i: (0, 0)),
                    pl.BlockSpec((1, LANES), lambda b, i: (0, 0))]
    lout = nb_out * TM
    out_shape = [jax.ShapeDtypeStruct((B, lout, D), F32),
                 jax.ShapeDtypeStruct((B, lout, D), BF16),
                 jax.ShapeDtypeStruct((B, lout, LANES), F32),
                 jax.ShapeDtypeStruct((8, LANES), F32)]
    out_specs = [blk, blk, pl.BlockSpec((1, TM, LANES), lambda b, i: (b, i, 0)),
                 pl.BlockSpec((8, LANES), lambda b, i: (0, 0))]
    if kind == "da":
        kern = _out_da_kernel
        in_specs = [blk] + common_specs
        args = list(acts)
    else:
        kern = _out_gla_kernel
        in_specs = [blk, blk, blk, pl.BlockSpec((1, GLA_DV_HEAD), lambda b, i: (0, 0))] + common_specs
        args = list(acts) + [norm_w.reshape(1, GLA_DV_HEAD)]
    return pl.pallas_call(
        kern,
        grid=(B, nb_out),
        in_specs=in_specs,
        out_specs=out_specs,
        out_shape=out_shape,
        scratch_shapes=[pltpu.VMEM((1, LANES), F32)],
        compiler_params=_cparams(2),
        name="mixer_out_" + kind,
    )(*args, w_out, xa, mods, lnp, rw, rb)


def _proj_gla_kernel(x_ref, mod_ref, w_ref, wz_ref, wg_ref, bg_ref,
                     q_ref, k_ref, v_ref, r_ref, g_ref):
    m = mod_ref[0]
    t = (x_ref[0] * m[MOD_SC1:MOD_SC1 + 1] + m[MOD_SH1:MOD_SH1 + 1]).astype(BF16)
    c0, c1, c2, c3 = GLA_DK, 2 * GLA_DK, 2 * GLA_DK + GLA_DV, 2 * GLA_DK + 2 * GLA_DV
    q_ref[0] = jnp.dot(t, w_ref[:, :c0], preferred_element_type=F32) * (GLA_DK_HEAD ** -0.5)
    k_ref[0] = jnp.dot(t, w_ref[:, c0:c1], preferred_element_type=F32)
    v_ref[0] = jnp.dot(t, w_ref[:, c1:c2], preferred_element_type=F32).astype(BF16)
    r_ref[0] = jnp.dot(t, w_ref[:, c2:c3], preferred_element_type=F32)
    z = jnp.dot(t, wz_ref[...], preferred_element_type=F32)
    gl = jnp.dot(z, wg_ref[...], preferred_element_type=F32) + bg_ref[...]
    log_sig = jnp.minimum(gl, 0.0) - jnp.log1p(jnp.exp(-jnp.abs(gl)))
    g_ref[0] = log_sig * (1.0 / GLA_TAU)


def _proj_gla(xa, mods, w_main, wz, wg, bg):
    B, LT, D = xa.shape
    nb = LT // TM
    nbl = nb - 1
    blk = lambda w: pl.BlockSpec((1, TM, w), lambda b, i: (b, i, 0))
    return pl.pallas_call(
        _proj_gla_kernel,
        grid=(B, nb),
        in_specs=[blk(D),
                  pl.BlockSpec((1, 8, D), lambda b, i: (2 * b + i // nbl, 0, 0)),
                  pl.BlockSpec(w_main.shape, lambda b, i: (0, 0)),
                  pl.BlockSpec(wz.shape, lambda b, i: (0, 0)),
                  pl.BlockSpec(wg.shape, lambda b, i: (0, 0)),
                  pl.BlockSpec(bg.shape, lambda b, i: (0, 0))],
        out_specs=[blk(GLA_DK), blk(GLA_DK), blk(GLA_DV), blk(GLA_DV), blk(2 * GLA_DK)],
        out_shape=[jax.ShapeDtypeStruct((B, LT, GLA_DK), F32),
                   jax.ShapeDtypeStruct((B, LT, GLA_DK), F32),
                   jax.ShapeDtypeStruct((B, LT, GLA_DV), BF16),
                   jax.ShapeDtypeStruct((B, LT, GLA_DV), F32),
                   jax.ShapeDtypeStruct((B, LT, 2 * GLA_DK), F32)],
        compiler_params=_cparams(2),
        name="gla_proj",
    )(xa, mods, w_main, wz, wg, bg)


def _gla_scan_kernel(q_ref, k_ref, v_ref, g_ref, o_ref, st_sc, *, reverse):
    j = pl.program_id(1)

    @pl.when(j == 0)
    def _():
        st_sc[...] = jnp.zeros(st_sc.shape, F32)

    C = GLA_CHUNK
    row = lax.broadcasted_iota(jnp.int32, (C, C), 0)
    col = lax.broadcasted_iota(jnp.int32, (C, C), 1)
    keep = (col >= row) if reverse else (col <= row)
    tri = keep.astype(F32)
    n_chunks = TM // C
    order = range(n_chunks - 1, -1, -1) if reverse else range(n_chunks)
    for c in order:
        rows = slice(c * C, (c + 1) * C)
        for h in range(GLA_HEADS):
            ks = slice(h * GLA_DK_HEAD, (h + 1) * GLA_DK_HEAD)
            vs = slice(h * GLA_DV_HEAD, (h + 1) * GLA_DV_HEAD)
            g = g_ref[0, rows, ks]
            b = jnp.dot(tri, g, preferred_element_type=F32, precision=lax.Precision.HIGHEST)
            tot = b[0:1] if reverse else b[C - 1:C]
            q = q_ref[0, rows, ks]
            k = k_ref[0, rows, ks]
            q_in = (q * jnp.exp(b)).astype(BF16)
            k_in = (k * jnp.exp(-b)).astype(BF16)
            k_st = (k * jnp.exp(tot - b)).astype(BF16)
            att = lax.dot_general(q_in, k_in, (((1,), (1,)), ((), ())), preferred_element_type=F32)
            att = jnp.where(keep, att, 0.0).astype(BF16)
            v = v_ref[0, rows, vs]
            st = st_sc[h]
            o = jnp.dot(att, v, preferred_element_type=F32)
            o = o + lax.dot_general(q_in, st.astype(BF16), (((1,), (1,)), ((), ())),
                                    preferred_element_type=F32)
            o_ref[0, rows, vs] = o
            ds = lax.dot_general(v, k_st, (((0,), (0,)), ((), ())), preferred_element_type=F32)
            st_sc[h] = st * jnp.exp(tot) + ds


def _gla_scan(q, k, v, g, reverse):
    B, LT, _ = q.shape
    nb = LT // TM
    ctx_blk = nb - 1
    if reverse:
        order = lambda j: jnp.where(j == 0, ctx_blk, ctx_blk - j)
    else:
        order = lambda j: jnp.where(j == 0, ctx_blk, j - 1)
    gcol = 1 if reverse else 0
    return pl.pallas_call(
        functools.partial(_gla_scan_kernel, reverse=reverse),
        grid=(B, nb),
        in_specs=[pl.BlockSpec((1, TM, GLA_DK), lambda b, j: (b, order(j), 0)),
                  pl.BlockSpec((1, TM, GLA_DK), lambda b, j: (b, order(j), 0)),
                  pl.BlockSpec((1, TM, GLA_DV), lambda b, j: (b, order(j), 0)),
                  pl.BlockSpec((1, TM, GLA_DK), lambda b, j: (b, order(j), gcol))],
        out_specs=pl.BlockSpec((1, TM, GLA_DV), lambda b, j: (b, order(j), 0)),
        out_shape=jax.ShapeDtypeStruct((B, LT, GLA_DV), F32),
        scratch_shapes=[pltpu.VMEM((GLA_HEADS, GLA_DV_HEAD, GLA_DK_HEAD), F32)],
        compiler_params=_cparams(2),
        name="gla_scan_bwd" if reverse else "gla_scan_fwd",
    )(q, k, v, g)


def _moe_kernel(be_ref, nu_ref, x_ref, wgu_ref, bgu_ref, wd_ref, bd_ref, y_ref, wgu_sc, wd_sc):
    i = pl.program_id(0)

    @pl.when(i < nu_ref[0])
    def _():
        @pl.when((i == 0) | (be_ref[i] != be_ref[jnp.maximum(i - 1, 0)]))
        def _():
            wgu_sc[...] = wgu_ref[0].astype(BF16)
            wd_sc[...] = wd_ref[0].astype(BF16)

        gu = jnp.dot(x_ref[...], wgu_sc[...], preferred_element_type=F32) + bgu_ref[0]
        half = gu.shape[1] // 2
        glu = jnp.minimum(gu[:, :half], SWIGLU_LIMIT)
        lin = jnp.clip(gu[:, half:], -SWIGLU_LIMIT, SWIGLU_LIMIT)
        act = glu * jax.nn.sigmoid(SWIGLU_ALPHA * glu) * (lin + 1.0)
        y_ref[...] = jnp.dot(act.astype(BF16), wd_sc[...], preferred_element_type=F32) + bd_ref[0]

    @pl.when(i >= nu_ref[0])
    def _():
        y_ref[...] = jnp.zeros(y_ref.shape, F32)


def _moe_experts(block_expert, n_used, xb, w_gu, b_gu, w_down, b_down):
    n_rows, D = xb.shape
    n_blocks = n_rows // MOE_BLOCK
    grid_spec = pltpu.PrefetchScalarGridSpec(
        num_scalar_prefetch=2,
        grid=(n_blocks,),
        in_specs=[pl.BlockSpec((MOE_BLOCK, D), lambda i, be, nu: (i, 0)),
                  pl.BlockSpec((1, D, 2 * D), lambda i, be, nu: (be[i], 0, 0)),
                  pl.BlockSpec((1, 1, 2 * D), lambda i, be, nu: (be[i], 0, 0)),
                  pl.BlockSpec((1, D, D), lambda i, be, nu: (be[i], 0, 0)),
                  pl.BlockSpec((1, 1, D), lambda i, be, nu: (be[i], 0, 0))],
        out_specs=pl.BlockSpec((MOE_BLOCK, D), lambda i, be, nu: (i, 0)),
        scratch_shapes=[pltpu.VMEM((D, 2 * D), BF16), pltpu.VMEM((D, D), BF16)],
    )
    return pl.pallas_call(
        _moe_kernel,
        grid_spec=grid_spec,
        out_shape=jax.ShapeDtypeStruct((n_rows, D), F32),
        compiler_params=pltpu.CompilerParams(dimension_semantics=("arbitrary",),
                                             vmem_limit_bytes=MOE_VMEM_LIMIT),
        name="moe_experts",
    )(block_expert, n_used, xb, w_gu, b_gu.reshape(N_EXPERTS, 1, 2 * D), w_down,
      b_down.reshape(N_EXPERTS, 1, D))


def _moe_ffn(u, route, counts, w_gu, b_gu, w_down, b_down):
    T, D = u.shape
    expert = route[:, :TOP_K].astype(jnp.int32)
    rank = route[:, TOP_K:2 * TOP_K].astype(jnp.int32)
    cnt = counts[0, :N_EXPERTS].astype(jnp.int32)
    padded = (cnt + MOE_BLOCK - 1) // MOE_BLOCK * MOE_BLOCK
    padded_end = jnp.cumsum(padded)
    padded_start = padded_end - padded
    pos = (padded_start[expert] + rank).T
    n_blocks = -(-T * TOP_K // MOE_BLOCK) + N_EXPERTS
    block_expert = jnp.minimum(
        jnp.searchsorted(padded_end, jnp.arange(n_blocks) * MOE_BLOCK, side='right'),
        N_EXPERTS - 1).astype(jnp.int32)
    n_used = (padded_end[-1:] // MOE_BLOCK).astype(jnp.int32)
    xb = jnp.zeros((n_blocks * MOE_BLOCK, D), u.dtype)
    for k in range(TOP_K):
        xb = xb.at[pos[k]].set(u, unique_indices=True)
    yb = _moe_experts(block_expert, n_used, xb, w_gu, b_gu, w_down, b_down)
    return yb[pos.reshape(-1)].reshape(TOP_K, T, D)


def _final_ln_kernel(x_ref, y_ref, rt_ref, mod_ref, ln_ref, o_ref):
    m = mod_ref[0]
    rt = rt_ref[0]
    f = rt[:, 2 * TOP_K:2 * TOP_K + 1] * y_ref[0]
    for k in range(1, TOP_K):
        f = f + rt[:, 2 * TOP_K + k:2 * TOP_K + k + 1] * y_ref[k]
    z = DEEPNORM_ALPHA * x_ref[0] + m[MOD_G2:MOD_G2 + 1] * f
    mu = jnp.mean(z, axis=-1, keepdims=True)
    zc = z - mu
    y = zc * lax.rsqrt(jnp.mean(zc * zc, axis=-1, keepdims=True) + NORM_EPS)
    o_ref[0] = y * ln_ref[0:1] + ln_ref[1:2]


def _final_ln(x1, yg, route, mods, lnp, nbl):
    B, L, D = x1.shape
    nblk = L // TM
    blk = pl.BlockSpec((1, TM, D), lambda b, i: (b, i, 0))
    return pl.pallas_call(
        _final_ln_kernel,
        grid=(B, nblk),
        in_specs=[blk,
                  pl.BlockSpec((TOP_K, TM, D), lambda b, i: (0, b * nblk + i, 0)),
                  pl.BlockSpec((1, TM, LANES), lambda b, i: (b, i, 0)),
                  pl.BlockSpec((1, 8, D), lambda b, i: (2 * b + i // nbl, 0, 0)),
                  pl.BlockSpec((2, D), lambda b, i: (0, 0))],
        out_specs=blk,
        out_shape=jax.ShapeDtypeStruct((B, L, D), F32),
        compiler_params=_cparams(2),
        name="final_ln",
    )(x1, yg, route, mods, lnp)


def _rope_tables(S, n_ctx):
    rows = S // GRID_W
    row = jnp.repeat(jnp.arange(rows), GRID_W).astype(F32)
    col = jnp.tile(jnp.arange(GRID_W), rows).astype(F32)
    inv = ROPE_BASE ** (-jnp.arange(ROPE_PAIRS_AXIS, dtype=F32) / ROPE_PAIRS_AXIS)
    ang = jnp.concatenate([row[:, None] * inv, col[:, None] * inv], -1)
    cos, sin = jnp.cos(ang), jnp.sin(ang)
    cos = jnp.concatenate([cos, jnp.ones((n_ctx, cos.shape[1]), F32)], axis=0)
    sin = jnp.concatenate([sin, jnp.zeros((n_ctx, sin.shape[1]), F32)], axis=0)
    return (jnp.concatenate([cos, cos, cos, cos], axis=1),
            jnp.concatenate([-sin, -sin, sin, sin], axis=1))


def _qk_column_perm():
    lane = np.arange(DA_HEAD_W)
    half, mp, jj = lane // 64, (lane % 64) // 32, lane % 32
    src = mp * DA_HEAD_DIM + half * 32 + jj
    head = np.arange(DA_HEADS)[:, None] * DA_HEAD_W
    perm = (head + src[None, :]).reshape(-1)
    return np.concatenate([perm, D_MODEL + perm])


def _layer_mods(mod_rows, B):
    D = D_MODEL
    parts = mod_rows.reshape(8, 6, D)
    sh1, sc1, g1, sh2, sc2, g2 = (parts[:, n] for n in range(6))
    tab = jnp.stack([1.0 + sc1, sh1, g1, 1.0 + sc2, sh2, g2, jnp.zeros_like(g1),
                     jnp.zeros_like(g1)], axis=1)
    lat = tab[:B]
    ctx = jnp.broadcast_to(tab[B:B + 1], (B, 8, D))
    return jnp.stack([lat, ctx], axis=1).reshape(2 * B, 8, D)


def kernel(x, c, ctx, c_ctx, ada_w, ada_b, ln_g, ln_b, da_w_in, da_w_out, da_lambda, da_subln_w,
           gla_w_in, gla_w_gate, gla_b_gate, gla_norm_w, gla_w_out, router_w, router_b,
           moe_w_gu, moe_b_gu, moe_w_down, moe_b_down):
    B, S, D = x.shape
    n_ctx = ctx.shape[1]
    assert D == D_MODEL and n_ctx == TM and S % TM == 0 and S % GRID_W == 0 and B + 1 <= 8
    nbl = S // TM
    nb = nbl + 1

    cc = jnp.concatenate([c, c_ctx[None, :], jnp.zeros((8 - B - 1, D), F32)], axis=0)
    mod_all = _ada_mods(cc, ada_w, ada_b)
    xa = jnp.concatenate([x, ctx], axis=1)

    rw = [jnp.pad(router_w[i], ((0, 0), (0, LANES - N_EXPERTS))) for i in range(DEPTH)]
    rb = [jnp.pad(router_b[i], (0, LANES - N_EXPERTS)).reshape(1, LANES) for i in range(DEPTH)]
    lnp = [[jnp.stack([ln_g[i, n], ln_b[i, n]]) for n in range(2)] for i in range(DEPTH)]

    mods = _layer_mods(mod_all[0], B)
    w_in = da_w_in[0]
    wqk = w_in[:, _qk_column_perm()].astype(BF16)
    wv = w_in[:, 2 * D:].astype(BF16)
    cos, sin = _rope_tables(S, n_ctx)
    q, k, v = _proj_da(xa, mods, cos, sin, wqk, wv)
    lam_init = _lambda_init(0)
    lv = da_lambda[0].astype(F32)
    lam = (jnp.exp(jnp.sum(lv[0] * lv[1])) - jnp.exp(jnp.sum(lv[2] * lv[3])) + lam_init).reshape(1)
    o = _diff_attention(lam, q, k, v, da_subln_w[0], lam_init)
    x1, u, route, counts = _mixer_out("da", [o], da_w_out[0].astype(BF16), xa, mods, lnp[0][0],
                                      rw[0], rb[0], nb)
    yg = _moe_ffn(u.reshape(B * nb * TM, D), route.reshape(B * nb * TM, LANES), counts,
                  moe_w_gu[0], moe_b_gu[0], moe_w_down[0], moe_b_down[0])
    xa = _final_ln(x1, yg, route, mods, lnp[0][1], nbl)

    mods = _layer_mods(mod_all[1], B)
    gw = gla_w_in[0]
    c3 = 2 * GLA_DK + 2 * GLA_DV
    w_main = gw[:, :c3].astype(BF16)
    wz = jnp.pad(gw[:, c3:], ((0, 0), (0, LANES - 2 * GLA_GATE_RANK))).astype(BF16)
    wg = jnp.zeros((LANES, 2 * GLA_DK), F32)
    wg = wg.at[:GLA_GATE_RANK, :GLA_DK].set(gla_w_gate[0, 0])
    wg = wg.at[GLA_GATE_RANK:2 * GLA_GATE_RANK, GLA_DK:].set(gla_w_gate[0, 1])
    bg = gla_b_gate[0].reshape(1, 2 * GLA_DK)
    gq, gk, gv, gr, gg = _proj_gla(xa, mods, w_main, wz, wg, bg)
    of = _gla_scan(gq, gk, gv, gg, reverse=False)
    ob = _gla_scan(gq, gk, gv, gg, reverse=True)
    x1, u, route, counts = _mixer_out("gla", [of, ob, gr], gla_w_out[0].astype(BF16), xa, mods,
                                      lnp[1][0], rw[1], rb[1], nbl, norm_w=gla_norm_w[0])
    yg = _moe_ffn(u.reshape(B * S, D), route.reshape(B * S, LANES), counts,
                  moe_w_gu[1], moe_b_gu[1], moe_w_down[1], moe_b_down[1])
    return _final_ln(x1, yg, route, mods, lnp[1][1], nbl)
```

```python
import functools
import math

import numpy as np
import jax
import jax.numpy as jnp
from jax import lax
from jax.experimental import pallas as pl
from jax.experimental.pallas import tpu as pltpu

F32 = jnp.float32
BF16 = jnp.bfloat16

D_MODEL = 1024
DEPTH = 2
GRID_W = 64

DA_HEADS = 8
DA_HEAD_DIM = 64
DA_HEAD_W = 2 * DA_HEAD_DIM
ROPE_BASE = 10000.0
ROPE_PAIRS_AXIS = DA_HEAD_DIM // 4

GLA_HEADS = 4
GLA_DK = D_MODEL // 2
GLA_DV = D_MODEL
GLA_DK_HEAD = GLA_DK // GLA_HEADS
GLA_DV_HEAD = GLA_DV // GLA_HEADS
GLA_GATE_RANK = 16
GLA_TAU = 16.0
GLA_CHUNK = 64

N_EXPERTS = 32
TOP_K = 4
SWIGLU_ALPHA = 1.702
SWIGLU_LIMIT = 7.0
MOE_BLOCK = 256

DEEPNORM_ALPHA = (2.0 * DEPTH) ** 0.25
NORM_EPS = 1e-5

LANES = 128
TM = 256
ATT_TK_MAX = 2816
VMEM_LIMIT = 48 * 1024 * 1024
MOE_VMEM_LIMIT = 56 * 1024 * 1024

MOD_SC1, MOD_SH1, MOD_G1, MOD_SC2, MOD_SH2, MOD_G2 = range(6)


def _cparams(n_axes):
    return pltpu.CompilerParams(dimension_semantics=("arbitrary",) * n_axes,
                                vmem_limit_bytes=VMEM_LIMIT)


def _lambda_init(layer_idx):
    return 0.8 - 0.6 * math.exp(-0.3 * layer_idx)


def _ada_kernel(c_ref, w_ref, b_ref, o_ref):
    c = c_ref[...]
    s = c * jax.nn.sigmoid(c)
    o_ref[0] = jnp.dot(s, w_ref[0], preferred_element_type=F32) + b_ref[0]


def _ada_mods(cc, ada_w, ada_b):
    nt = 1536
    n6 = 6 * D_MODEL
    return pl.pallas_call(
        _ada_kernel,
        grid=(DEPTH, n6 // nt),
        in_specs=[pl.BlockSpec((8, D_MODEL), lambda l, j: (0, 0)),
                  pl.BlockSpec((1, D_MODEL, nt), lambda l, j: (l, 0, j)),
                  pl.BlockSpec((1, 1, nt), lambda l, j: (l, 0, j))],
        out_specs=pl.BlockSpec((1, 8, nt), lambda l, j: (l, 0, j)),
        out_shape=jax.ShapeDtypeStruct((DEPTH, 8, n6), F32),
        compiler_params=_cparams(2),
        name="ada_mods",
    )(cc, ada_w, ada_b.reshape(DEPTH, 1, n6))


def _proj_da_kernel(x_ref, mod_ref, cos_ref, sin_ref, wqk_ref, wv_ref, q_ref, k_ref, v_ref):
    m = mod_ref[0]
    t = (x_ref[0] * m[MOD_SC1:MOD_SC1 + 1] + m[MOD_SH1:MOD_SH1 + 1]).astype(BF16)
    cos = cos_ref[...]
    sin = sin_ref[...]
    q_scale = DA_HEAD_DIM ** -0.5 * math.log2(math.e)
    for j in range(DA_HEADS):
        y2 = jnp.dot(t, wqk_ref[:, j * 256:(j + 1) * 256], preferred_element_type=F32)
        for hh in range(2):
            y = y2[:, hh * LANES:(hh + 1) * LANES]
            y = y * cos + pltpu.roll(y, 64, 1) * sin
            col = (2 * j + hh) * LANES
            if col < D_MODEL:
                q_ref[0, :, col:col + LANES] = (y * q_scale).astype(BF16)
            else:
                k_ref[0, :, col - D_MODEL:col - D_MODEL + LANES] = y.astype(BF16)
    v_ref[0] = jnp.dot(t, wv_ref[...], preferred_element_type=F32).astype(BF16)


def _proj_da(xa, mods, cos, sin, wqk, wv):
    B, LT, D = xa.shape
    nb = LT // TM
    nbl = nb - 1
    out = jax.ShapeDtypeStruct((B, LT, D), BF16)
    blk = pl.BlockSpec((1, TM, D), lambda b, i: (b, i, 0))
    return pl.pallas_call(
        _proj_da_kernel,
        grid=(B, nb),
        in_specs=[blk,
                  pl.BlockSpec((1, 8, D), lambda b, i: (2 * b + i // nbl, 0, 0)),
                  pl.BlockSpec((TM, LANES), lambda b, i: (i, 0)),
                  pl.BlockSpec((TM, LANES), lambda b, i: (i, 0)),
                  pl.BlockSpec((D, 2 * D), lambda b, i: (0, 0)),
                  pl.BlockSpec((D, D), lambda b, i: (0, 0))],
        out_specs=[blk, blk, blk],
        out_shape=[out, out, out],
        compiler_params=_cparams(2),
        name="da_proj",
    )(xa, mods, cos, sin, wqk, wv)


def _attn_kernel(lam_ref, q_ref, k_ref, v_ref, sw_ref, o_ref, vext_sc, m_sc, acc_sc, s_sc, *,
                 n_lat, n_ctx, lam_init):
    i = pl.program_id(2)
    tq = q_ref.shape[1]

    @pl.when(i == 0)
    def _():
        vext_sc[:, :DA_HEAD_W] = v_ref[0]
        vext_sc[:, DA_HEAD_W:] = jnp.ones((vext_sc.shape[0], DA_HEAD_W), BF16)

    q = q_ref[0]
    lane = lax.broadcasted_iota(jnp.int32, (1, DA_HEAD_W), 1)
    map0 = (lane % 64) < 32
    zero = jnp.zeros_like(q)
    qs = jnp.concatenate([jnp.where(map0, q, zero), jnp.where(map0, zero, q)], axis=0)
    m_sc[...] = jnp.full(m_sc.shape, -jnp.inf, F32)
    acc_sc[...] = jnp.zeros(acc_sc.shape, F32)

    def scores(off, tk):
        k = k_ref[0, pl.ds(off, tk), :]
        return lax.dot_general(qs, k, (((1,), (1,)), ((), ())), preferred_element_type=F32)

    def accumulate(s, off, tk):
        m_prev = m_sc[...]
        m_new = jnp.maximum(m_prev, jnp.max(s, axis=1, keepdims=True))
        alpha = jnp.exp2(m_prev - m_new)
        p = jnp.exp2(s - jnp.tile(m_new, (1, tk // LANES)))
        pv = jnp.dot(p.astype(BF16), vext_sc[pl.ds(off, tk), :], preferred_element_type=F32)
        acc_sc[...] = jnp.tile(alpha, (1, 2)) * acc_sc[...] + pv
        m_sc[...] = m_new

    n_q_lat = n_lat // tq
    tk = s_sc.shape[2]
    n_steps = (n_lat + n_ctx) // tk

    @pl.when(i < n_q_lat)
    def _():
        s_sc[0] = scores(0, tk)
        for t in range(n_steps):
            if t + 1 < n_steps:
                s_sc[(t + 1) % 2] = scores((t + 1) * tk, tk)
            accumulate(s_sc[t % 2], t * tk, tk)

    @pl.when(i >= n_q_lat)
    def _():
        accumulate(scores(n_lat, n_ctx), n_lat, n_ctx)

    acc = acc_sc[...]
    o0 = acc[:tq, :DA_HEAD_W] / acc[:tq, DA_HEAD_W:DA_HEAD_W + 1]
    o1 = acc[tq:, :DA_HEAD_W] / acc[tq:, DA_HEAD_W:DA_HEAD_W + 1]
    o = o0 - lam_ref[0] * o1
    o = o * lax.rsqrt(jnp.mean(o * o, axis=-1, keepdims=True) + NORM_EPS)
    o_ref[0] = (o * sw_ref[...] * (1.0 - lam_init)).astype(BF16)


def _diff_attention(lam, q, k, v, subln_w, lam_init):
    B, LT, D = q.shape
    nb = LT // TM
    tk = max(t for t in range(TM, ATT_TK_MAX + 1, TM) if LT % t == 0)
    kern = functools.partial(_attn_kernel, n_lat=LT - TM, n_ctx=TM, lam_init=lam_init)
    grid_spec = pltpu.PrefetchScalarGridSpec(
        num_scalar_prefetch=1,
        grid=(B, DA_HEADS, nb),
        in_specs=[pl.BlockSpec((1, TM, DA_HEAD_W), lambda b, h, i, lam: (b, i, h)),
                  pl.BlockSpec((1, LT, DA_HEAD_W), lambda b, h, i, lam: (b, 0, h)),
                  pl.BlockSpec((1, LT, DA_HEAD_W), lambda b, h, i, lam: (b, 0, h)),
                  pl.BlockSpec((1, DA_HEAD_W), lambda b, h, i, lam: (0, 0))],
        out_specs=pl.BlockSpec((1, TM, DA_HEAD_W), lambda b, h, i, lam: (b, i, h)),
        scratch_shapes=[pltpu.VMEM((LT, 2 * DA_HEAD_W), BF16),
                        pltpu.VMEM((2 * TM, LANES), F32),
                        pltpu.VMEM((2 * TM, 2 * DA_HEAD_W), F32),
                        pltpu.VMEM((2, 2 * TM, tk), F32)],
    )
    return pl.pallas_call(
        kern,
        grid_spec=grid_spec,
        out_shape=jax.ShapeDtypeStruct((B, LT, D), BF16),
        compiler_params=_cparams(3),
        name="diff_attn",
    )(lam, q, k, v, subln_w.reshape(1, DA_HEAD_W))


def _route_block(logits, cnt_sc):
    lane = lax.broadcasted_iota(jnp.int32, logits.shape, 1)
    work = jnp.where(lane < N_EXPERTS, logits, -jnp.inf)
    tops, idxs, hits = [], [], []
    for _ in range(TOP_K):
        mk = jnp.max(work, axis=1, keepdims=True)
        ik = jnp.min(jnp.where(work == mk, lane, LANES), axis=1, keepdims=True)
        hit = lane == ik
        tops.append(mk)
        idxs.append(ik.astype(F32))
        hits.append(hit)
        work = jnp.where(hit, -jnp.inf, work)
    chosen = functools.reduce(jnp.logical_or, hits).astype(F32)
    n = logits.shape[0]
    row = lax.broadcasted_iota(jnp.int32, (n, n), 0)
    col = lax.broadcasted_iota(jnp.int32, (n, n), 1)
    before = jnp.dot((col < row).astype(BF16), chosen.astype(BF16), preferred_element_type=F32)
    rank_all = cnt_sc[...] + before
    cnt_sc[...] = cnt_sc[...] + jnp.sum(chosen, axis=0, keepdims=True)
    exps = [jnp.exp(t - tops[0]) for t in tops]
    denom = functools.reduce(jnp.add, exps)
    table = jnp.zeros(logits.shape, F32)
    for k in range(TOP_K):
        rk = jnp.sum(jnp.where(hits[k], rank_all, 0.0), axis=1, keepdims=True)
        table = jnp.where(lane == k, idxs[k], table)
        table = jnp.where(lane == TOP_K + k, rk, table)
        table = jnp.where(lane == 2 * TOP_K + k, exps[k] / denom, table)
    return table


def _post_mixer(pre, w_ref, x_ref, mod_ref, ln_ref, rw_ref, rb_ref, x1_ref, u_ref, rt_ref, cnt_ref,
                cnt_sc):
    @pl.when((pl.program_id(0) == 0) & (pl.program_id(1) == 0))
    def _():
        cnt_sc[...] = jnp.zeros(cnt_sc.shape, F32)

    m = mod_ref[0]
    y = jnp.dot(pre, w_ref[...], preferred_element_type=F32)
    z = DEEPNORM_ALPHA * x_ref[0] + m[MOD_G1:MOD_G1 + 1] * y
    mu = jnp.mean(z, axis=-1, keepdims=True)
    zc = z - mu
    x1 = zc * lax.rsqrt(jnp.mean(zc * zc, axis=-1, keepdims=True) + NORM_EPS)
    x1 = x1 * ln_ref[0:1] + ln_ref[1:2]
    x1_ref[0] = x1
    u = x1 * m[MOD_SC2:MOD_SC2 + 1] + m[MOD_SH2:MOD_SH2 + 1]
    u_ref[0] = u.astype(BF16)
    logits = jnp.dot(u, rw_ref[...], preferred_element_type=F32,
                     precision=lax.Precision.HIGHEST) + rb_ref[...]
    rt_ref[0] = _route_block(logits, cnt_sc)
    cnt_ref[...] = jnp.broadcast_to(cnt_sc[...], cnt_ref.shape)


def _out_da_kernel(o_ref, w_ref, x_ref, mod_ref, ln_ref, rw_ref, rb_ref, x1_ref, u_ref, rt_ref,
                   cnt_ref, cnt_sc):
    _post_mixer(o_ref[0], w_ref, x_ref, mod_ref, ln_ref, rw_ref, rb_ref, x1_ref, u_ref, rt_ref,
                cnt_ref, cnt_sc)


def _out_gla_kernel(of_ref, ob_ref, r_ref, nw_ref, w_ref, x_ref, mod_ref, ln_ref, rw_ref, rb_ref,
                    x1_ref, u_ref, rt_ref, cnt_ref, cnt_sc):
    parts = []
    for h in range(GLA_HEADS):
        sl = slice(h * GLA_DV_HEAD, (h + 1) * GLA_DV_HEAD)
        o = of_ref[0, :, sl] + ob_ref[0, :, sl]
        o = o * lax.rsqrt(jnp.mean(o * o, axis=-1, keepdims=True) + NORM_EPS) * nw_ref[...]
        r = r_ref[0, :, sl]
        parts.append((o * (r * jax.nn.sigmoid(r))).astype(BF16))
    pre = jnp.concatenate(parts, axis=1)
    _post_mixer(pre, w_ref, x_ref, mod_ref, ln_ref, rw_ref, rb_ref, x1_ref, u_ref, rt_ref, cnt_ref,
                cnt_sc)


def _mixer_out(kind, acts, w_out, xa, mods, lnp, rw, rb, nb_out, norm_w=None):
    B, LT, D = xa.shape
    nbl = LT // TM - 1
    blk = pl.BlockSpec((1, TM, D), lambda b, i: (b, i, 0))
    common_specs = [pl.BlockSpec((D, D), lambda b, i: (0, 0)),
                    blk,
                    pl.BlockSpec((1, 8, D), lambda b, i: (2 * b + i // nbl, 0, 0)),
                    pl.BlockSpec((2, D), lambda b, i: (0, 0)),
                    pl.BlockSpec((D, LANES), lambda b, i: (0, 0)),
                    pl.BlockSpec((1, LANES), lambda b, i: (0, 0))]
    lout = nb_out * TM
    out_shape = [jax.ShapeDtypeStruct((B, lout, D), F32),
                 jax.ShapeDtypeStruct((B, lout, D), BF16),
                 jax.ShapeDtypeStruct((B, lout, LANES), F32),
                 jax.ShapeDtypeStruct((8, LANES), F32)]
    out_specs = [blk, blk, pl.BlockSpec((1, TM, LANES), lambda b, i: (b, i, 0)),
                 pl.BlockSpec((8, LANES), lambda b, i: (0, 0))]
    if kind == "da":
        kern = _out_da_kernel
        in_specs = [blk] + common_specs
        args = list(acts)
    else:
        kern = _out_gla_kernel
        in_specs = [blk, blk, blk, pl.BlockSpec((1, GLA_DV_HEAD), lambda b, i: (0, 0))] + common_specs
        args = list(acts) + [norm_w.reshape(1, GLA_DV_HEAD)]
    return pl.pallas_call(
        kern,
        grid=(B, nb_out),
        in_specs=in_specs,
        out_specs=out_specs,
        out_shape=out_shape,
        scratch_shapes=[pltpu.VMEM((1, LANES), F32)],
        compiler_params=_cparams(2),
        name="mixer_out_" + kind,
    )(*args, w_out, xa, mods, lnp, rw, rb)


def _proj_gla_kernel(x_ref, mod_ref, w_ref, wz_ref, wg_ref, bg_ref,
                     q_ref, k_ref, v_ref, r_ref, g_ref):
    m = mod_ref[0]
    t = (x_ref[0] * m[MOD_SC1:MOD_SC1 + 1] + m[MOD_SH1:MOD_SH1 + 1]).astype(BF16)
    c0, c1, c2, c3 = GLA_DK, 2 * GLA_DK, 2 * GLA_DK + GLA_DV, 2 * GLA_DK + 2 * GLA_DV
    q_ref[0] = jnp.dot(t, w_ref[:, :c0], preferred_element_type=F32) * (GLA_DK_HEAD ** -0.5)
    k_ref[0] = jnp.dot(t, w_ref[:, c0:c1], preferred_element_type=F32)
    v_ref[0] = jnp.dot(t, w_ref[:, c1:c2], preferred_element_type=F32).astype(BF16)
    r_ref[0] = jnp.dot(t, w_ref[:, c2:c3], preferred_element_type=F32)
    z = jnp.dot(t, wz_ref[...], preferred_element_type=F32)
    gl = jnp.dot(z, wg_ref[...], preferred_element_type=F32) + bg_ref[...]
    log_sig = jnp.minimum(gl, 0.0) - jnp.log1p(jnp.exp(-jnp.abs(gl)))
    g_ref[0] = log_sig * (1.0 / GLA_TAU)


def _proj_gla(xa, mods, w_main, wz, wg, bg):
    B, LT, D = xa.shape
    nb = LT // TM
    nbl = nb - 1
    blk = lambda w: pl.BlockSpec((1, TM, w), lambda b, i: (b, i, 0))
    return pl.pallas_call(
        _proj_gla_kernel,
        grid=(B, nb),
        in_specs=[blk(D),
                  pl.BlockSpec((1, 8, D), lambda b, i: (2 * b + i // nbl, 0, 0)),
                  pl.BlockSpec(w_main.shape, lambda b, i: (0, 0)),
                  pl.BlockSpec(wz.shape, lambda b, i: (0, 0)),
                  pl.BlockSpec(wg.shape, lambda b, i: (0, 0)),
                  pl.BlockSpec(bg.shape, lambda b, i: (0, 0))],
        out_specs=[blk(GLA_DK), blk(GLA_DK), blk(GLA_DV), blk(GLA_DV), blk(2 * GLA_DK)],
        out_shape=[jax.ShapeDtypeStruct((B, LT, GLA_DK), F32),
                   jax.ShapeDtypeStruct((B, LT, GLA_DK), F32),
                   jax.ShapeDtypeStruct((B, LT, GLA_DV), BF16),
                   jax.ShapeDtypeStruct((B, LT, GLA_DV), F32),
                   jax.ShapeDtypeStruct((B, LT, 2 * GLA_DK), F32)],
        compiler_params=_cparams(2),
        name="gla_proj",
    )(xa, mods, w_main, wz, wg, bg)


def _gla_scan_kernel(q_ref, k_ref, v_ref, g_ref, o_ref, st_sc, *, reverse):
    j = pl.program_id(1)

    @pl.when(j == 0)
    def _():
        st_sc[...] = jnp.zeros(st_sc.shape, F32)

    C = GLA_CHUNK
    n_chunks = TM // C

    def causal(n):
        row = lax.broadcasted_iota(jnp.int32, (n, n), 0)
        col = lax.broadcasted_iota(jnp.int32, (n, n), 1)
        return (row // C == col // C) & ((col >= row) if reverse else (col <= row))

    keep = causal(C)
    tri = causal(TM).astype(BF16)
    g = g_ref[0]
    g_hi = g.astype(BF16)
    rem = g - g_hi.astype(F32)
    g_mid = rem.astype(BF16)
    g_lo = (rem - g_mid.astype(F32)).astype(BF16)
    b_all = (jnp.dot(tri, g_hi, preferred_element_type=F32)
             + jnp.dot(tri, g_mid, preferred_element_type=F32)
             + jnp.dot(tri, g_lo, preferred_element_type=F32))

    states = [st_sc[h] for h in range(GLA_HEADS)]
    order = range(n_chunks - 1, -1, -1) if reverse else range(n_chunks)
    for c in order:
        rows = slice(c * C, (c + 1) * C)
        for h in range(GLA_HEADS):
            ks = slice(h * GLA_DK_HEAD, (h + 1) * GLA_DK_HEAD)
            vs = slice(h * GLA_DV_HEAD, (h + 1) * GLA_DV_HEAD)
            b = b_all[rows, ks]
            tot = b[0:1] if reverse else b[C - 1:C]
            q = q_ref[0, rows, ks]
            k = k_ref[0, rows, ks]
            q_in = (q * jnp.exp(b)).astype(BF16)
            k_in = (k * jnp.exp(-b)).astype(BF16)
            k_st = (k * jnp.exp(tot - b)).astype(BF16)
            att = lax.dot_general(q_in, k_in, (((1,), (1,)), ((), ())), preferred_element_type=F32)
            att = jnp.where(keep, att, 0.0).astype(BF16)
            v = v_ref[0, rows, vs]
            st = states[h]
            o = jnp.dot(att, v, preferred_element_type=F32)
            o = o + lax.dot_general(q_in, st.astype(BF16), (((1,), (1,)), ((), ())),
                                    preferred_element_type=F32)
            o_ref[0, rows, vs] = o
            ds = lax.dot_general(v, k_st, (((0,), (0,)), ((), ())), preferred_element_type=F32)
            states[h] = st * jnp.exp(tot) + ds
    for h in range(GLA_HEADS):
        st_sc[h] = states[h]


def _gla_scan(q, k, v, g, reverse):
    B, LT, _ = q.shape
    nb = LT // TM
    ctx_blk = nb - 1
    if reverse:
        order = lambda j: jnp.where(j == 0, ctx_blk, ctx_blk - j)
    else:
        order = lambda j: jnp.where(j == 0, ctx_blk, j - 1)
    gcol = 1 if reverse else 0
    return pl.pallas_call(
        functools.partial(_gla_scan_kernel, reverse=reverse),
        grid=(B, nb),
        in_specs=[pl.BlockSpec((1, TM, GLA_DK), lambda b, j: (b, order(j), 0)),
                  pl.BlockSpec((1, TM, GLA_DK), lambda b, j: (b, order(j), 0)),
                  pl.BlockSpec((1, TM, GLA_DV), lambda b, j: (b, order(j), 0)),
                  pl.BlockSpec((1, TM, GLA_DK), lambda b, j: (b, order(j), gcol))],
        out_specs=pl.BlockSpec((1, TM, GLA_DV), lambda b, j: (b, order(j), 0)),
        out_shape=jax.ShapeDtypeStruct((B, LT, GLA_DV), F32),
        scratch_shapes=[pltpu.VMEM((GLA_HEADS, GLA_DV_HEAD, GLA_DK_HEAD), F32)],
        compiler_params=_cparams(2),
        name="gla_scan_bwd" if reverse else "gla_scan_fwd",
    )(q, k, v, g)


def _moe_kernel(be_ref, nu_ref, x_ref, wgu_ref, bgu_ref, wd_ref, bd_ref, y_ref, wgu_sc, wd_sc):
    i = pl.program_id(0)

    @pl.when(i < nu_ref[0])
    def _():
        @pl.when((i == 0) | (be_ref[i] != be_ref[jnp.maximum(i - 1, 0)]))
        def _():
            wgu_sc[...] = wgu_ref[0].astype(BF16)
            wd_sc[...] = wd_ref[0].astype(BF16)

        gu = jnp.dot(x_ref[...], wgu_sc[...], preferred_element_type=F32) + bgu_ref[0]
        half = gu.shape[1] // 2
        glu = jnp.minimum(gu[:, :half], SWIGLU_LIMIT)
        lin = jnp.clip(gu[:, half:], -SWIGLU_LIMIT, SWIGLU_LIMIT)
        act = glu * jax.nn.sigmoid(SWIGLU_ALPHA * glu) * (lin + 1.0)
        y_ref[...] = jnp.dot(act.astype(BF16), wd_sc[...], preferred_element_type=F32) + bd_ref[0]

    @pl.when(i >= nu_ref[0])
    def _():
        y_ref[...] = jnp.zeros(y_ref.shape, F32)


def _moe_experts(block_expert, n_used, xb, w_gu, b_gu, w_down, b_down):
    n_rows, D = xb.shape
    n_blocks = n_rows // MOE_BLOCK
    grid_spec = pltpu.PrefetchScalarGridSpec(
        num_scalar_prefetch=2,
        grid=(n_blocks,),
        in_specs=[pl.BlockSpec((MOE_BLOCK, D), lambda i, be, nu: (i, 0)),
                  pl.BlockSpec((1, D, 2 * D), lambda i, be, nu: (be[i], 0, 0)),
                  pl.BlockSpec((1, 1, 2 * D), lambda i, be, nu: (be[i], 0, 0)),
                  pl.BlockSpec((1, D, D), lambda i, be, nu: (be[i], 0, 0)),
                  pl.BlockSpec((1, 1, D), lambda i, be, nu: (be[i], 0, 0))],
        out_specs=pl.BlockSpec((MOE_BLOCK, D), lambda i, be, nu: (i, 0)),
        scratch_shapes=[pltpu.VMEM((D, 2 * D), BF16), pltpu.VMEM((D, D), BF16)],
    )
    return pl.pallas_call(
        _moe_kernel,
        grid_spec=grid_spec,
        out_shape=jax.ShapeDtypeStruct((n_rows, D), F32),
        compiler_params=pltpu.CompilerParams(dimension_semantics=("arbitrary",),
                                             vmem_limit_bytes=MOE_VMEM_LIMIT),
        name="moe_experts",
    )(block_expert, n_used, xb, w_gu, b_gu.reshape(N_EXPERTS, 1, 2 * D), w_down,
      b_down.reshape(N_EXPERTS, 1, D))


def _moe_ffn(u, route, counts, w_gu, b_gu, w_down, b_down):
    T, D = u.shape
    expert = route[:, :TOP_K].astype(jnp.int32)
    rank = route[:, TOP_K:2 * TOP_K].astype(jnp.int32)
    cnt = counts[0, :N_EXPERTS].astype(jnp.int32)
    padded = (cnt + MOE_BLOCK - 1) // MOE_BLOCK * MOE_BLOCK
    padded_end = jnp.cumsum(padded)
    padded_start = padded_end - padded
    pos = (padded_start[expert] + rank).T
    n_blocks = -(-T * TOP_K // MOE_BLOCK) + N_EXPERTS
    n_rows = n_blocks * MOE_BLOCK
    block_start = jnp.arange(n_blocks, dtype=jnp.int32) * MOE_BLOCK
    block_expert = jnp.minimum(jnp.sum(padded_end[None, :] <= block_start[:, None], axis=1),
                               N_EXPERTS - 1).astype(jnp.int32)
    n_used = (padded_end[-1:] // MOE_BLOCK).astype(jnp.int32)
    tok = jnp.tile(jnp.arange(T, dtype=jnp.int32), TOP_K)
    _, sorted_tok = lax.sort((pos.reshape(-1), tok), num_keys=1)
    row = jnp.arange(n_rows, dtype=jnp.int32)
    row_e = jnp.repeat(block_expert, MOE_BLOCK)
    row_rank = row - padded_start[row_e]
    compact = jnp.cumsum(cnt)[row_e] - cnt[row_e] + row_rank
    row_tok = jnp.where(row_rank < cnt[row_e], sorted_tok[jnp.minimum(compact, T * TOP_K - 1)], T)
    xb = jnp.concatenate([u, jnp.zeros((1, D), u.dtype)], axis=0)[row_tok]
    yb = _moe_experts(block_expert, n_used, xb, w_gu, b_gu, w_down, b_down)
    return yb[pos.reshape(-1)].reshape(TOP_K, T, D)


def _final_ln_kernel(x_ref, y_ref, rt_ref, mod_ref, ln_ref, o_ref):
    m = mod_ref[0]
    rt = rt_ref[0]
    f = rt[:, 2 * TOP_K:2 * TOP_K + 1] * y_ref[0]
    for k in range(1, TOP_K):
        f = f + rt[:, 2 * TOP_K + k:2 * TOP_K + k + 1] * y_ref[k]
    z = DEEPNORM_ALPHA * x_ref[0] + m[MOD_G2:MOD_G2 + 1] * f
    mu = jnp.mean(z, axis=-1, keepdims=True)
    zc = z - mu
    y = zc * lax.rsqrt(jnp.mean(zc * zc, axis=-1, keepdims=True) + NORM_EPS)
    o_ref[0] = y * ln_ref[0:1] + ln_ref[1:2]


def _final_ln(x1, yg, route, mods, lnp, nbl):
    B, L, D = x1.shape
    nblk = L // TM
    blk = pl.BlockSpec((1, TM, D), lambda b, i: (b, i, 0))
    return pl.pallas_call(
        _final_ln_kernel,
        grid=(B, nblk),
        in_specs=[blk,
                  pl.BlockSpec((TOP_K, TM, D), lambda b, i: (0, b * nblk + i, 0)),
                  pl.BlockSpec((1, TM, LANES), lambda b, i: (b, i, 0)),
                  pl.BlockSpec((1, 8, D), lambda b, i: (2 * b + i // nbl, 0, 0)),
                  pl.BlockSpec((2, D), lambda b, i: (0, 0))],
        out_specs=blk,
        out_shape=jax.ShapeDtypeStruct((B, L, D), F32),
        compiler_params=_cparams(2),
        name="final_ln",
    )(x1, yg, route, mods, lnp)


def _rope_tables(S, n_ctx):
    rows = S // GRID_W
    row = jnp.repeat(jnp.arange(rows), GRID_W).astype(F32)
    col = jnp.tile(jnp.arange(GRID_W), rows).astype(F32)
    inv = ROPE_BASE ** (-jnp.arange(ROPE_PAIRS_AXIS, dtype=F32) / ROPE_PAIRS_AXIS)
    ang = jnp.concatenate([row[:, None] * inv, col[:, None] * inv], -1)
    cos, sin = jnp.cos(ang), jnp.sin(ang)
    cos = jnp.concatenate([cos, jnp.ones((n_ctx, cos.shape[1]), F32)], axis=0)
    sin = jnp.concatenate([sin, jnp.zeros((n_ctx, sin.shape[1]), F32)], axis=0)
    return (jnp.concatenate([cos, cos, cos, cos], axis=1),
            jnp.concatenate([-sin, -sin, sin, sin], axis=1))


def _qk_column_perm():
    lane = np.arange(DA_HEAD_W)
    half, mp, jj = lane // 64, (lane % 64) // 32, lane % 32
    src = mp * DA_HEAD_DIM + half * 32 + jj
    head = np.arange(DA_HEADS)[:, None] * DA_HEAD_W
    perm = (head + src[None, :]).reshape(-1)
    return np.concatenate([perm, D_MODEL + perm])


def _layer_mods(mod_rows, B):
    D = D_MODEL
    parts = mod_rows.reshape(8, 6, D)
    sh1, sc1, g1, sh2, sc2, g2 = (parts[:, n] for n in range(6))
    tab = jnp.stack([1.0 + sc1, sh1, g1, 1.0 + sc2, sh2, g2, jnp.zeros_like(g1),
                     jnp.zeros_like(g1)], axis=1)
    lat = tab[:B]
    ctx = jnp.broadcast_to(tab[B:B + 1], (B, 8, D))
    return jnp.stack([lat, ctx], axis=1).reshape(2 * B, 8, D)


def kernel(x, c, ctx, c_ctx, ada_w, ada_b, ln_g, ln_b, da_w_in, da_w_out, da_lambda, da_subln_w,
           gla_w_in, gla_w_gate, gla_b_gate, gla_norm_w, gla_w_out, router_w, router_b,
           moe_w_gu, moe_b_gu, moe_w_down, moe_b_down):
    B, S, D = x.shape
    n_ctx = ctx.shape[1]
    assert D == D_MODEL and n_ctx == TM and S % TM == 0 and S % GRID_W == 0 and B + 1 <= 8
    nbl = S // TM
    nb = nbl + 1

    cc = jnp.concatenate([c, c_ctx[None, :], jnp.zeros((8 - B - 1, D), F32)], axis=0)
    mod_all = _ada_mods(cc, ada_w, ada_b)
    xa = jnp.concatenate([x, ctx], axis=1)

    rw = [jnp.pad(router_w[i], ((0, 0), (0, LANES - N_EXPERTS))) for i in range(DEPTH)]
    rb = [jnp.pad(router_b[i], (0, LANES - N_EXPERTS)).reshape(1, LANES) for i in range(DEPTH)]
    lnp = [[jnp.stack([ln_g[i, n], ln_b[i, n]]) for n in range(2)] for i in range(DEPTH)]

    mods = _layer_mods(mod_all[0], B)
    w_in = da_w_in[0]
    wqk = w_in[:, _qk_column_perm()].astype(BF16)
    wv = w_in[:, 2 * D:].astype(BF16)
    cos, sin = _rope_tables(S, n_ctx)
    q, k, v = _proj_da(xa, mods, cos, sin, wqk, wv)
    lam_init = _lambda_init(0)
    lv = da_lambda[0].astype(F32)
    lam = (jnp.exp(jnp.sum(lv[0] * lv[1])) - jnp.exp(jnp.sum(lv[2] * lv[3])) + lam_init).reshape(1)
    o = _diff_attention(lam, q, k, v, da_subln_w[0], lam_init)
    x1, u, route, counts = _mixer_out("da", [o], da_w_out[0].astype(BF16), xa, mods, lnp[0][0],
                                      rw[0], rb[0], nb)
    yg = _moe_ffn(u.reshape(B * nb * TM, D), route.reshape(B * nb * TM, LANES), counts,
                  moe_w_gu[0], moe_b_gu[0], moe_w_down[0], moe_b_down[0])
    xa = _final_ln(x1, yg, route, mods, lnp[0][1], nbl)

    mods = _layer_mods(mod_all[1], B)
    gw = gla_w_in[0]
    c3 = 2 * GLA_DK + 2 * GLA_DV
    w_main = gw[:, :c3].astype(BF16)
    wz = jnp.pad(gw[:, c3:], ((0, 0), (0, LANES - 2 * GLA_GATE_RANK))).astype(BF16)
    wg = jnp.zeros((LANES, 2 * GLA_DK), F32)
    wg = wg.at[:GLA_GATE_RANK, :GLA_DK].set(gla_w_gate[0, 0])
    wg = wg.at[GLA_GATE_RANK:2 * GLA_GATE_RANK, GLA_DK:].set(gla_w_gate[0, 1])
    bg = gla_b_gate[0].reshape(1, 2 * GLA_DK)
    gq, gk, gv, gr, gg = _proj_gla(xa, mods, w_main, wz, wg, bg)
    of = _gla_scan(gq, gk, gv, gg, reverse=False)
    ob = _gla_scan(gq, gk, gv, gg, reverse=True)
    x1, u, route, counts = _mixer_out("gla", [of, ob, gr], gla_w_out[0].astype(BF16), xa, mods,
                                      lnp[1][0], rw[1], rb[1], nbl, norm_w=gla_norm_w[0])
    yg = _moe_ffn(u.reshape(B * S, D), route.reshape(B * S, LANES), counts,
                  moe_w_gu[1], moe_b_gu[1], moe_w_down[1], moe_b_down[1])
    return _final_ln(x1, yg, route, mods, lnp[1][1], nbl)
```

```python
import functools
import math

import numpy as np
import jax
import jax.numpy as jnp
from jax import lax
from jax.experimental import pallas as pl
from jax.experimental.pallas import tpu as pltpu

F32 = jnp.float32
BF16 = jnp.bfloat16

D_MODEL = 1024
DEPTH = 2
GRID_W = 64

DA_HEADS = 8
DA_HEAD_DIM = 64
DA_HEAD_W = 2 * DA_HEAD_DIM
ROPE_BASE = 10000.0
ROPE_PAIRS_AXIS = DA_HEAD_DIM // 4

GLA_HEADS = 4
GLA_DK = D_MODEL // 2
GLA_DV = D_MODEL
GLA_DK_HEAD = GLA_DK // GLA_HEADS
GLA_DV_HEAD = GLA_DV // GLA_HEADS
GLA_GATE_RANK = 16
GLA_TAU = 16.0
GLA_CHUNK = 64

N_EXPERTS = 32
TOP_K = 4
SWIGLU_ALPHA = 1.702
SWIGLU_LIMIT = 7.0
MOE_BLOCK = 256

DEEPNORM_ALPHA = (2.0 * DEPTH) ** 0.25
NORM_EPS = 1e-5

LANES = 128
TM = 256
ATT_TK_MAX = 2816
VMEM_LIMIT = 48 * 1024 * 1024
MOE_VMEM_LIMIT = 56 * 1024 * 1024

MOD_SC1, MOD_SH1, MOD_G1, MOD_SC2, MOD_SH2, MOD_G2 = range(6)


def _cparams(n_axes):
    return pltpu.CompilerParams(dimension_semantics=("arbitrary",) * n_axes,
                                vmem_limit_bytes=VMEM_LIMIT)


def _lambda_init(layer_idx):
    return 0.8 - 0.6 * math.exp(-0.3 * layer_idx)


def _ada_kernel(c_ref, w_ref, b_ref, o_ref):
    c = c_ref[...]
    s = c * jax.nn.sigmoid(c)
    o_ref[0] = jnp.dot(s, w_ref[0], preferred_element_type=F32) + b_ref[0]


def _ada_mods(cc, ada_w, ada_b):
    nt = 1536
    n6 = 6 * D_MODEL
    return pl.pallas_call(
        _ada_kernel,
        grid=(DEPTH, n6 // nt),
        in_specs=[pl.BlockSpec((8, D_MODEL), lambda l, j: (0, 0)),
                  pl.BlockSpec((1, D_MODEL, nt), lambda l, j: (l, 0, j)),
                  pl.BlockSpec((1, 1, nt), lambda l, j: (l, 0, j))],
        out_specs=pl.BlockSpec((1, 8, nt), lambda l, j: (l, 0, j)),
        out_shape=jax.ShapeDtypeStruct((DEPTH, 8, n6), F32),
        compiler_params=_cparams(2),
        name="ada_mods",
    )(cc, ada_w, ada_b.reshape(DEPTH, 1, n6))


def _proj_da_kernel(x_ref, mod_ref, cos_ref, sin_ref, wqk_ref, wv_ref, q_ref, k_ref, v_ref):
    m = mod_ref[0]
    t = (x_ref[0] * m[MOD_SC1:MOD_SC1 + 1] + m[MOD_SH1:MOD_SH1 + 1]).astype(BF16)
    cos = cos_ref[...]
    sin = sin_ref[...]
    q_scale = DA_HEAD_DIM ** -0.5 * math.log2(math.e)
    for j in range(DA_HEADS):
        y2 = jnp.dot(t, wqk_ref[:, j * 256:(j + 1) * 256], preferred_element_type=F32)
        for hh in range(2):
            y = y2[:, hh * LANES:(hh + 1) * LANES]
            y = y * cos + pltpu.roll(y, 64, 1) * sin
            col = (2 * j + hh) * LANES
            if col < D_MODEL:
                q_ref[0, :, col:col + LANES] = (y * q_scale).astype(BF16)
            else:
                k_ref[0, :, col - D_MODEL:col - D_MODEL + LANES] = y.astype(BF16)
    v_ref[0] = jnp.dot(t, wv_ref[...], preferred_element_type=F32).astype(BF16)


def _proj_da(xa, mods, cos, sin, wqk, wv):
    B, LT, D = xa.shape
    nb = LT // TM
    nbl = nb - 1
    out = jax.ShapeDtypeStruct((B, LT, D), BF16)
    blk = pl.BlockSpec((1, TM, D), lambda b, i: (b, i, 0))
    return pl.pallas_call(
        _proj_da_kernel,
        grid=(B, nb),
        in_specs=[blk,
                  pl.BlockSpec((1, 8, D), lambda b, i: (2 * b + i // nbl, 0, 0)),
                  pl.BlockSpec((TM, LANES), lambda b, i: (i, 0)),
                  pl.BlockSpec((TM, LANES), lambda b, i: (i, 0)),
                  pl.BlockSpec((D, 2 * D), lambda b, i: (0, 0)),
                  pl.BlockSpec((D, D), lambda b, i: (0, 0))],
        out_specs=[blk, blk, blk],
        out_shape=[out, out, out],
        compiler_params=_cparams(2),
        name="da_proj",
    )(xa, mods, cos, sin, wqk, wv)


def _attn_kernel(lam_ref, q_ref, k_ref, v_ref, sw_ref, o_ref, vext_sc, m_sc, acc_sc, s_sc, *,
                 n_lat, n_ctx, lam_init):
    i = pl.program_id(2)
    tq = q_ref.shape[1]

    @pl.when(i == 0)
    def _():
        vext_sc[:, :DA_HEAD_W] = v_ref[0]
        vext_sc[:, DA_HEAD_W:] = jnp.ones((vext_sc.shape[0], DA_HEAD_W), BF16)

    q = q_ref[0]
    lane = lax.broadcasted_iota(jnp.int32, (1, DA_HEAD_W), 1)
    map0 = (lane % 64) < 32
    zero = jnp.zeros_like(q)
    qs = jnp.concatenate([jnp.where(map0, q, zero), jnp.where(map0, zero, q)], axis=0)
    m_sc[...] = jnp.full(m_sc.shape, -jnp.inf, F32)
    acc_sc[...] = jnp.zeros(acc_sc.shape, F32)

    def scores(off, tk):
        k = k_ref[0, pl.ds(off, tk), :]
        return lax.dot_general(qs, k, (((1,), (1,)), ((), ())), preferred_element_type=F32)

    def accumulate(s, off, tk):
        m_prev = m_sc[...]
        m_new = jnp.maximum(m_prev, jnp.max(s, axis=1, keepdims=True))
        alpha = jnp.exp2(m_prev - m_new)
        p = jnp.exp2(s - jnp.tile(m_new, (1, tk // LANES)))
        pv = jnp.dot(p.astype(BF16), vext_sc[pl.ds(off, tk), :], preferred_element_type=F32)
        acc_sc[...] = jnp.tile(alpha, (1, 2)) * acc_sc[...] + pv
        m_sc[...] = m_new

    n_q_lat = n_lat // tq
    tk = s_sc.shape[2]
    n_steps = (n_lat + n_ctx) // tk

    @pl.when(i < n_q_lat)
    def _():
        s_sc[0] = scores(0, tk)
        for t in range(n_steps):
            if t + 1 < n_steps:
                s_sc[(t + 1) % 2] = scores((t + 1) * tk, tk)
            accumulate(s_sc[t % 2], t * tk, tk)

    @pl.when(i >= n_q_lat)
    def _():
        accumulate(scores(n_lat, n_ctx), n_lat, n_ctx)

    acc = acc_sc[...]
    o0 = acc[:tq, :DA_HEAD_W] / acc[:tq, DA_HEAD_W:DA_HEAD_W + 1]
    o1 = acc[tq:, :DA_HEAD_W] / acc[tq:, DA_HEAD_W:DA_HEAD_W + 1]
    o = o0 - lam_ref[0] * o1
    o = o * lax.rsqrt(jnp.mean(o * o, axis=-1, keepdims=True) + NORM_EPS)
    o_ref[0] = (o * sw_ref[...] * (1.0 - lam_init)).astype(BF16)


def _diff_attention(lam, q, k, v, subln_w, lam_init):
    B, LT, D = q.shape
    nb = LT // TM
    tk = max(t for t in range(TM, ATT_TK_MAX + 1, TM) if LT % t == 0)
    kern = functools.partial(_attn_kernel, n_lat=LT - TM, n_ctx=TM, lam_init=lam_init)
    grid_spec = pltpu.PrefetchScalarGridSpec(
        num_scalar_prefetch=1,
        grid=(B, DA_HEADS, nb),
        in_specs=[pl.BlockSpec((1, TM, DA_HEAD_W), lambda b, h, i, lam: (b, i, h)),
                  pl.BlockSpec((1, LT, DA_HEAD_W), lambda b, h, i, lam: (b, 0, h)),
                  pl.BlockSpec((1, LT, DA_HEAD_W), lambda b, h, i, lam: (b, 0, h)),
                  pl.BlockSpec((1, DA_HEAD_W), lambda b, h, i, lam: (0, 0))],
        out_specs=pl.BlockSpec((1, TM, DA_HEAD_W), lambda b, h, i, lam: (b, i, h)),
        scratch_shapes=[pltpu.VMEM((LT, 2 * DA_HEAD_W), BF16),
                        pltpu.VMEM((2 * TM, LANES), F32),
                        pltpu.VMEM((2 * TM, 2 * DA_HEAD_W), F32),
                        pltpu.VMEM((2, 2 * TM, tk), F32)],
    )
    return pl.pallas_call(
        kern,
        grid_spec=grid_spec,
        out_shape=jax.ShapeDtypeStruct((B, LT, D), BF16),
        compiler_params=_cparams(3),
        name="diff_attn",
    )(lam, q, k, v, subln_w.reshape(1, DA_HEAD_W))


def _route_block(logits, cnt_sc):
    lane = lax.broadcasted_iota(jnp.int32, logits.shape, 1)
    lane_f = lane.astype(F32)
    work = jnp.where(lane < N_EXPERTS, logits, -jnp.inf)
    tops, idxs, hits = [], [], []
    for _ in range(TOP_K):
        mk = jnp.max(work, axis=1, keepdims=True)
        ik = jnp.min(jnp.where(work == mk, lane_f, float(LANES)), axis=1, keepdims=True)
        hit = lane_f == ik
        tops.append(mk)
        idxs.append(ik)
        hits.append(hit)
        work = jnp.where(hit, -jnp.inf, work)
    chosen = functools.reduce(jnp.logical_or, hits).astype(F32)
    n = logits.shape[0]
    row = lax.broadcasted_iota(jnp.int32, (n, n), 0)
    col = lax.broadcasted_iota(jnp.int32, (n, n), 1)
    before = jnp.dot((col < row).astype(BF16), chosen.astype(BF16), preferred_element_type=F32)
    rank_all = cnt_sc[...] + before
    cnt_sc[...] = cnt_sc[...] + jnp.sum(chosen, axis=0, keepdims=True)
    exps = [jnp.exp(t - tops[0]) for t in tops]
    denom = functools.reduce(jnp.add, exps)
    table = jnp.zeros(logits.shape, F32)
    for k in range(TOP_K):
        rk = jnp.sum(jnp.where(hits[k], rank_all, 0.0), axis=1, keepdims=True)
        table = jnp.where(lane == k, idxs[k], table)
        table = jnp.where(lane == TOP_K + k, rk, table)
        table = jnp.where(lane == 2 * TOP_K + k, exps[k] / denom, table)
    return table


def _post_mixer(pre, w_ref, x_ref, mod_ref, ln_ref, rw_ref, rb_ref, x1_ref, u_ref, rt_ref, cnt_ref,
                cnt_sc):
    @pl.when((pl.program_id(0) == 0) & (pl.program_id(1) == 0))
    def _():
        cnt_sc[...] = jnp.zeros(cnt_sc.shape, F32)

    m = mod_ref[0]
    y = jnp.dot(pre, w_ref[...], preferred_element_type=F32)
    z = DEEPNORM_ALPHA * x_ref[0] + m[MOD_G1:MOD_G1 + 1] * y
    mu = jnp.mean(z, axis=-1, keepdims=True)
    zc = z - mu
    x1 = zc * lax.rsqrt(jnp.mean(zc * zc, axis=-1, keepdims=True) + NORM_EPS)
    x1 = x1 * ln_ref[0:1] + ln_ref[1:2]
    x1_ref[0] = x1
    u = x1 * m[MOD_SC2:MOD_SC2 + 1] + m[MOD_SH2:MOD_SH2 + 1]
    u_hi = u.astype(BF16)
    u_ref[0] = u_hi
    u_lo = (u - u_hi.astype(F32)).astype(BF16)
    d_hi = jnp.dot(u_hi, rw_ref[...], preferred_element_type=F32)
    d_lo = jnp.dot(u_lo, rw_ref[:, :LANES], preferred_element_type=F32)
    logits = d_hi[:, :LANES] + d_hi[:, LANES:] + d_lo + rb_ref[...]
    rt_ref[0] = _route_block(logits, cnt_sc)
    cnt_ref[...] = jnp.broadcast_to(cnt_sc[...], cnt_ref.shape)


def _out_da_kernel(o_ref, w_ref, x_ref, mod_ref, ln_ref, rw_ref, rb_ref, x1_ref, u_ref, rt_ref,
                   cnt_ref, cnt_sc):
    _post_mixer(o_ref[0], w_ref, x_ref, mod_ref, ln_ref, rw_ref, rb_ref, x1_ref, u_ref, rt_ref,
                cnt_ref, cnt_sc)


def _out_gla_kernel(of_ref, ob_ref, r_ref, nw_ref, w_ref, x_ref, mod_ref, ln_ref, rw_ref, rb_ref,
                    x1_ref, u_ref, rt_ref, cnt_ref, cnt_sc):
    parts = []
    for h in range(GLA_HEADS):
        sl = slice(h * GLA_DV_HEAD, (h + 1) * GLA_DV_HEAD)
        o = of_ref[0, :, sl] + ob_ref[0, :, sl]
        o = o * lax.rsqrt(jnp.mean(o * o, axis=-1, keepdims=True) + NORM_EPS) * nw_ref[...]
        r = r_ref[0, :, sl]
        parts.append((o * (r * jax.nn.sigmoid(r))).astype(BF16))
    pre = jnp.concatenate(parts, axis=1)
    _post_mixer(pre, w_ref, x_ref, mod_ref, ln_ref, rw_ref, rb_ref, x1_ref, u_ref, rt_ref, cnt_ref,
                cnt_sc)


def _mixer_out(kind, acts, w_out, xa, mods, lnp, rw, rb, nb_out, norm_w=None):
    B, LT, D = xa.shape
    nbl = LT // TM - 1
    blk = pl.BlockSpec((1, TM, D), lambda b, i: (b, i, 0))
    common_specs = [pl.BlockSpec((D, D), lambda b, i: (0, 0)),
                    blk,
                    pl.BlockSpec((1, 8, D), lambda b, i: (2 * b + i // nbl, 0, 0)),
                    pl.BlockSpec((2, D), lambda b, i: (0, 0)),
                    pl.BlockSpec((D, 2 * LANES), lambda b, i: (0, 0)),
                    pl.BlockSpec((1, LANES), lambda b, i: (0, 0))]
    lout = nb_out * TM
    out_shape = [jax.ShapeDtypeStruct((B, lout, D), F32),
                 jax.ShapeDtypeStruct((B, lout, D), BF16),
                 jax.ShapeDtypeStruct((B, lout, LANES), F32),
                 jax.ShapeDtypeStruct((8, LANES), F32)]
    out_specs = [blk, blk, pl.BlockSpec((1, TM, LANES), lambda b, i: (b, i, 0)),
                 pl.BlockSpec((8, LANES), lambda b, i: (0, 0))]
    if kind == "da":
        kern = _out_da_kernel
        in_specs = [blk] + common_specs
        args = list(acts)
    else:
        kern = _out_gla_kernel
        in_specs = [blk, blk, blk, pl.BlockSpec((1, GLA_DV_HEAD), lambda b, i: (0, 0))] + common_specs
        args = list(acts) + [norm_w.reshape(1, GLA_DV_HEAD)]
    return pl.pallas_call(
        kern,
        grid=(B, nb_out),
        in_specs=in_specs,
        out_specs=out_specs,
        out_shape=out_shape,
        scratch_shapes=[pltpu.VMEM((1, LANES), F32)],
        compiler_params=_cparams(2),
        name="mixer_out_" + kind,
    )(*args, w_out, xa, mods, lnp, rw, rb)


def _proj_gla_kernel(x_ref, mod_ref, w_ref, wz_ref, wg_ref, bg_ref,
                     q_ref, k_ref, v_ref, r_ref, g_ref):
    m = mod_ref[0]
    t = (x_ref[0] * m[MOD_SC1:MOD_SC1 + 1] + m[MOD_SH1:MOD_SH1 + 1]).astype(BF16)
    c0, c1, c2, c3 = GLA_DK, 2 * GLA_DK, 2 * GLA_DK + GLA_DV, 2 * GLA_DK + 2 * GLA_DV
    q_ref[0] = jnp.dot(t, w_ref[:, :c0], preferred_element_type=F32) * (GLA_DK_HEAD ** -0.5)
    k_ref[0] = jnp.dot(t, w_ref[:, c0:c1], preferred_element_type=F32)
    v_ref[0] = jnp.dot(t, w_ref[:, c1:c2], preferred_element_type=F32).astype(BF16)
    r_ref[0] = jnp.dot(t, w_ref[:, c2:c3], preferred_element_type=F32)
    z = jnp.dot(t, wz_ref[...], preferred_element_type=F32)
    gl = jnp.dot(z, wg_ref[...], preferred_element_type=F32) + bg_ref[...]
    log_sig = jnp.minimum(gl, 0.0) - jnp.log1p(jnp.exp(-jnp.abs(gl)))
    g_ref[0] = log_sig * (1.0 / GLA_TAU)


def _proj_gla(xa, mods, w_main, wz, wg, bg):
    B, LT, D = xa.shape
    nb = LT // TM
    nbl = nb - 1
    blk = lambda w: pl.BlockSpec((1, TM, w), lambda b, i: (b, i, 0))
    return pl.pallas_call(
        _proj_gla_kernel,
        grid=(B, nb),
        in_specs=[blk(D),
                  pl.BlockSpec((1, 8, D), lambda b, i: (2 * b + i // nbl, 0, 0)),
                  pl.BlockSpec(w_main.shape, lambda b, i: (0, 0)),
                  pl.BlockSpec(wz.shape, lambda b, i: (0, 0)),
                  pl.BlockSpec(wg.shape, lambda b, i: (0, 0)),
                  pl.BlockSpec(bg.shape, lambda b, i: (0, 0))],
        out_specs=[blk(GLA_DK), blk(GLA_DK), blk(GLA_DV), blk(GLA_DV), blk(2 * GLA_DK)],
        out_shape=[jax.ShapeDtypeStruct((B, LT, GLA_DK), F32),
                   jax.ShapeDtypeStruct((B, LT, GLA_DK), F32),
                   jax.ShapeDtypeStruct((B, LT, GLA_DV), BF16),
                   jax.ShapeDtypeStruct((B, LT, GLA_DV), F32),
                   jax.ShapeDtypeStruct((B, LT, 2 * GLA_DK), F32)],
        compiler_params=_cparams(2),
        name="gla_proj",
    )(xa, mods, w_main, wz, wg, bg)


def _gla_scan_kernel(q_ref, k_ref, v_ref, g_ref, o_ref, st_sc, *, reverse):
    j = pl.program_id(1)

    @pl.when(j == 0)
    def _():
        st_sc[...] = jnp.zeros(st_sc.shape, F32)

    C = GLA_CHUNK
    n_chunks = TM // C

    def causal(n):
        row = lax.broadcasted_iota(jnp.int32, (n, n), 0)
        col = lax.broadcasted_iota(jnp.int32, (n, n), 1)
        return (row // C == col // C) & ((col >= row) if reverse else (col <= row))

    keep = causal(C)
    tri = causal(TM).astype(BF16)
    g = g_ref[0]
    g_hi = g.astype(BF16)
    rem = g - g_hi.astype(F32)
    g_mid = rem.astype(BF16)
    g_lo = (rem - g_mid.astype(F32)).astype(BF16)
    b_all = (jnp.dot(tri, g_hi, preferred_element_type=F32)
             + jnp.dot(tri, g_mid, preferred_element_type=F32)
             + jnp.dot(tri, g_lo, preferred_element_type=F32))

    states = [st_sc[h] for h in range(GLA_HEADS)]
    order = range(n_chunks - 1, -1, -1) if reverse else range(n_chunks)
    for c in order:
        rows = slice(c * C, (c + 1) * C)
        for h in range(GLA_HEADS):
            ks = slice(h * GLA_DK_HEAD, (h + 1) * GLA_DK_HEAD)
            vs = slice(h * GLA_DV_HEAD, (h + 1) * GLA_DV_HEAD)
            b = b_all[rows, ks]
            tot = b[0:1] if reverse else b[C - 1:C]
            q = q_ref[0, rows, ks]
            k = k_ref[0, rows, ks]
            q_in = (q * jnp.exp(b)).astype(BF16)
            k_in = (k * jnp.exp(-b)).astype(BF16)
            k_st = (k * jnp.exp(tot - b)).astype(BF16)
            att = lax.dot_general(q_in, k_in, (((1,), (1,)), ((), ())), preferred_element_type=F32)
            att = jnp.where(keep, att, 0.0).astype(BF16)
            v = v_ref[0, rows, vs]
            st = states[h]
            o = jnp.dot(att, v, preferred_element_type=F32)
            o = o + lax.dot_general(q_in, st.astype(BF16), (((1,), (1,)), ((), ())),
                                    preferred_element_type=F32)
            o_ref[0, rows, vs] = o
            ds = lax.dot_general(v, k_st, (((0,), (0,)), ((), ())), preferred_element_type=F32)
            states[h] = st * jnp.exp(tot) + ds
    for h in range(GLA_HEADS):
        st_sc[h] = states[h]


def _gla_scan(q, k, v, g, reverse):
    B, LT, _ = q.shape
    nb = LT // TM
    ctx_blk = nb - 1
    if reverse:
        order = lambda j: jnp.where(j == 0, ctx_blk, ctx_blk - j)
    else:
        order = lambda j: jnp.where(j == 0, ctx_blk, j - 1)
    gcol = 1 if reverse else 0
    return pl.pallas_call(
        functools.partial(_gla_scan_kernel, reverse=reverse),
        grid=(B, nb),
        in_specs=[pl.BlockSpec((1, TM, GLA_DK), lambda b, j: (b, order(j), 0)),
                  pl.BlockSpec((1, TM, GLA_DK), lambda b, j: (b, order(j), 0)),
                  pl.BlockSpec((1, TM, GLA_DV), lambda b, j: (b, order(j), 0)),
                  pl.BlockSpec((1, TM, GLA_DK), lambda b, j: (b, order(j), gcol))],
        out_specs=pl.BlockSpec((1, TM, GLA_DV), lambda b, j: (b, order(j), 0)),
        out_shape=jax.ShapeDtypeStruct((B, LT, GLA_DV), F32),
        scratch_shapes=[pltpu.VMEM((GLA_HEADS, GLA_DV_HEAD, GLA_DK_HEAD), F32)],
        compiler_params=_cparams(2),
        name="gla_scan_bwd" if reverse else "gla_scan_fwd",
    )(q, k, v, g)


def _moe_kernel(be_ref, nu_ref, x_ref, wgu_ref, bgu_ref, wd_ref, bd_ref, y_ref, wgu_sc, wd_sc):
    i = pl.program_id(0)

    @pl.when(i < nu_ref[0])
    def _():
        @pl.when((i == 0) | (be_ref[i] != be_ref[jnp.maximum(i - 1, 0)]))
        def _():
            wgu_sc[...] = wgu_ref[0].astype(BF16)
            wd_sc[...] = wd_ref[0].astype(BF16)

        gu = jnp.dot(x_ref[...], wgu_sc[...], preferred_element_type=F32) + bgu_ref[0]
        half = gu.shape[1] // 2
        glu = jnp.minimum(gu[:, :half], SWIGLU_LIMIT)
        lin = jnp.clip(gu[:, half:], -SWIGLU_LIMIT, SWIGLU_LIMIT)
        act = glu * jax.nn.sigmoid(SWIGLU_ALPHA * glu) * (lin + 1.0)
        y = jnp.dot(act.astype(BF16), wd_sc[...], preferred_element_type=F32) + bd_ref[0]
        y_ref[...] = y.astype(y_ref.dtype)

    @pl.when(i >= nu_ref[0])
    def _():
        y_ref[...] = jnp.zeros(y_ref.shape, y_ref.dtype)


def _moe_experts(block_expert, n_used, xb, w_gu, b_gu, w_down, b_down):
    n_rows, D = xb.shape
    n_blocks = n_rows // MOE_BLOCK
    grid_spec = pltpu.PrefetchScalarGridSpec(
        num_scalar_prefetch=2,
        grid=(n_blocks,),
        in_specs=[pl.BlockSpec((MOE_BLOCK, D), lambda i, be, nu: (i, 0)),
                  pl.BlockSpec((1, D, 2 * D), lambda i, be, nu: (be[i], 0, 0)),
                  pl.BlockSpec((1, 1, 2 * D), lambda i, be, nu: (be[i], 0, 0)),
                  pl.BlockSpec((1, D, D), lambda i, be, nu: (be[i], 0, 0)),
                  pl.BlockSpec((1, 1, D), lambda i, be, nu: (be[i], 0, 0))],
        out_specs=pl.BlockSpec((MOE_BLOCK, D), lambda i, be, nu: (i, 0)),
        scratch_shapes=[pltpu.VMEM((D, 2 * D), BF16), pltpu.VMEM((D, D), BF16)],
    )
    return pl.pallas_call(
        _moe_kernel,
        grid_spec=grid_spec,
        out_shape=jax.ShapeDtypeStruct((n_rows, D), BF16),
        compiler_params=pltpu.CompilerParams(dimension_semantics=("arbitrary",),
                                             vmem_limit_bytes=MOE_VMEM_LIMIT),
        name="moe_experts",
    )(block_expert, n_used, xb, w_gu, b_gu.reshape(N_EXPERTS, 1, 2 * D), w_down,
      b_down.reshape(N_EXPERTS, 1, D))


def _moe_ffn(u, route, counts, w_gu, b_gu, w_down, b_down):
    T, D = u.shape
    expert = route[:, :TOP_K].astype(jnp.int32)
    rank = route[:, TOP_K:2 * TOP_K].astype(jnp.int32)
    cnt = counts[0, :N_EXPERTS].astype(jnp.int32)
    padded = (cnt + MOE_BLOCK - 1) // MOE_BLOCK * MOE_BLOCK
    padded_end = jnp.cumsum(padded)
    padded_start = padded_end - padded
    pos = (padded_start[expert] + rank).T
    n_blocks = -(-T * TOP_K // MOE_BLOCK) + N_EXPERTS
    n_rows = n_blocks * MOE_BLOCK
    block_start = jnp.arange(n_blocks, dtype=jnp.int32) * MOE_BLOCK
    block_expert = jnp.minimum(jnp.sum(padded_end[None, :] <= block_start[:, None], axis=1),
                               N_EXPERTS - 1).astype(jnp.int32)
    n_used = (padded_end[-1:] // MOE_BLOCK).astype(jnp.int32)
    tok = jnp.tile(jnp.arange(T, dtype=jnp.int32), TOP_K)
    _, sorted_tok = lax.sort((pos.reshape(-1), tok), num_keys=1)
    row = jnp.arange(n_rows, dtype=jnp.int32)
    row_e = jnp.repeat(block_expert, MOE_BLOCK)
    row_rank = row - padded_start[row_e]
    compact = jnp.cumsum(cnt)[row_e] - cnt[row_e] + row_rank
    row_tok = jnp.where(row_rank < cnt[row_e], sorted_tok[jnp.minimum(compact, T * TOP_K - 1)], 0)
    xb = u[row_tok]
    yb = _moe_experts(block_expert, n_used, xb, w_gu, b_gu, w_down, b_down)
    return yb[pos.reshape(-1)].reshape(TOP_K, T, D)


def _final_ln_kernel(x_ref, y_ref, rt_ref, mod_ref, ln_ref, o_ref):
    m = mod_ref[0]
    rt = rt_ref[0]
    f = rt[:, 2 * TOP_K:2 * TOP_K + 1] * y_ref[0].astype(F32)
    for k in range(1, TOP_K):
        f = f + rt[:, 2 * TOP_K + k:2 * TOP_K + k + 1] * y_ref[k].astype(F32)
    z = DEEPNORM_ALPHA * x_ref[0] + m[MOD_G2:MOD_G2 + 1] * f
    mu = jnp.mean(z, axis=-1, keepdims=True)
    zc = z - mu
    y = zc * lax.rsqrt(jnp.mean(zc * zc, axis=-1, keepdims=True) + NORM_EPS)
    o_ref[0] = y * ln_ref[0:1] + ln_ref[1:2]


def _final_ln(x1, yg, route, mods, lnp, nbl):
    B, L, D = x1.shape
    nblk = L // TM
    blk = pl.BlockSpec((1, TM, D), lambda b, i: (b, i, 0))
    return pl.pallas_call(
        _final_ln_kernel,
        grid=(B, nblk),
        in_specs=[blk,
                  pl.BlockSpec((TOP_K, TM, D), lambda b, i: (0, b * nblk + i, 0)),
                  pl.BlockSpec((1, TM, LANES), lambda b, i: (b, i, 0)),
                  pl.BlockSpec((1, 8, D), lambda b, i: (2 * b + i // nbl, 0, 0)),
                  pl.BlockSpec((2, D), lambda b, i: (0, 0))],
        out_specs=blk,
        out_shape=jax.ShapeDtypeStruct((B, L, D), F32),
        compiler_params=_cparams(2),
        name="final_ln",
    )(x1, yg, route, mods, lnp)


def _rope_tables(S, n_ctx):
    rows = S // GRID_W
    row = jnp.repeat(jnp.arange(rows), GRID_W).astype(F32)
    col = jnp.tile(jnp.arange(GRID_W), rows).astype(F32)
    inv = ROPE_BASE ** (-jnp.arange(ROPE_PAIRS_AXIS, dtype=F32) / ROPE_PAIRS_AXIS)
    ang = jnp.concatenate([row[:, None] * inv, col[:, None] * inv], -1)
    cos, sin = jnp.cos(ang), jnp.sin(ang)
    cos = jnp.concatenate([cos, jnp.ones((n_ctx, cos.shape[1]), F32)], axis=0)
    sin = jnp.concatenate([sin, jnp.zeros((n_ctx, sin.shape[1]), F32)], axis=0)
    return (jnp.concatenate([cos, cos, cos, cos], axis=1),
            jnp.concatenate([-sin, -sin, sin, sin], axis=1))


def _qk_column_perm():
    lane = np.arange(DA_HEAD_W)
    half, mp, jj = lane // 64, (lane % 64) // 32, lane % 32
    src = mp * DA_HEAD_DIM + half * 32 + jj
    head = np.arange(DA_HEADS)[:, None] * DA_HEAD_W
    perm = (head + src[None, :]).reshape(-1)
    return np.concatenate([perm, D_MODEL + perm])


def _split_router_w(w):
    w = jnp.pad(w, ((0, 0), (0, LANES - N_EXPERTS)))
    hi = w.astype(BF16)
    lo = (w - hi.astype(F32)).astype(BF16)
    return jnp.concatenate([hi, lo], axis=1)


def _layer_mods(mod_rows, B):
    D = D_MODEL
    parts = mod_rows.reshape(8, 6, D)
    sh1, sc1, g1, sh2, sc2, g2 = (parts[:, n] for n in range(6))
    tab = jnp.stack([1.0 + sc1, sh1, g1, 1.0 + sc2, sh2, g2, jnp.zeros_like(g1),
                     jnp.zeros_like(g1)], axis=1)
    lat = tab[:B]
    ctx = jnp.broadcast_to(tab[B:B + 1], (B, 8, D))
    return jnp.stack([lat, ctx], axis=1).reshape(2 * B, 8, D)


def kernel(x, c, ctx, c_ctx, ada_w, ada_b, ln_g, ln_b, da_w_in, da_w_out, da_lambda, da_subln_w,
           gla_w_in, gla_w_gate, gla_b_gate, gla_norm_w, gla_w_out, router_w, router_b,
           moe_w_gu, moe_b_gu, moe_w_down, moe_b_down):
    B, S, D = x.shape
    n_ctx = ctx.shape[1]
    assert D == D_MODEL and n_ctx == TM and S % TM == 0 and S % GRID_W == 0 and B + 1 <= 8
    nbl = S // TM
    nb = nbl + 1

    cc = jnp.concatenate([c, c_ctx[None, :], jnp.zeros((8 - B - 1, D), F32)], axis=0)
    mod_all = _ada_mods(cc, ada_w, ada_b)
    xa = jnp.concatenate([x, ctx], axis=1)

    rw = [_split_router_w(router_w[i]) for i in range(DEPTH)]
    rb = [jnp.pad(router_b[i], (0, LANES - N_EXPERTS)).reshape(1, LANES) for i in range(DEPTH)]
    lnp = [[jnp.stack([ln_g[i, n], ln_b[i, n]]) for n in range(2)] for i in range(DEPTH)]

    mods = _layer_mods(mod_all[0], B)
    w_in = da_w_in[0]
    wqk = w_in[:, _qk_column_perm()].astype(BF16)
    wv = w_in[:, 2 * D:].astype(BF16)
    cos, sin = _rope_tables(S, n_ctx)
    q, k, v = _proj_da(xa, mods, cos, sin, wqk, wv)
    lam_init = _lambda_init(0)
    lv = da_lambda[0].astype(F32)
    lam = (jnp.exp(jnp.sum(lv[0] * lv[1])) - jnp.exp(jnp.sum(lv[2] * lv[3])) + lam_init).reshape(1)
    o = _diff_attention(lam, q, k, v, da_subln_w[0], lam_init)
    x1, u, route, counts = _mixer_out("da", [o], da_w_out[0].astype(BF16), xa, mods, lnp[0][0],
                                      rw[0], rb[0], nb)
    yg = _moe_ffn(u.reshape(B * nb * TM, D), route.reshape(B * nb * TM, LANES), counts,
                  moe_w_gu[0], moe_b_gu[0], moe_w_down[0], moe_b_down[0])
    xa = _final_ln(x1, yg, route, mods, lnp[0][1], nbl)

    mods = _layer_mods(mod_all[1], B)
    gw = gla_w_in[0]
    c3 = 2 * GLA_DK + 2 * GLA_DV
    w_main = gw[:, :c3].astype(BF16)
    wz = jnp.pad(gw[:, c3:], ((0, 0), (0, LANES - 2 * GLA_GATE_RANK))).astype(BF16)
    wg = jnp.zeros((LANES, 2 * GLA_DK), F32)
    wg = wg.at[:GLA_GATE_RANK, :GLA_DK].set(gla_w_gate[0, 0])
    wg = wg.at[GLA_GATE_RANK:2 * GLA_GATE_RANK, GLA_DK:].set(gla_w_gate[0, 1])
    bg = gla_b_gate[0].reshape(1, 2 * GLA_DK)
    gq, gk, gv, gr, gg = _proj_gla(xa, mods, w_main, wz, wg, bg)
    of = _gla_scan(gq, gk, gv, gg, reverse=False)
    ob = _gla_scan(gq, gk, gv, gg, reverse=True)
    x1, u, route, counts = _mixer_out("gla", [of, ob, gr], gla_w_out[0].astype(BF16), xa, mods,
                                      lnp[1][0], rw[1], rb[1], nbl, norm_w=gla_norm_w[0])
    yg = _moe_ffn(u.reshape(B * S, D), route.reshape(B * S, LANES), counts,
                  moe_w_gu[1], moe_b_gu[1], moe_w_down[1], moe_b_down[1])
    return _final_ln(x1, yg, route, mods, lnp[1][1], nbl)
```

```python
import functools
import math

import numpy as np
import jax
import jax.numpy as jnp
from jax import lax
from jax.experimental import pallas as pl
from jax.experimental.pallas import tpu as pltpu

F32 = jnp.float32
BF16 = jnp.bfloat16

D_MODEL = 1024
DEPTH = 2
GRID_W = 64

DA_HEADS = 8
DA_HEAD_DIM = 64
DA_HEAD_W = 2 * DA_HEAD_DIM
ROPE_BASE = 10000.0
ROPE_PAIRS_AXIS = DA_HEAD_DIM // 4

GLA_HEADS = 4
GLA_DK = D_MODEL // 2
GLA_DV = D_MODEL
GLA_DK_HEAD = GLA_DK // GLA_HEADS
GLA_DV_HEAD = GLA_DV // GLA_HEADS
GLA_GATE_RANK = 16
GLA_TAU = 16.0
GLA_CHUNK = 64

N_EXPERTS = 32
TOP_K = 4
SWIGLU_ALPHA = 1.702
SWIGLU_LIMIT = 7.0
MOE_BLOCK = 256

DEEPNORM_ALPHA = (2.0 * DEPTH) ** 0.25
NORM_EPS = 1e-5

LANES = 128
TM = 256
ATT_TK_MAX = 2816
VMEM_LIMIT = 48 * 1024 * 1024
MOE_VMEM_LIMIT = 56 * 1024 * 1024

MOD_SC1, MOD_SH1, MOD_G1, MOD_SC2, MOD_SH2, MOD_G2 = range(6)


def _cparams(n_axes):
    return pltpu.CompilerParams(dimension_semantics=("arbitrary",) * n_axes,
                                vmem_limit_bytes=VMEM_LIMIT)


def _lambda_init(layer_idx):
    return 0.8 - 0.6 * math.exp(-0.3 * layer_idx)


def _ada_kernel(c_ref, w_ref, b_ref, o_ref):
    c = c_ref[...]
    s = c * jax.nn.sigmoid(c)
    o_ref[0] = jnp.dot(s, w_ref[0], preferred_element_type=F32) + b_ref[0]


def _ada_mods(cc, ada_w, ada_b):
    nt = 1536
    n6 = 6 * D_MODEL
    return pl.pallas_call(
        _ada_kernel,
        grid=(DEPTH, n6 // nt),
        in_specs=[pl.BlockSpec((8, D_MODEL), lambda l, j: (0, 0)),
                  pl.BlockSpec((1, D_MODEL, nt), lambda l, j: (l, 0, j)),
                  pl.BlockSpec((1, 1, nt), lambda l, j: (l, 0, j))],
        out_specs=pl.BlockSpec((1, 8, nt), lambda l, j: (l, 0, j)),
        out_shape=jax.ShapeDtypeStruct((DEPTH, 8, n6), F32),
        compiler_params=_cparams(2),
        name="ada_mods",
    )(cc, ada_w, ada_b.reshape(DEPTH, 1, n6))


def _proj_da_kernel(x_ref, mod_ref, cos_ref, sin_ref, wqk_ref, wv_ref, q_ref, k_ref, v_ref):
    m = mod_ref[0]
    t = (x_ref[0] * m[MOD_SC1:MOD_SC1 + 1] + m[MOD_SH1:MOD_SH1 + 1]).astype(BF16)
    cos = cos_ref[...]
    sin = sin_ref[...]
    q_scale = DA_HEAD_DIM ** -0.5 * math.log2(math.e)
    for j in range(DA_HEADS):
        y2 = jnp.dot(t, wqk_ref[:, j * 256:(j + 1) * 256], preferred_element_type=F32)
        for hh in range(2):
            y = y2[:, hh * LANES:(hh + 1) * LANES]
            y = y * cos + pltpu.roll(y, 64, 1) * sin
            col = (2 * j + hh) * LANES
            if col < D_MODEL:
                q_ref[0, :, col:col + LANES] = (y * q_scale).astype(BF16)
            else:
                k_ref[0, :, col - D_MODEL:col - D_MODEL + LANES] = y.astype(BF16)
    v_ref[0] = jnp.dot(t, wv_ref[...], preferred_element_type=F32).astype(BF16)


def _proj_da(xa, mods, cos, sin, wqk, wv):
    B, LT, D = xa.shape
    nb = LT // TM
    nbl = nb - 1
    out = jax.ShapeDtypeStruct((B, LT, D), BF16)
    blk = pl.BlockSpec((1, TM, D), lambda b, i: (b, i, 0))
    return pl.pallas_call(
        _proj_da_kernel,
        grid=(B, nb),
        in_specs=[blk,
                  pl.BlockSpec((1, 8, D), lambda b, i: (2 * b + i // nbl, 0, 0)),
                  pl.BlockSpec((TM, LANES), lambda b, i: (i, 0)),
                  pl.BlockSpec((TM, LANES), lambda b, i: (i, 0)),
                  pl.BlockSpec((D, 2 * D), lambda b, i: (0, 0)),
                  pl.BlockSpec((D, D), lambda b, i: (0, 0))],
        out_specs=[blk, blk, blk],
        out_shape=[out, out, out],
        compiler_params=_cparams(2),
        name="da_proj",
    )(xa, mods, cos, sin, wqk, wv)


def _attn_kernel(lam_ref, q_ref, k_ref, v_ref, sw_ref, o_ref, vext_sc, m_sc, acc_sc, s_sc, *,
                 n_lat, n_ctx, lam_init):
    i = pl.program_id(2)
    tq = q_ref.shape[1]

    @pl.when(i == 0)
    def _():
        vext_sc[:, :DA_HEAD_W] = v_ref[0]
        vext_sc[:, DA_HEAD_W:] = jnp.ones((vext_sc.shape[0], DA_HEAD_W), BF16)

    q = q_ref[0]
    lane = lax.broadcasted_iota(jnp.int32, (1, DA_HEAD_W), 1)
    map0 = (lane % 64) < 32
    zero = jnp.zeros_like(q)
    qs = jnp.concatenate([jnp.where(map0, q, zero), jnp.where(map0, zero, q)], axis=0)
    m_sc[...] = jnp.full(m_sc.shape, -jnp.inf, F32)
    acc_sc[...] = jnp.zeros(acc_sc.shape, F32)

    def scores(off, tk):
        k = k_ref[0, pl.ds(off, tk), :]
        return lax.dot_general(qs, k, (((1,), (1,)), ((), ())), preferred_element_type=F32)

    def accumulate(s, off, tk):
        m_prev = m_sc[...]
        m_new = jnp.maximum(m_prev, jnp.max(s, axis=1, keepdims=True))
        alpha = jnp.exp2(m_prev - m_new)
        p = jnp.exp2(s - jnp.tile(m_new, (1, tk // LANES)))
        pv = jnp.dot(p.astype(BF16), vext_sc[pl.ds(off, tk), :], preferred_element_type=F32)
        acc_sc[...] = jnp.tile(alpha, (1, 2)) * acc_sc[...] + pv
        m_sc[...] = m_new

    n_q_lat = n_lat // tq
    tk = s_sc.shape[2]
    n_steps = (n_lat + n_ctx) // tk

    @pl.when(i < n_q_lat)
    def _():
        s_sc[0] = scores(0, tk)
        for t in range(n_steps):
            if t + 1 < n_steps:
                s_sc[(t + 1) % 2] = scores((t + 1) * tk, tk)
            accumulate(s_sc[t % 2], t * tk, tk)

    @pl.when(i >= n_q_lat)
    def _():
        accumulate(scores(n_lat, n_ctx), n_lat, n_ctx)

    acc = acc_sc[...]
    o0 = acc[:tq, :DA_HEAD_W] / acc[:tq, DA_HEAD_W:DA_HEAD_W + 1]
    o1 = acc[tq:, :DA_HEAD_W] / acc[tq:, DA_HEAD_W:DA_HEAD_W + 1]
    o = o0 - lam_ref[0] * o1
    o = o * lax.rsqrt(jnp.mean(o * o, axis=-1, keepdims=True) + NORM_EPS)
    o_ref[0] = (o * sw_ref[...] * (1.0 - lam_init)).astype(BF16)


def _diff_attention(lam, q, k, v, subln_w, lam_init):
    B, LT, D = q.shape
    nb = LT // TM
    tk = max(t for t in range(TM, ATT_TK_MAX + 1, TM) if LT % t == 0)
    kern = functools.partial(_attn_kernel, n_lat=LT - TM, n_ctx=TM, lam_init=lam_init)
    grid_spec = pltpu.PrefetchScalarGridSpec(
        num_scalar_prefetch=1,
        grid=(B, DA_HEADS, nb),
        in_specs=[pl.BlockSpec((1, TM, DA_HEAD_W), lambda b, h, i, lam: (b, i, h)),
                  pl.BlockSpec((1, LT, DA_HEAD_W), lambda b, h, i, lam: (b, 0, h)),
                  pl.BlockSpec((1, LT, DA_HEAD_W), lambda b, h, i, lam: (b, 0, h)),
                  pl.BlockSpec((1, DA_HEAD_W), lambda b, h, i, lam: (0, 0))],
        out_specs=pl.BlockSpec((1, TM, DA_HEAD_W), lambda b, h, i, lam: (b, i, h)),
        scratch_shapes=[pltpu.VMEM((LT, 2 * DA_HEAD_W), BF16),
                        pltpu.VMEM((2 * TM, LANES), F32),
                        pltpu.VMEM((2 * TM, 2 * DA_HEAD_W), F32),
                        pltpu.VMEM((2, 2 * TM, tk), F32)],
    )
    return pl.pallas_call(
        kern,
        grid_spec=grid_spec,
        out_shape=jax.ShapeDtypeStruct((B, LT, D), BF16),
        compiler_params=_cparams(3),
        name="diff_attn",
    )(lam, q, k, v, subln_w.reshape(1, DA_HEAD_W))


def _route_block(logits, cnt_sc):
    lane = lax.broadcasted_iota(jnp.int32, logits.shape, 1)
    lane_f = lane.astype(F32)
    work = jnp.where(lane < N_EXPERTS, logits, -jnp.inf)
    tops, idxs, hits = [], [], []
    for _ in range(TOP_K):
        mk = jnp.max(work, axis=1, keepdims=True)
        ik = jnp.min(jnp.where(work == mk, lane_f, float(LANES)), axis=1, keepdims=True)
        hit = lane_f == ik
        tops.append(mk)
        idxs.append(ik)
        hits.append(hit)
        work = jnp.where(hit, -jnp.inf, work)
    chosen = functools.reduce(jnp.logical_or, hits).astype(F32)
    n = logits.shape[0]
    row = lax.broadcasted_iota(jnp.int32, (n, n), 0)
    col = lax.broadcasted_iota(jnp.int32, (n, n), 1)
    before = jnp.dot((col < row).astype(BF16), chosen.astype(BF16), preferred_element_type=F32)
    rank_all = cnt_sc[...] + before
    cnt_sc[...] = cnt_sc[...] + jnp.sum(chosen, axis=0, keepdims=True)
    exps = [jnp.exp(t - tops[0]) for t in tops]
    denom = functools.reduce(jnp.add, exps)
    table = jnp.zeros(logits.shape, F32)
    for k in range(TOP_K):
        rk = jnp.sum(jnp.where(hits[k], rank_all, 0.0), axis=1, keepdims=True)
        table = jnp.where(lane == k, idxs[k], table)
        table = jnp.where(lane == TOP_K + k, rk, table)
        table = jnp.where(lane == 2 * TOP_K + k, exps[k] / denom, table)
    return table


def _post_mixer(pre, w_ref, x_ref, mod_ref, ln_ref, rw_ref, rb_ref, x1_ref, u_ref, rt_ref, cnt_ref,
                cnt_sc):
    @pl.when((pl.program_id(0) == 0) & (pl.program_id(1) == 0))
    def _():
        cnt_sc[...] = jnp.zeros(cnt_sc.shape, F32)

    m = mod_ref[0]
    y = jnp.dot(pre, w_ref[...], preferred_element_type=F32)
    z = DEEPNORM_ALPHA * x_ref[0] + m[MOD_G1:MOD_G1 + 1] * y
    mu = jnp.mean(z, axis=-1, keepdims=True)
    zc = z - mu
    x1 = zc * lax.rsqrt(jnp.mean(zc * zc, axis=-1, keepdims=True) + NORM_EPS)
    x1 = x1 * ln_ref[0:1] + ln_ref[1:2]
    x1_ref[0] = x1
    u = x1 * m[MOD_SC2:MOD_SC2 + 1] + m[MOD_SH2:MOD_SH2 + 1]
    u_hi = u.astype(BF16)
    u_ref[0] = u_hi
    u_lo = (u - u_hi.astype(F32)).astype(BF16)
    d_hi = jnp.dot(u_hi, rw_ref[...], preferred_element_type=F32)
    d_lo = jnp.dot(u_lo, rw_ref[:, :LANES], preferred_element_type=F32)
    logits = d_hi[:, :LANES] + d_hi[:, LANES:] + d_lo + rb_ref[...]
    rt_ref[0] = _route_block(logits, cnt_sc)
    cnt_ref[...] = jnp.broadcast_to(cnt_sc[...], cnt_ref.shape)


def _out_da_kernel(o_ref, w_ref, x_ref, mod_ref, ln_ref, rw_ref, rb_ref, x1_ref, u_ref, rt_ref,
                   cnt_ref, cnt_sc):
    _post_mixer(o_ref[0], w_ref, x_ref, mod_ref, ln_ref, rw_ref, rb_ref, x1_ref, u_ref, rt_ref,
                cnt_ref, cnt_sc)


def _out_gla_kernel(of_ref, ob_ref, r_ref, nw_ref, w_ref, x_ref, mod_ref, ln_ref, rw_ref, rb_ref,
                    x1_ref, u_ref, rt_ref, cnt_ref, cnt_sc):
    parts = []
    for h in range(GLA_HEADS):
        sl = slice(h * GLA_DV_HEAD, (h + 1) * GLA_DV_HEAD)
        o = of_ref[0, :, sl] + ob_ref[0, :, sl]
        o = o * lax.rsqrt(jnp.mean(o * o, axis=-1, keepdims=True) + NORM_EPS) * nw_ref[...]
        r = r_ref[0, :, sl]
        parts.append((o * (r * jax.nn.sigmoid(r))).astype(BF16))
    pre = jnp.concatenate(parts, axis=1)
    _post_mixer(pre, w_ref, x_ref, mod_ref, ln_ref, rw_ref, rb_ref, x1_ref, u_ref, rt_ref, cnt_ref,
                cnt_sc)


def _mixer_out(kind, acts, w_out, xa, mods, lnp, rw, rb, nb_out, norm_w=None):
    B, LT, D = xa.shape
    nbl = LT // TM - 1
    blk = pl.BlockSpec((1, TM, D), lambda b, i: (b, i, 0))
    common_specs = [pl.BlockSpec((D, D), lambda b, i: (0, 0)),
                    blk,
                    pl.BlockSpec((1, 8, D), lambda b, i: (2 * b + i // nbl, 0, 0)),
                    pl.BlockSpec((2, D), lambda b, i: (0, 0)),
                    pl.BlockSpec((D, 2 * LANES), lambda b, i: (0, 0)),
                    pl.BlockSpec((1, LANES), lambda b, i: (0, 0))]
    lout = nb_out * TM
    out_shape = [jax.ShapeDtypeStruct((B, lout, D), F32),
                 jax.ShapeDtypeStruct((B, lout, D), BF16),
                 jax.ShapeDtypeStruct((B, lout, LANES), F32),
                 jax.ShapeDtypeStruct((8, LANES), F32)]
    out_specs = [blk, blk, pl.BlockSpec((1, TM, LANES), lambda b, i: (b, i, 0)),
                 pl.BlockSpec((8, LANES), lambda b, i: (0, 0))]
    if kind == "da":
        kern = _out_da_kernel
        in_specs = [blk] + common_specs
        args = list(acts)
    else:
        kern = _out_gla_kernel
        in_specs = [blk, blk, blk, pl.BlockSpec((1, GLA_DV_HEAD), lambda b, i: (0, 0))] + common_specs
        args = list(acts) + [norm_w.reshape(1, GLA_DV_HEAD)]
    return pl.pallas_call(
        kern,
        grid=(B, nb_out),
        in_specs=in_specs,
        out_specs=out_specs,
        out_shape=out_shape,
        scratch_shapes=[pltpu.VMEM((1, LANES), F32)],
        compiler_params=_cparams(2),
        name="mixer_out_" + kind,
    )(*args, w_out, xa, mods, lnp, rw, rb)


def _proj_gla_kernel(x_ref, mod_ref, w_ref, wz_ref, wg_ref, bg_ref,
                     q_ref, k_ref, v_ref, r_ref, g_ref):
    m = mod_ref[0]
    t = (x_ref[0] * m[MOD_SC1:MOD_SC1 + 1] + m[MOD_SH1:MOD_SH1 + 1]).astype(BF16)
    c0, c1, c2, c3 = GLA_DK, 2 * GLA_DK, 2 * GLA_DK + GLA_DV, 2 * GLA_DK + 2 * GLA_DV
    q_ref[0] = jnp.dot(t, w_ref[:, :c0], preferred_element_type=F32) * (GLA_DK_HEAD ** -0.5)
    k_ref[0] = jnp.dot(t, w_ref[:, c0:c1], preferred_element_type=F32)
    v_ref[0] = jnp.dot(t, w_ref[:, c1:c2], preferred_element_type=F32).astype(BF16)
    r_ref[0] = jnp.dot(t, w_ref[:, c2:c3], preferred_element_type=F32)
    z = jnp.dot(t, wz_ref[...], preferred_element_type=F32)
    gl = jnp.dot(z, wg_ref[...], preferred_element_type=F32) + bg_ref[...]
    log_sig = jnp.minimum(gl, 0.0) - jnp.log1p(jnp.exp(-jnp.abs(gl)))
    g_ref[0] = log_sig * (1.0 / GLA_TAU)


def _proj_gla(xa, mods, w_main, wz, wg, bg):
    B, LT, D = xa.shape
    nb = LT // TM
    nbl = nb - 1
    blk = lambda w: pl.BlockSpec((1, TM, w), lambda b, i: (b, i, 0))
    return pl.pallas_call(
        _proj_gla_kernel,
        grid=(B, nb),
        in_specs=[blk(D),
                  pl.BlockSpec((1, 8, D), lambda b, i: (2 * b + i // nbl, 0, 0)),
                  pl.BlockSpec(w_main.shape, lambda b, i: (0, 0)),
                  pl.BlockSpec(wz.shape, lambda b, i: (0, 0)),
                  pl.BlockSpec(wg.shape, lambda b, i: (0, 0)),
                  pl.BlockSpec(bg.shape, lambda b, i: (0, 0))],
        out_specs=[blk(GLA_DK), blk(GLA_DK), blk(GLA_DV), blk(GLA_DV), blk(2 * GLA_DK)],
        out_shape=[jax.ShapeDtypeStruct((B, LT, GLA_DK), F32),
                   jax.ShapeDtypeStruct((B, LT, GLA_DK), F32),
                   jax.ShapeDtypeStruct((B, LT, GLA_DV), BF16),
                   jax.ShapeDtypeStruct((B, LT, GLA_DV), F32),
                   jax.ShapeDtypeStruct((B, LT, 2 * GLA_DK), F32)],
        compiler_params=_cparams(2),
        name="gla_proj",
    )(xa, mods, w_main, wz, wg, bg)


def _gla_scan_kernel(q_ref, k_ref, v_ref, g_ref, o_ref, st_sc, *, reverse):
    j = pl.program_id(1)

    @pl.when(j == 0)
    def _():
        st_sc[...] = jnp.zeros(st_sc.shape, F32)

    C = GLA_CHUNK
    n_chunks = TM // C

    def causal(n):
        row = lax.broadcasted_iota(jnp.int32, (n, n), 0)
        col = lax.broadcasted_iota(jnp.int32, (n, n), 1)
        return (row // C == col // C) & ((col >= row) if reverse else (col <= row))

    keep = causal(C)
    tri = causal(TM).astype(BF16)
    g = g_ref[0]
    g_hi = g.astype(BF16)
    rem = g - g_hi.astype(F32)
    g_mid = rem.astype(BF16)
    g_lo = (rem - g_mid.astype(F32)).astype(BF16)
    b_all = (jnp.dot(tri, g_hi, preferred_element_type=F32)
             + jnp.dot(tri, g_mid, preferred_element_type=F32)
             + jnp.dot(tri, g_lo, preferred_element_type=F32))

    states = [st_sc[h] for h in range(GLA_HEADS)]
    order = range(n_chunks - 1, -1, -1) if reverse else range(n_chunks)
    for c in order:
        rows = slice(c * C, (c + 1) * C)
        for h in range(GLA_HEADS):
            ks = slice(h * GLA_DK_HEAD, (h + 1) * GLA_DK_HEAD)
            vs = slice(h * GLA_DV_HEAD, (h + 1) * GLA_DV_HEAD)
            b = b_all[rows, ks]
            tot = b[0:1] if reverse else b[C - 1:C]
            q = q_ref[0, rows, ks]
            k = k_ref[0, rows, ks]
            q_in = (q * jnp.exp(b)).astype(BF16)
            k_in = (k * jnp.exp(-b)).astype(BF16)
            k_st = (k * jnp.exp(tot - b)).astype(BF16)
            att = lax.dot_general(q_in, k_in, (((1,), (1,)), ((), ())), preferred_element_type=F32)
            att = jnp.where(keep, att, 0.0).astype(BF16)
            v = v_ref[0, rows, vs]
            st = states[h]
            o = jnp.dot(att, v, preferred_element_type=F32)
            o = o + lax.dot_general(q_in, st.astype(BF16), (((1,), (1,)), ((), ())),
                                    preferred_element_type=F32)
            o_ref[0, rows, vs] = o
            ds = lax.dot_general(v, k_st, (((0,), (0,)), ((), ())), preferred_element_type=F32)
            states[h] = st * jnp.exp(tot) + ds
    for h in range(GLA_HEADS):
        st_sc[h] = states[h]


def _gla_scan(q, k, v, g, reverse):
    B, LT, _ = q.shape
    nb = LT // TM
    ctx_blk = nb - 1
    if reverse:
        order = lambda j: jnp.where(j == 0, ctx_blk, ctx_blk - j)
    else:
        order = lambda j: jnp.where(j == 0, ctx_blk, j - 1)
    gcol = 1 if reverse else 0
    return pl.pallas_call(
        functools.partial(_gla_scan_kernel, reverse=reverse),
        grid=(B, nb),
        in_specs=[pl.BlockSpec((1, TM, GLA_DK), lambda b, j: (b, order(j), 0)),
                  pl.BlockSpec((1, TM, GLA_DK), lambda b, j: (b, order(j), 0)),
                  pl.BlockSpec((1, TM, GLA_DV), lambda b, j: (b, order(j), 0)),
                  pl.BlockSpec((1, TM, GLA_DK), lambda b, j: (b, order(j), gcol))],
        out_specs=pl.BlockSpec((1, TM, GLA_DV), lambda b, j: (b, order(j), 0)),
        out_shape=jax.ShapeDtypeStruct((B, LT, GLA_DV), F32),
        scratch_shapes=[pltpu.VMEM((GLA_HEADS, GLA_DV_HEAD, GLA_DK_HEAD), F32)],
        compiler_params=_cparams(2),
        name="gla_scan_bwd" if reverse else "gla_scan_fwd",
    )(q, k, v, g)


def _moe_kernel(be_ref, nu_ref, x_ref, wgu_ref, bgu_ref, wd_ref, bd_ref, y_ref, wgu_sc, wd_sc):
    i = pl.program_id(0)

    @pl.when(i < nu_ref[0])
    def _():
        @pl.when((i == 0) | (be_ref[i] != be_ref[jnp.maximum(i - 1, 0)]))
        def _():
            wgu_sc[...] = wgu_ref[0, 0].astype(BF16)
            wd_sc[...] = wd_ref[0, 0].astype(BF16)

        gu = jnp.dot(x_ref[...], wgu_sc[...], preferred_element_type=F32) + bgu_ref[0, 0]
        half = gu.shape[1] // 2
        glu = jnp.minimum(gu[:, :half], SWIGLU_LIMIT)
        lin = jnp.clip(gu[:, half:], -SWIGLU_LIMIT, SWIGLU_LIMIT)
        act = glu * jax.nn.sigmoid(SWIGLU_ALPHA * glu) * (lin + 1.0)
        y = jnp.dot(act.astype(BF16), wd_sc[...], preferred_element_type=F32) + bd_ref[0, 0]
        y_ref[...] = y.astype(y_ref.dtype)

    @pl.when(i >= nu_ref[0])
    def _():
        y_ref[...] = jnp.zeros(y_ref.shape, y_ref.dtype)


def _moe_experts(layer, block_expert, n_used, xb, w_gu, b_gu, w_down, b_down):
    n_rows, D = xb.shape
    n_blocks = n_rows // MOE_BLOCK
    grid_spec = pltpu.PrefetchScalarGridSpec(
        num_scalar_prefetch=2,
        grid=(n_blocks,),
        in_specs=[pl.BlockSpec((MOE_BLOCK, D), lambda i, be, nu: (i, 0)),
                  pl.BlockSpec((1, 1, D, 2 * D), lambda i, be, nu: (layer, be[i], 0, 0)),
                  pl.BlockSpec((1, 1, 1, 2 * D), lambda i, be, nu: (layer, be[i], 0, 0)),
                  pl.BlockSpec((1, 1, D, D), lambda i, be, nu: (layer, be[i], 0, 0)),
                  pl.BlockSpec((1, 1, 1, D), lambda i, be, nu: (layer, be[i], 0, 0))],
        out_specs=pl.BlockSpec((MOE_BLOCK, D), lambda i, be, nu: (i, 0)),
        scratch_shapes=[pltpu.VMEM((D, 2 * D), BF16), pltpu.VMEM((D, D), BF16)],
    )
    return pl.pallas_call(
        _moe_kernel,
        grid_spec=grid_spec,
        out_shape=jax.ShapeDtypeStruct((n_rows, D), BF16),
        compiler_params=pltpu.CompilerParams(dimension_semantics=("arbitrary",),
                                             vmem_limit_bytes=MOE_VMEM_LIMIT),
        name="moe_experts",
    )(block_expert, n_used, xb, w_gu, b_gu.reshape(DEPTH, N_EXPERTS, 1, 2 * D), w_down,
      b_down.reshape(DEPTH, N_EXPERTS, 1, D))


def _moe_ffn(layer, u, route, counts, w_gu, b_gu, w_down, b_down):
    T, D = u.shape
    expert = route[:, :TOP_K].astype(jnp.int32)
    rank = route[:, TOP_K:2 * TOP_K].astype(jnp.int32)
    cnt = counts[0, :N_EXPERTS].astype(jnp.int32)
    padded = (cnt + MOE_BLOCK - 1) // MOE_BLOCK * MOE_BLOCK
    padded_end = jnp.cumsum(padded)
    padded_start = padded_end - padded
    pos = (padded_start[expert] + rank).T
    n_blocks = -(-T * TOP_K // MOE_BLOCK) + N_EXPERTS
    n_rows = n_blocks * MOE_BLOCK
    block_start = jnp.arange(n_blocks, dtype=jnp.int32) * MOE_BLOCK
    block_expert = jnp.minimum(jnp.sum(padded_end[None, :] <= block_start[:, None], axis=1),
                               N_EXPERTS - 1).astype(jnp.int32)
    n_used = (padded_end[-1:] // MOE_BLOCK).astype(jnp.int32)
    tok = jnp.tile(jnp.arange(T, dtype=jnp.int32), TOP_K)
    _, sorted_tok = lax.sort((pos.reshape(-1), tok), num_keys=1)
    row = jnp.arange(n_rows, dtype=jnp.int32)
    row_e = jnp.repeat(block_expert, MOE_BLOCK)
    row_rank = row - padded_start[row_e]
    compact = jnp.cumsum(cnt)[row_e] - cnt[row_e] + row_rank
    row_tok = jnp.where(row_rank < cnt[row_e], sorted_tok[jnp.minimum(compact, T * TOP_K - 1)], 0)
    xb = u[row_tok]
    yb = _moe_experts(layer, block_expert, n_used, xb, w_gu, b_gu, w_down, b_down)
    return yb[pos.reshape(-1)].reshape(TOP_K, T, D)


def _final_ln_kernel(x_ref, y_ref, rt_ref, mod_ref, ln_ref, o_ref):
    m = mod_ref[0]
    rt = rt_ref[0]
    f = rt[:, 2 * TOP_K:2 * TOP_K + 1] * y_ref[0].astype(F32)
    for k in range(1, TOP_K):
        f = f + rt[:, 2 * TOP_K + k:2 * TOP_K + k + 1] * y_ref[k].astype(F32)
    z = DEEPNORM_ALPHA * x_ref[0] + m[MOD_G2:MOD_G2 + 1] * f
    mu = jnp.mean(z, axis=-1, keepdims=True)
    zc = z - mu
    y = zc * lax.rsqrt(jnp.mean(zc * zc, axis=-1, keepdims=True) + NORM_EPS)
    o_ref[0] = y * ln_ref[0:1] + ln_ref[1:2]


def _final_ln(x1, yg, route, mods, lnp, nbl):
    B, L, D = x1.shape
    nblk = L // TM
    blk = pl.BlockSpec((1, TM, D), lambda b, i: (b, i, 0))
    return pl.pallas_call(
        _final_ln_kernel,
        grid=(B, nblk),
        in_specs=[blk,
                  pl.BlockSpec((TOP_K, TM, D), lambda b, i: (0, b * nblk + i, 0)),
                  pl.BlockSpec((1, TM, LANES), lambda b, i: (b, i, 0)),
                  pl.BlockSpec((1, 8, D), lambda b, i: (2 * b + i // nbl, 0, 0)),
                  pl.BlockSpec((2, D), lambda b, i: (0, 0))],
        out_specs=blk,
        out_shape=jax.ShapeDtypeStruct((B, L, D), F32),
        compiler_params=_cparams(2),
        name="final_ln",
    )(x1, yg, route, mods, lnp)


def _rope_tables(S, n_ctx):
    rows = S // GRID_W
    row = jnp.repeat(jnp.arange(rows), GRID_W).astype(F32)
    col = jnp.tile(jnp.arange(GRID_W), rows).astype(F32)
    inv = ROPE_BASE ** (-jnp.arange(ROPE_PAIRS_AXIS, dtype=F32) / ROPE_PAIRS_AXIS)
    ang = jnp.concatenate([row[:, None] * inv, col[:, None] * inv], -1)
    cos, sin = jnp.cos(ang), jnp.sin(ang)
    cos = jnp.concatenate([cos, jnp.ones((n_ctx, cos.shape[1]), F32)], axis=0)
    sin = jnp.concatenate([sin, jnp.zeros((n_ctx, sin.shape[1]), F32)], axis=0)
    return (jnp.concatenate([cos, cos, cos, cos], axis=1),
            jnp.concatenate([-sin, -sin, sin, sin], axis=1))


def _qk_column_perm():
    lane = np.arange(DA_HEAD_W)
    half, mp, jj = lane // 64, (lane % 64) // 32, lane % 32
    src = mp * DA_HEAD_DIM + half * 32 + jj
    head = np.arange(DA_HEADS)[:, None] * DA_HEAD_W
    perm = (head + src[None, :]).reshape(-1)
    return np.concatenate([perm, D_MODEL + perm])


def _split_router_w(w):
    w = jnp.pad(w, ((0, 0), (0, LANES - N_EXPERTS)))
    hi = w.astype(BF16)
    lo = (w - hi.astype(F32)).astype(BF16)
    return jnp.concatenate([hi, lo], axis=1)


def _layer_mods(mod_rows, B):
    D = D_MODEL
    parts = mod_rows.reshape(8, 6, D)
    sh1, sc1, g1, sh2, sc2, g2 = (parts[:, n] for n in range(6))
    tab = jnp.stack([1.0 + sc1, sh1, g1, 1.0 + sc2, sh2, g2, jnp.zeros_like(g1),
                     jnp.zeros_like(g1)], axis=1)
    lat = tab[:B]
    ctx = jnp.broadcast_to(tab[B:B + 1], (B, 8, D))
    return jnp.stack([lat, ctx], axis=1).reshape(2 * B, 8, D)


def kernel(x, c, ctx, c_ctx, ada_w, ada_b, ln_g, ln_b, da_w_in, da_w_out, da_lambda, da_subln_w,
           gla_w_in, gla_w_gate, gla_b_gate, gla_norm_w, gla_w_out, router_w, router_b,
           moe_w_gu, moe_b_gu, moe_w_down, moe_b_down):
    B, S, D = x.shape
    n_ctx = ctx.shape[1]
    assert D == D_MODEL and n_ctx == TM and S % TM == 0 and S % GRID_W == 0 and B + 1 <= 8
    nbl = S // TM
    nb = nbl + 1

    cc = jnp.concatenate([c, c_ctx[None, :], jnp.zeros((8 - B - 1, D), F32)], axis=0)
    mod_all = _ada_mods(cc, ada_w, ada_b)
    xa = jnp.concatenate([x, ctx], axis=1)

    rw = [_split_router_w(router_w[i]) for i in range(DEPTH)]
    rb = [jnp.pad(router_b[i], (0, LANES - N_EXPERTS)).reshape(1, LANES) for i in range(DEPTH)]
    lnp = [[jnp.stack([ln_g[i, n], ln_b[i, n]]) for n in range(2)] for i in range(DEPTH)]

    mods = _layer_mods(mod_all[0], B)
    w_in = da_w_in[0]
    wqk = w_in[:, _qk_column_perm()].astype(BF16)
    wv = w_in[:, 2 * D:].astype(BF16)
    cos, sin = _rope_tables(S, n_ctx)
    q, k, v = _proj_da(xa, mods, cos, sin, wqk, wv)
    lam_init = _lambda_init(0)
    lv = da_lambda[0].astype(F32)
    lam = (jnp.exp(jnp.sum(lv[0] * lv[1])) - jnp.exp(jnp.sum(lv[2] * lv[3])) + lam_init).reshape(1)
    o = _diff_attention(lam, q, k, v, da_subln_w[0], lam_init)
    x1, u, route, counts = _mixer_out("da", [o], da_w_out[0].astype(BF16), xa, mods, lnp[0][0],
                                      rw[0], rb[0], nb)
    yg = _moe_ffn(0, u.reshape(B * nb * TM, D), route.reshape(B * nb * TM, LANES), counts,
                  moe_w_gu, moe_b_gu, moe_w_down, moe_b_down)
    xa = _final_ln(x1, yg, route, mods, lnp[0][1], nbl)

    mods = _layer_mods(mod_all[1], B)
    gw = gla_w_in[0]
    c3 = 2 * GLA_DK + 2 * GLA_DV
    w_main = gw[:, :c3].astype(BF16)
    wz = jnp.pad(gw[:, c3:], ((0, 0), (0, LANES - 2 * GLA_GATE_RANK))).astype(BF16)
    wg = jnp.zeros((LANES, 2 * GLA_DK), F32)
    wg = wg.at[:GLA_GATE_RANK, :GLA_DK].set(gla_w_gate[0, 0])
    wg = wg.at[GLA_GATE_RANK:2 * GLA_GATE_RANK, GLA_DK:].set(gla_w_gate[0, 1])
    bg = gla_b_gate[0].reshape(1, 2 * GLA_DK)
    gq, gk, gv, gr, gg = _proj_gla(xa, mods, w_main, wz, wg, bg)
    of = _gla_scan(gq, gk, gv, gg, reverse=False)
    ob = _gla_scan(gq, gk, gv, gg, reverse=True)
    x1, u, route, counts = _mixer_out("gla", [of, ob, gr], gla_w_out[0].astype(BF16), xa, mods,
                                      lnp[1][0], rw[1], rb[1], nbl, norm_w=gla_norm_w[0])
    yg = _moe_ffn(1, u.reshape(B * S, D), route.reshape(B * S, LANES), counts,
                  moe_w_gu, moe_b_gu, moe_w_down, moe_b_down)
    return _final_ln(x1, yg, route, mods, lnp[1][1], nbl)
```

```python
import functools
import math

import numpy as np
import jax
import jax.numpy as jnp
from jax import lax
from jax.experimental import pallas as pl
from jax.experimental.pallas import tpu as pltpu

F32 = jnp.float32
BF16 = jnp.bfloat16

D_MODEL = 1024
DEPTH = 2
GRID_W = 64

DA_HEADS = 8
DA_HEAD_DIM = 64
DA_HEAD_W = 2 * DA_HEAD_DIM
ROPE_BASE = 10000.0
ROPE_PAIRS_AXIS = DA_HEAD_DIM // 4

GLA_HEADS = 4
GLA_DK = D_MODEL // 2
GLA_DV = D_MODEL
GLA_DK_HEAD = GLA_DK // GLA_HEADS
GLA_DV_HEAD = GLA_DV // GLA_HEADS
GLA_GATE_RANK = 16
GLA_TAU = 16.0
GLA_CHUNK = 64

N_EXPERTS = 32
TOP_K = 4
SWIGLU_ALPHA = 1.702
SWIGLU_LIMIT = 7.0
MOE_BLOCK = 256

DEEPNORM_ALPHA = (2.0 * DEPTH) ** 0.25
NORM_EPS = 1e-5

LANES = 128
TM = 256
ATT_TK_MAX = 2816
VMEM_LIMIT = 48 * 1024 * 1024
MOE_VMEM_LIMIT = 56 * 1024 * 1024

MOD_SC1, MOD_SH1, MOD_G1, MOD_SC2, MOD_SH2, MOD_G2 = range(6)


def _cparams(n_axes):
    return pltpu.CompilerParams(dimension_semantics=("arbitrary",) * n_axes,
                                vmem_limit_bytes=VMEM_LIMIT)


def _lambda_init(layer_idx):
    return 0.8 - 0.6 * math.exp(-0.3 * layer_idx)


def _ada_kernel(c_ref, w_ref, b_ref, o_ref):
    c = c_ref[...]
    s = c * jax.nn.sigmoid(c)
    o_ref[0] = jnp.dot(s, w_ref[0], preferred_element_type=F32) + b_ref[0]


def _ada_mods(cc, ada_w, ada_b):
    nt = 1536
    n6 = 6 * D_MODEL
    return pl.pallas_call(
        _ada_kernel,
        grid=(DEPTH, n6 // nt),
        in_specs=[pl.BlockSpec((8, D_MODEL), lambda l, j: (0, 0)),
                  pl.BlockSpec((1, D_MODEL, nt), lambda l, j: (l, 0, j)),
                  pl.BlockSpec((1, 1, nt), lambda l, j: (l, 0, j))],
        out_specs=pl.BlockSpec((1, 8, nt), lambda l, j: (l, 0, j)),
        out_shape=jax.ShapeDtypeStruct((DEPTH, 8, n6), F32),
        compiler_params=_cparams(2),
        name="ada_mods",
    )(cc, ada_w, ada_b.reshape(DEPTH, 1, n6))


def _proj_da_kernel(x_ref, mod_ref, cos_ref, sin_ref, wqk_ref, wv_ref, q_ref, k_ref, v_ref):
    m = mod_ref[0]
    t = (x_ref[0] * m[MOD_SC1:MOD_SC1 + 1] + m[MOD_SH1:MOD_SH1 + 1]).astype(BF16)
    cos = cos_ref[...]
    sin = sin_ref[...]
    q_scale = DA_HEAD_DIM ** -0.5 * math.log2(math.e)
    for j in range(DA_HEADS):
        y2 = jnp.dot(t, wqk_ref[:, j * 256:(j + 1) * 256], preferred_element_type=F32)
        for hh in range(2):
            y = y2[:, hh * LANES:(hh + 1) * LANES]
            y = y * cos + pltpu.roll(y, 64, 1) * sin
            col = (2 * j + hh) * LANES
            if col < D_MODEL:
                q_ref[0, :, col:col + LANES] = (y * q_scale).astype(BF16)
            else:
                k_ref[0, :, col - D_MODEL:col - D_MODEL + LANES] = y.astype(BF16)
    v_ref[0] = jnp.dot(t, wv_ref[...], preferred_element_type=F32).astype(BF16)


def _proj_da(xa, mods, cos, sin, wqk, wv):
    B, LT, D = xa.shape
    nb = LT // TM
    nbl = nb - 1
    out = jax.ShapeDtypeStruct((B, LT, D), BF16)
    blk = pl.BlockSpec((1, TM, D), lambda b, i: (b, i, 0))
    return pl.pallas_call(
        _proj_da_kernel,
        grid=(B, nb),
        in_specs=[blk,
                  pl.BlockSpec((1, 8, D), lambda b, i: (2 * b + i // nbl, 0, 0)),
                  pl.BlockSpec((TM, LANES), lambda b, i: (i, 0)),
                  pl.BlockSpec((TM, LANES), lambda b, i: (i, 0)),
                  pl.BlockSpec((D, 2 * D), lambda b, i: (0, 0)),
                  pl.BlockSpec((D, D), lambda b, i: (0, 0))],
        out_specs=[blk, blk, blk],
        out_shape=[out, out, out],
        compiler_params=_cparams(2),
        name="da_proj",
    )(xa, mods, cos, sin, wqk, wv)


def _attn_kernel(lam_ref, q_ref, k_ref, v_ref, sw_ref, o_ref, vext_sc, m_sc, acc_sc, s_sc, *,
                 n_lat, n_ctx, lam_init):
    i = pl.program_id(2)
    tq = q_ref.shape[1]

    @pl.when(i == 0)
    def _():
        vext_sc[:, :DA_HEAD_W] = v_ref[0]
        vext_sc[:, DA_HEAD_W:] = jnp.ones((vext_sc.shape[0], DA_HEAD_W), BF16)

    q = q_ref[0]
    lane = lax.broadcasted_iota(jnp.int32, (1, DA_HEAD_W), 1)
    map0 = (lane % 64) < 32
    zero = jnp.zeros_like(q)
    qs = jnp.concatenate([jnp.where(map0, q, zero), jnp.where(map0, zero, q)], axis=0)
    m_sc[...] = jnp.full(m_sc.shape, -jnp.inf, F32)
    acc_sc[...] = jnp.zeros(acc_sc.shape, F32)

    def scores(off, tk):
        k = k_ref[0, pl.ds(off, tk), :]
        return lax.dot_general(qs, k, (((1,), (1,)), ((), ())), preferred_element_type=F32)

    def accumulate(s, off, tk):
        m_prev = m_sc[...]
        m_new = jnp.maximum(m_prev, jnp.max(s, axis=1, keepdims=True))
        alpha = jnp.exp2(m_prev - m_new)
        p = jnp.exp2(s - jnp.tile(m_new, (1, tk // LANES)))
        pv = jnp.dot(p.astype(BF16), vext_sc[pl.ds(off, tk), :], preferred_element_type=F32)
        acc_sc[...] = jnp.tile(alpha, (1, 2)) * acc_sc[...] + pv
        m_sc[...] = m_new

    n_q_lat = n_lat // tq
    tk = s_sc.shape[2]
    n_steps = (n_lat + n_ctx) // tk

    @pl.when(i < n_q_lat)
    def _():
        s_sc[0] = scores(0, tk)
        for t in range(n_steps):
            if t + 1 < n_steps:
                s_sc[(t + 1) % 2] = scores((t + 1) * tk, tk)
            accumulate(s_sc[t % 2], t * tk, tk)

    @pl.when(i >= n_q_lat)
    def _():
        accumulate(scores(n_lat, n_ctx), n_lat, n_ctx)

    acc = acc_sc[...]
    o0 = acc[:tq, :DA_HEAD_W] / acc[:tq, DA_HEAD_W:DA_HEAD_W + 1]
    o1 = acc[tq:, :DA_HEAD_W] / acc[tq:, DA_HEAD_W:DA_HEAD_W + 1]
    o = o0 - lam_ref[0] * o1
    o = o * lax.rsqrt(jnp.mean(o * o, axis=-1, keepdims=True) + NORM_EPS)
    o_ref[0] = (o * sw_ref[...] * (1.0 - lam_init)).astype(BF16)


def _diff_attention(lam, q, k, v, subln_w, lam_init):
    B, LT, D = q.shape
    nb = LT // TM
    tk = max(t for t in range(TM, ATT_TK_MAX + 1, TM) if LT % t == 0)
    kern = functools.partial(_attn_kernel, n_lat=LT - TM, n_ctx=TM, lam_init=lam_init)
    grid_spec = pltpu.PrefetchScalarGridSpec(
        num_scalar_prefetch=1,
        grid=(B, DA_HEADS, nb),
        in_specs=[pl.BlockSpec((1, TM, DA_HEAD_W), lambda b, h, i, lam: (b, i, h)),
                  pl.BlockSpec((1, LT, DA_HEAD_W), lambda b, h, i, lam: (b, 0, h)),
                  pl.BlockSpec((1, LT, DA_HEAD_W), lambda b, h, i, lam: (b, 0, h)),
                  pl.BlockSpec((1, DA_HEAD_W), lambda b, h, i, lam: (0, 0))],
        out_specs=pl.BlockSpec((1, TM, DA_HEAD_W), lambda b, h, i, lam: (b, i, h)),
        scratch_shapes=[pltpu.VMEM((LT, 2 * DA_HEAD_W), BF16),
                        pltpu.VMEM((2 * TM, LANES), F32),
                        pltpu.VMEM((2 * TM, 2 * DA_HEAD_W), F32),
                        pltpu.VMEM((2, 2 * TM, tk), F32)],
    )
    return pl.pallas_call(
        kern,
        grid_spec=grid_spec,
        out_shape=jax.ShapeDtypeStruct((B, LT, D), BF16),
        compiler_params=_cparams(3),
        name="diff_attn",
    )(lam, q, k, v, subln_w.reshape(1, DA_HEAD_W))


def _route_block(logits, cnt_sc):
    lane = lax.broadcasted_iota(jnp.int32, logits.shape, 1)
    lane_f = lane.astype(F32)
    work = jnp.where(lane < N_EXPERTS, logits, -jnp.inf)
    tops, idxs, hits = [], [], []
    for _ in range(TOP_K):
        mk = jnp.max(work, axis=1, keepdims=True)
        ik = jnp.min(jnp.where(work == mk, lane_f, float(LANES)), axis=1, keepdims=True)
        hit = lane_f == ik
        tops.append(mk)
        idxs.append(ik)
        hits.append(hit)
        work = jnp.where(hit, -jnp.inf, work)
    chosen = functools.reduce(jnp.logical_or, hits).astype(F32)
    n = logits.shape[0]
    row = lax.broadcasted_iota(jnp.int32, (n, n), 0)
    col = lax.broadcasted_iota(jnp.int32, (n, n), 1)
    before = jnp.dot((col < row).astype(BF16), chosen.astype(BF16), preferred_element_type=F32)
    rank_all = cnt_sc[...] + before
    cnt_sc[...] = cnt_sc[...] + jnp.sum(chosen, axis=0, keepdims=True)
    exps = [jnp.exp(t - tops[0]) for t in tops]
    denom = functools.reduce(jnp.add, exps)
    table = jnp.zeros(logits.shape, F32)
    for k in range(TOP_K):
        rk = jnp.sum(jnp.where(hits[k], rank_all, 0.0), axis=1, keepdims=True)
        table = jnp.where(lane == k, idxs[k], table)
        table = jnp.where(lane == TOP_K + k, rk, table)
        table = jnp.where(lane == 2 * TOP_K + k, exps[k] / denom, table)
    return table


def _post_mixer(pre, w_ref, x_ref, mod_ref, ln_ref, rw_ref, rb_ref, x1_ref, u_ref, rt_ref, cnt_ref,
                cnt_sc):
    @pl.when((pl.program_id(0) == 0) & (pl.program_id(1) == 0))
    def _():
        cnt_sc[...] = jnp.zeros(cnt_sc.shape, F32)

    m = mod_ref[0]
    y = jnp.dot(pre, w_ref[...], preferred_element_type=F32)
    z = DEEPNORM_ALPHA * x_ref[0] + m[MOD_G1:MOD_G1 + 1] * y
    mu = jnp.mean(z, axis=-1, keepdims=True)
    zc = z - mu
    x1 = zc * lax.rsqrt(jnp.mean(zc * zc, axis=-1, keepdims=True) + NORM_EPS)
    x1 = x1 * ln_ref[0:1] + ln_ref[1:2]
    x1_ref[0] = x1
    u = x1 * m[MOD_SC2:MOD_SC2 + 1] + m[MOD_SH2:MOD_SH2 + 1]
    u_ref[0] = u
    u_hi = u.astype(BF16)
    u_lo = (u - u_hi.astype(F32)).astype(BF16)
    d_hi = jnp.dot(u_hi, rw_ref[...], preferred_element_type=F32)
    d_lo = jnp.dot(u_lo, rw_ref[:, :LANES], preferred_element_type=F32)
    logits = d_hi[:, :LANES] + d_hi[:, LANES:] + d_lo + rb_ref[...]
    rt_ref[0] = _route_block(logits, cnt_sc)
    cnt_ref[...] = jnp.broadcast_to(cnt_sc[...], cnt_ref.shape)


def _out_da_kernel(o_ref, w_ref, x_ref, mod_ref, ln_ref, rw_ref, rb_ref, x1_ref, u_ref, rt_ref,
                   cnt_ref, cnt_sc):
    _post_mixer(o_ref[0], w_ref, x_ref, mod_ref, ln_ref, rw_ref, rb_ref, x1_ref, u_ref, rt_ref,
                cnt_ref, cnt_sc)


def _out_gla_kernel(of_ref, ob_ref, r_ref, nw_ref, w_ref, x_ref, mod_ref, ln_ref, rw_ref, rb_ref,
                    x1_ref, u_ref, rt_ref, cnt_ref, cnt_sc):
    parts = []
    for h in range(GLA_HEADS):
        sl = slice(h * GLA_DV_HEAD, (h + 1) * GLA_DV_HEAD)
        o = of_ref[0, :, sl] + ob_ref[0, :, sl]
        o = o * lax.rsqrt(jnp.mean(o * o, axis=-1, keepdims=True) + NORM_EPS) * nw_ref[...]
        r = r_ref[0, :, sl]
        parts.append((o * (r * jax.nn.sigmoid(r))).astype(BF16))
    pre = jnp.concatenate(parts, axis=1)
    _post_mixer(pre, w_ref, x_ref, mod_ref, ln_ref, rw_ref, rb_ref, x1_ref, u_ref, rt_ref, cnt_ref,
                cnt_sc)


def _mixer_out(kind, acts, w_out, xa, mods, lnp, rw, rb, nb_out, norm_w=None):
    B, LT, D = xa.shape
    nbl = LT // TM - 1
    blk = pl.BlockSpec((1, TM, D), lambda b, i: (b, i, 0))
    common_specs = [pl.BlockSpec((D, D), lambda b, i: (0, 0)),
                    blk,
                    pl.BlockSpec((1, 8, D), lambda b, i: (2 * b + i // nbl, 0, 0)),
                    pl.BlockSpec((2, D), lambda b, i: (0, 0)),
                    pl.BlockSpec((D, 2 * LANES), lambda b, i: (0, 0)),
                    pl.BlockSpec((1, LANES), lambda b, i: (0, 0))]
    lout = nb_out * TM
    out_shape = [jax.ShapeDtypeStruct((B, lout, D), F32),
                 jax.ShapeDtypeStruct((B, lout, D), F32),
                 jax.ShapeDtypeStruct((B, lout, LANES), F32),
                 jax.ShapeDtypeStruct((8, LANES), F32)]
    out_specs = [blk, blk, pl.BlockSpec((1, TM, LANES), lambda b, i: (b, i, 0)),
                 pl.BlockSpec((8, LANES), lambda b, i: (0, 0))]
    if kind == "da":
        kern = _out_da_kernel
        in_specs = [blk] + common_specs
        args = list(acts)
    else:
        kern = _out_gla_kernel
        in_specs = [blk, blk, blk, pl.BlockSpec((1, GLA_DV_HEAD), lambda b, i: (0, 0))] + common_specs
        args = list(acts) + [norm_w.reshape(1, GLA_DV_HEAD)]
    return pl.pallas_call(
        kern,
        grid=(B, nb_out),
        in_specs=in_specs,
        out_specs=out_specs,
        out_shape=out_shape,
        scratch_shapes=[pltpu.VMEM((1, LANES), F32)],
        compiler_params=_cparams(2),
        name="mixer_out_" + kind,
    )(*args, w_out, xa, mods, lnp, rw, rb)


def _proj_gla_kernel(x_ref, mod_ref, w_ref, wz_ref, wg_ref, bg_ref,
                     q_ref, k_ref, v_ref, r_ref, g_ref):
    m = mod_ref[0]
    t = (x_ref[0] * m[MOD_SC1:MOD_SC1 + 1] + m[MOD_SH1:MOD_SH1 + 1]).astype(BF16)
    c0, c1, c2, c3 = GLA_DK, 2 * GLA_DK, 2 * GLA_DK + GLA_DV, 2 * GLA_DK + 2 * GLA_DV
    q_ref[0] = jnp.dot(t, w_ref[:, :c0], preferred_element_type=F32) * (GLA_DK_HEAD ** -0.5)
    k_ref[0] = jnp.dot(t, w_ref[:, c0:c1], preferred_element_type=F32)
    v_ref[0] = jnp.dot(t, w_ref[:, c1:c2], preferred_element_type=F32).astype(BF16)
    r_ref[0] = jnp.dot(t, w_ref[:, c2:c3], preferred_element_type=F32)
    z = jnp.dot(t, wz_ref[...], preferred_element_type=F32)
    gl = jnp.dot(z, wg_ref[...], preferred_element_type=F32) + bg_ref[...]
    log_sig = jnp.minimum(gl, 0.0) - jnp.log1p(jnp.exp(-jnp.abs(gl)))
    g_ref[0] = log_sig * (1.0 / GLA_TAU)


def _proj_gla(xa, mods, w_main, wz, wg, bg):
    B, LT, D = xa.shape
    nb = LT // TM
    nbl = nb - 1
    blk = lambda w: pl.BlockSpec((1, TM, w), lambda b, i: (b, i, 0))
    return pl.pallas_call(
        _proj_gla_kernel,
        grid=(B, nb),
        in_specs=[blk(D),
                  pl.BlockSpec((1, 8, D), lambda b, i: (2 * b + i // nbl, 0, 0)),
                  pl.BlockSpec(w_main.shape, lambda b, i: (0, 0)),
                  pl.BlockSpec(wz.shape, lambda b, i: (0, 0)),
                  pl.BlockSpec(wg.shape, lambda b, i: (0, 0)),
                  pl.BlockSpec(bg.shape, lambda b, i: (0, 0))],
        out_specs=[blk(GLA_DK), blk(GLA_DK), blk(GLA_DV), blk(GLA_DV), blk(2 * GLA_DK)],
        out_shape=[jax.ShapeDtypeStruct((B, LT, GLA_DK), F32),
                   jax.ShapeDtypeStruct((B, LT, GLA_DK), F32),
                   jax.ShapeDtypeStruct((B, LT, GLA_DV), BF16),
                   jax.ShapeDtypeStruct((B, LT, GLA_DV), F32),
                   jax.ShapeDtypeStruct((B, LT, 2 * GLA_DK), F32)],
        compiler_params=_cparams(2),
        name="gla_proj",
    )(xa, mods, w_main, wz, wg, bg)


def _gla_scan_kernel(q_ref, k_ref, v_ref, g_ref, o_ref, st_sc, *, reverse):
    j = pl.program_id(1)

    @pl.when(j == 0)
    def _():
        st_sc[...] = jnp.zeros(st_sc.shape, F32)

    C = GLA_CHUNK
    n_chunks = TM // C

    def causal(n):
        row = lax.broadcasted_iota(jnp.int32, (n, n), 0)
        col = lax.broadcasted_iota(jnp.int32, (n, n), 1)
        return (row // C == col // C) & ((col >= row) if reverse else (col <= row))

    keep = causal(C)
    tri = causal(TM).astype(BF16)
    g = g_ref[0]
    g_hi = g.astype(BF16)
    rem = g - g_hi.astype(F32)
    g_mid = rem.astype(BF16)
    g_lo = (rem - g_mid.astype(F32)).astype(BF16)
    b_all = (jnp.dot(tri, g_hi, preferred_element_type=F32)
             + jnp.dot(tri, g_mid, preferred_element_type=F32)
             + jnp.dot(tri, g_lo, preferred_element_type=F32))

    states = [st_sc[h] for h in range(GLA_HEADS)]
    order = range(n_chunks - 1, -1, -1) if reverse else range(n_chunks)
    for c in order:
        rows = slice(c * C, (c + 1) * C)
        for h in range(GLA_HEADS):
            ks = slice(h * GLA_DK_HEAD, (h + 1) * GLA_DK_HEAD)
            vs = slice(h * GLA_DV_HEAD, (h + 1) * GLA_DV_HEAD)
            b = b_all[rows, ks]
            tot = b[0:1] if reverse else b[C - 1:C]
            q = q_ref[0, rows, ks]
            k = k_ref[0, rows, ks]
            q_in = (q * jnp.exp(b)).astype(BF16)
            k_in = (k * jnp.exp(-b)).astype(BF16)
            k_st = (k * jnp.exp(tot - b)).astype(BF16)
            att = lax.dot_general(q_in, k_in, (((1,), (1,)), ((), ())), preferred_element_type=F32)
            att = jnp.where(keep, att, 0.0).astype(BF16)
            v = v_ref[0, rows, vs]
            st = states[h]
            o = jnp.dot(att, v, preferred_element_type=F32)
            o = o + lax.dot_general(q_in, st.astype(BF16), (((1,), (1,)), ((), ())),
                                    preferred_element_type=F32)
            o_ref[0, rows, vs] = o
            ds = lax.dot_general(v, k_st, (((0,), (0,)), ((), ())), preferred_element_type=F32)
            states[h] = st * jnp.exp(tot) + ds
    for h in range(GLA_HEADS):
        st_sc[h] = states[h]


def _gla_scan(q, k, v, g, reverse):
    B, LT, _ = q.shape
    nb = LT // TM
    ctx_blk = nb - 1
    if reverse:
        order = lambda j: jnp.where(j == 0, ctx_blk, ctx_blk - j)
    else:
        order = lambda j: jnp.where(j == 0, ctx_blk, j - 1)
    gcol = 1 if reverse else 0
    return pl.pallas_call(
        functools.partial(_gla_scan_kernel, reverse=reverse),
        grid=(B, nb),
        in_specs=[pl.BlockSpec((1, TM, GLA_DK), lambda b, j: (b, order(j), 0)),
                  pl.BlockSpec((1, TM, GLA_DK), lambda b, j: (b, order(j), 0)),
                  pl.BlockSpec((1, TM, GLA_DV), lambda b, j: (b, order(j), 0)),
                  pl.BlockSpec((1, TM, GLA_DK), lambda b, j: (b, order(j), gcol))],
        out_specs=pl.BlockSpec((1, TM, GLA_DV), lambda b, j: (b, order(j), 0)),
        out_shape=jax.ShapeDtypeStruct((B, LT, GLA_DV), F32),
        scratch_shapes=[pltpu.VMEM((GLA_HEADS, GLA_DV_HEAD, GLA_DK_HEAD), F32)],
        compiler_params=_cparams(2),
        name="gla_scan_bwd" if reverse else "gla_scan_fwd",
    )(q, k, v, g)


def _moe_kernel(pe_ref, nu_ref, tok_ref, tokn_ref, dst_ref, dstp_ref, u_hbm, wgu_ref, bgu_ref, wd_ref,
                bd_ref, yg_hbm, wgu_sc, wd_sc, x0, x1, y0, y1, gsem, ssem, *, dump_row):
    j = pl.program_id(0)
    n_used = nu_ref[0]
    H = MOE_BLOCK
    xbuf, ybuf = (x0, x1), (y0, y1)

    def gather(tab_ref, half, slot):
        for r in range(H):
            t = tab_ref[0, 0, half * H + r]
            pltpu.make_async_copy(u_hbm.at[pl.ds(t, 1), :], xbuf[slot].at[pl.ds(r, 1), :],
                                  gsem.at[slot]).start()

    def gather_wait(slot):
        pltpu.make_async_copy(u_hbm.at[pl.ds(0, H), :], xbuf[slot], gsem.at[slot]).wait()

    def scatter(row_of, slot):
        for r in range(H):
            pltpu.make_async_copy(ybuf[slot].at[pl.ds(r, 1), :], yg_hbm.at[pl.ds(row_of(r), 1), :],
                                  ssem.at[slot]).start()

    def scatter_wait(slot):
        pltpu.make_async_copy(ybuf[slot], yg_hbm.at[pl.ds(0, H), :], ssem.at[slot]).wait()

    def ffn(slot):
        gu = jnp.dot(xbuf[slot][...].astype(BF16), wgu_sc[...], preferred_element_type=F32)
        gu = gu + bgu_ref[0, 0]
        half = gu.shape[1] // 2
        glu = jnp.minimum(gu[:, :half], SWIGLU_LIMIT)
        lin = jnp.clip(gu[:, half:], -SWIGLU_LIMIT, SWIGLU_LIMIT)
        act = glu * jax.nn.sigmoid(SWIGLU_ALPHA * glu) * (lin + 1.0)
        ybuf[slot][...] = (jnp.dot(act.astype(BF16), wd_sc[...], preferred_element_type=F32)
                           + bd_ref[0, 0])

    @pl.when(j < n_used)
    def _():
        @pl.when(j == 0)
        def _():
            y0[...] = jnp.zeros(y0.shape, F32)
            y1[...] = jnp.zeros(y1.shape, F32)
            gather(tok_ref, 0, 0)
            scatter(lambda r: dump_row + r, 0)

        @pl.when((j == 0) | (pe_ref[j] != pe_ref[jnp.maximum(j - 1, 0)]))
        def _():
            wgu_sc[...] = wgu_ref[0, 0].astype(BF16)
            wd_sc[...] = wd_ref[0, 0].astype(BF16)

        gather_wait(0)
        scatter_wait(0)
        scatter(lambda r: dstp_ref[0, 0, H + r], 1)
        gather(tok_ref, 1, 1)
        ffn(0)
        gather_wait(1)
        scatter_wait(1)
        scatter(lambda r: dst_ref[0, 0, r], 0)
        gather(tokn_ref, 0, 0)
        ffn(1)

        @pl.when(j == n_used - 1)
        def _():
            gather_wait(0)
            scatter_wait(0)
            scatter(lambda r: dst_ref[0, 0, H + r], 1)
            scatter_wait(1)


def _moe_experts(layer, pair_expert, n_used, tok_tab, dst_tab, dst_prev_tab, u, w_gu, b_gu, w_down,
                 b_down, n_out_rows, dump_row):
    n_pairs = tok_tab.shape[0]
    D = u.shape[1]
    pair = 2 * MOE_BLOCK
    tab = lambda f: pl.BlockSpec((1, 1, pair), f, memory_space=pltpu.SMEM)
    grid_spec = pltpu.PrefetchScalarGridSpec(
        num_scalar_prefetch=2,
        grid=(n_pairs,),
        in_specs=[tab(lambda j, pe, nu: (j, 0, 0)),
                  tab(lambda j, pe, nu: (jnp.minimum(j + 1, n_pairs - 1), 0, 0)),
                  tab(lambda j, pe, nu: (j, 0, 0)),
                  tab(lambda j, pe, nu: (j, 0, 0)),
                  pl.BlockSpec(memory_space=pl.ANY),
                  pl.BlockSpec((1, 1, D, 2 * D), lambda j, pe, nu: (layer, pe[j], 0, 0)),
                  pl.BlockSpec((1, 1, 1, 2 * D), lambda j, pe, nu: (layer, pe[j], 0, 0)),
                  pl.BlockSpec((1, 1, D, D), lambda j, pe, nu: (layer, pe[j], 0, 0)),
                  pl.BlockSpec((1, 1, 1, D), lambda j, pe, nu: (layer, pe[j], 0, 0))],
        out_specs=pl.BlockSpec(memory_space=pl.ANY),
        scratch_shapes=[pltpu.VMEM((D, 2 * D), BF16), pltpu.VMEM((D, D), BF16),
                        pltpu.VMEM((MOE_BLOCK, D), F32), pltpu.VMEM((MOE_BLOCK, D), F32),
                        pltpu.VMEM((MOE_BLOCK, D), F32), pltpu.VMEM((MOE_BLOCK, D), F32),
                        pltpu.SemaphoreType.DMA((2,)), pltpu.SemaphoreType.DMA((2,))],
    )
    return pl.pallas_call(
        functools.partial(_moe_kernel, dump_row=dump_row),
        grid_spec=grid_spec,
        out_shape=jax.ShapeDtypeStruct((n_out_rows, D), F32),
        compiler_params=pltpu.CompilerParams(dimension_semantics=("arbitrary",),
                                             vmem_limit_bytes=MOE_VMEM_LIMIT),
        name="moe_experts",
    )(pair_expert, n_used, tok_tab, tok_tab, dst_tab, dst_prev_tab, u, w_gu,
      b_gu.reshape(DEPTH, N_EXPERTS, 1, 2 * D), w_down, b_down.reshape(DEPTH, N_EXPERTS, 1, D))


def _moe_ffn(layer, u, route, counts, w_gu, b_gu, w_down, b_down):
    T, D = u.shape
    pair = 2 * MOE_BLOCK
    n_assign = T * TOP_K
    expert = route[:, :TOP_K].astype(jnp.int32)
    rank = route[:, TOP_K:2 * TOP_K].astype(jnp.int32)
    cnt = counts[0, :N_EXPERTS].astype(jnp.int32)
    padded = (cnt + pair - 1) // pair * pair
    padded_end = jnp.cumsum(padded)
    padded_start = padded_end - padded
    pos = (padded_start[expert] + rank).T
    n_pairs = -(-n_assign // pair) + N_EXPERTS
    n_rows = n_pairs * pair
    pair_start = jnp.arange(n_pairs, dtype=jnp.int32) * pair
    pair_expert = jnp.minimum(jnp.sum(padded_end[None, :] <= pair_start[:, None], axis=1),
                              N_EXPERTS - 1).astype(jnp.int32)
    n_used = (padded_end[-1:] // pair).astype(jnp.int32)
    _, sorted_a = lax.sort((pos.reshape(-1), jnp.arange(n_assign, dtype=jnp.int32)), num_keys=1)
    row = jnp.arange(n_rows, dtype=jnp.int32)
    row_e = jnp.repeat(pair_expert, pair)
    row_rank = row - padded_start[row_e]
    compact = jnp.cumsum(cnt)[row_e] - cnt[row_e] + row_rank
    valid = row_rank < cnt[row_e]
    assign = sorted_a[jnp.minimum(compact, n_assign - 1)]
    tok_tab = jnp.where(valid, assign % T, 0).reshape(n_pairs, 1, pair)
    dst_tab = jnp.where(valid, assign, n_assign + row % pair).reshape(n_pairs, 1, pair)
    dump = (n_assign + jnp.arange(pair, dtype=jnp.int32)).reshape(1, 1, pair)
    dst_prev_tab = jnp.concatenate([dump, dst_tab[:-1]], axis=0)
    return _moe_experts(layer, pair_expert, n_used, tok_tab, dst_tab, dst_prev_tab, u, w_gu, b_gu,
                        w_down, b_down, n_assign + pair, n_assign)


def _final_ln_kernel(x_ref, *refs):
    y_refs = refs[:TOP_K]
    rt_ref, mod_ref, ln_ref, o_ref = refs[TOP_K:]
    m = mod_ref[0]
    rt = rt_ref[0]
    f = rt[:, 2 * TOP_K:2 * TOP_K + 1] * y_refs[0][...]
    for k in range(1, TOP_K):
        f = f + rt[:, 2 * TOP_K + k:2 * TOP_K + k + 1] * y_refs[k][...]
    z = DEEPNORM_ALPHA * x_ref[0] + m[MOD_G2:MOD_G2 + 1] * f
    mu = jnp.mean(z, axis=-1, keepdims=True)
    zc = z - mu
    y = zc * lax.rsqrt(jnp.mean(zc * zc, axis=-1, keepdims=True) + NORM_EPS)
    o_ref[0] = y * ln_ref[0:1] + ln_ref[1:2]


def _final_ln(x1, yg, route, mods, lnp, nbl):
    B, L, D = x1.shape
    nblk = L // TM
    blk = pl.BlockSpec((1, TM, D), lambda b, i: (b, i, 0))
    choice = lambda k: pl.BlockSpec((TM, D), lambda b, i: (k * B * nblk + b * nblk + i, 0))
    return pl.pallas_call(
        _final_ln_kernel,
        grid=(B, nblk),
        in_specs=[blk] + [choice(k) for k in range(TOP_K)] + [
                  pl.BlockSpec((1, TM, LANES), lambda b, i: (b, i, 0)),
                  pl.BlockSpec((1, 8, D), lambda b, i: (2 * b + i // nbl, 0, 0)),
                  pl.BlockSpec((2, D), lambda b, i: (0, 0))],
        out_specs=blk,
        out_shape=jax.ShapeDtypeStruct((B, L, D), F32),
        compiler_params=_cparams(2),
        name="final_ln",
    )(x1, *([yg] * TOP_K), route, mods, lnp)


def _rope_tables(S, n_ctx):
    rows = S // GRID_W
    row = jnp.repeat(jnp.arange(rows), GRID_W).astype(F32)
    col = jnp.tile(jnp.arange(GRID_W), rows).astype(F32)
    inv = ROPE_BASE ** (-jnp.arange(ROPE_PAIRS_AXIS, dtype=F32) / ROPE_PAIRS_AXIS)
    ang = jnp.concatenate([row[:, None] * inv, col[:, None] * inv], -1)
    cos, sin = jnp.cos(ang), jnp.sin(ang)
    cos = jnp.concatenate([cos, jnp.ones((n_ctx, cos.shape[1]), F32)], axis=0)
    sin = jnp.concatenate([sin, jnp.zeros((n_ctx, sin.shape[1]), F32)], axis=0)
    return (jnp.concatenate([cos, cos, cos, cos], axis=1),
            jnp.concatenate([-sin, -sin, sin, sin], axis=1))


def _qk_column_perm():
    lane = np.arange(DA_HEAD_W)
    half, mp, jj = lane // 64, (lane % 64) // 32, lane % 32
    src = mp * DA_HEAD_DIM + half * 32 + jj
    head = np.arange(DA_HEADS)[:, None] * DA_HEAD_W
    perm = (head + src[None, :]).reshape(-1)
    return np.concatenate([perm, D_MODEL + perm])


def _split_router_w(w):
    w = jnp.pad(w, ((0, 0), (0, LANES - N_EXPERTS)))
    hi = w.astype(BF16)
    lo = (w - hi.astype(F32)).astype(BF16)
    return jnp.concatenate([hi, lo], axis=1)


def _layer_mods(mod_rows, B):
    D = D_MODEL
    parts = mod_rows.reshape(8, 6, D)
    sh1, sc1, g1, sh2, sc2, g2 = (parts[:, n] for n in range(6))
    tab = jnp.stack([1.0 + sc1, sh1, g1, 1.0 + sc2, sh2, g2, jnp.zeros_like(g1),
                     jnp.zeros_like(g1)], axis=1)
    lat = tab[:B]
    ctx = jnp.broadcast_to(tab[B:B + 1], (B, 8, D))
    return jnp.stack([lat, ctx], axis=1).reshape(2 * B, 8, D)


def kernel(x, c, ctx, c_ctx, ada_w, ada_b, ln_g, ln_b, da_w_in, da_w_out, da_lambda, da_subln_w,
           gla_w_in, gla_w_gate, gla_b_gate, gla_norm_w, gla_w_out, router_w, router_b,
           moe_w_gu, moe_b_gu, moe_w_down, moe_b_down):
    B, S, D = x.shape
    n_ctx = ctx.shape[1]
    assert D == D_MODEL and n_ctx == TM and S % TM == 0 and S % GRID_W == 0 and B + 1 <= 8
    nbl = S // TM
    nb = nbl + 1

    cc = jnp.concatenate([c, c_ctx[None, :], jnp.zeros((8 - B - 1, D), F32)], axis=0)
    mod_all = _ada_mods(cc, ada_w, ada_b)
    xa = jnp.concatenate([x, ctx], axis=1)

    rw = [_split_router_w(router_w[i]) for i in range(DEPTH)]
    rb = [jnp.pad(router_b[i], (0, LANES - N_EXPERTS)).reshape(1, LANES) for i in range(DEPTH)]
    lnp = [[jnp.stack([ln_g[i, n], ln_b[i, n]]) for n in range(2)] for i in range(DEPTH)]

    mods = _layer_mods(mod_all[0], B)
    w_in = da_w_in[0]
    wqk = w_in[:, _qk_column_perm()].astype(BF16)
    wv = w_in[:, 2 * D:].astype(BF16)
    cos, sin = _rope_tables(S, n_ctx)
    q, k, v = _proj_da(xa, mods, cos, sin, wqk, wv)
    lam_init = _lambda_init(0)
    lv = da_lambda[0].astype(F32)
    lam = (jnp.exp(jnp.sum(lv[0] * lv[1])) - jnp.exp(jnp.sum(lv[2] * lv[3])) + lam_init).reshape(1)
    o = _diff_attention(lam, q, k, v, da_subln_w[0], lam_init)
    x1, u, route, counts = _mixer_out("da", [o], da_w_out[0].astype(BF16), xa, mods, lnp[0][0],
                                      rw[0], rb[0], nb)
    yg = _moe_ffn(0, u.reshape(B * nb * TM, D), route.reshape(B * nb * TM, LANES), counts,
                  moe_w_gu, moe_b_gu, moe_w_down, moe_b_down)
    xa = _final_ln(x1, yg, route, mods, lnp[0][1], nbl)

    mods = _layer_mods(mod_all[1], B)
    gw = gla_w_in[0]
    c3 = 2 * GLA_DK + 2 * GLA_DV
    w_main = gw[:, :c3].astype(BF16)
    wz = jnp.pad(gw[:, c3:], ((0, 0), (0, LANES - 2 * GLA_GATE_RANK))).astype(BF16)
    wg = jnp.zeros((LANES, 2 * GLA_DK), F32)
    wg = wg.at[:GLA_GATE_RANK, :GLA_DK].set(gla_w_gate[0, 0])
    wg = wg.at[GLA_GATE_RANK:2 * GLA_GATE_RANK, GLA_DK:].set(gla_w_gate[0, 1])
    bg = gla_b_gate[0].reshape(1, 2 * GLA_DK)
    gq, gk, gv, gr, gg = _proj_gla(xa, mods, w_main, wz, wg, bg)
    of = _gla_scan(gq, gk, gv, gg, reverse=False)
    ob = _gla_scan(gq, gk, gv, gg, reverse=True)
    x1, u, route, counts = _mixer_out("gla", [of, ob, gr], gla_w_out[0].astype(BF16), xa, mods,
                                      lnp[1][0], rw[1], rb[1], nbl, norm_w=gla_norm_w[0])
    yg = _moe_ffn(1, u.reshape(B * S, D), route.reshape(B * S, LANES), counts,
                  moe_w_gu, moe_b_gu, moe_w_down, moe_b_down)
    return _final_ln(x1, yg, route, mods, lnp[1][1], nbl)
```

```python
import functools
import math

import numpy as np
import jax
import jax.numpy as jnp
from jax import lax
from jax.experimental import pallas as pl
from jax.experimental.pallas import tpu as pltpu

F32 = jnp.float32
BF16 = jnp.bfloat16

D_MODEL = 1024
DEPTH = 2
GRID_W = 64

DA_HEADS = 8
DA_HEAD_DIM = 64
DA_HEAD_W = 2 * DA_HEAD_DIM
ROPE_BASE = 10000.0
ROPE_PAIRS_AXIS = DA_HEAD_DIM // 4

GLA_HEADS = 4
GLA_DK = D_MODEL // 2
GLA_DV = D_MODEL
GLA_DK_HEAD = GLA_DK // GLA_HEADS
GLA_DV_HEAD = GLA_DV // GLA_HEADS
GLA_GATE_RANK = 16
GLA_TAU = 16.0
GLA_CHUNK = 64

N_EXPERTS = 32
TOP_K = 4
SWIGLU_ALPHA = 1.702
SWIGLU_LIMIT = 7.0
MOE_BLOCK = 256
ROW_SLABS = D_MODEL // 128

DEEPNORM_ALPHA = (2.0 * DEPTH) ** 0.25
NORM_EPS = 1e-5

LANES = 128
TM = 256
ATT_TK_MAX = 2816
VMEM_LIMIT = 48 * 1024 * 1024
MOE_VMEM_LIMIT = 56 * 1024 * 1024

MOD_SC1, MOD_SH1, MOD_G1, MOD_SC2, MOD_SH2, MOD_G2 = range(6)


def _cparams(n_axes):
    return pltpu.CompilerParams(dimension_semantics=("arbitrary",) * n_axes,
                                vmem_limit_bytes=VMEM_LIMIT)


def _lambda_init(layer_idx):
    return 0.8 - 0.6 * math.exp(-0.3 * layer_idx)


def _ada_kernel(c_ref, w_ref, b_ref, o_ref):
    c = c_ref[...]
    s = c * jax.nn.sigmoid(c)
    o_ref[0] = jnp.dot(s, w_ref[0], preferred_element_type=F32) + b_ref[0]


def _ada_mods(cc, ada_w, ada_b):
    nt = 1536
    n6 = 6 * D_MODEL
    return pl.pallas_call(
        _ada_kernel,
        grid=(DEPTH, n6 // nt),
        in_specs=[pl.BlockSpec((8, D_MODEL), lambda l, j: (0, 0)),
                  pl.BlockSpec((1, D_MODEL, nt), lambda l, j: (l, 0, j)),
                  pl.BlockSpec((1, 1, nt), lambda l, j: (l, 0, j))],
        out_specs=pl.BlockSpec((1, 8, nt), lambda l, j: (l, 0, j)),
        out_shape=jax.ShapeDtypeStruct((DEPTH, 8, n6), F32),
        compiler_params=_cparams(2),
        name="ada_mods",
    )(cc, ada_w, ada_b.reshape(DEPTH, 1, n6))


def _proj_da_kernel(x_ref, mod_ref, cos_ref, sin_ref, wqk_ref, wv_ref, q_ref, k_ref, v_ref):
    m = mod_ref[0]
    t = (x_ref[0] * m[MOD_SC1:MOD_SC1 + 1] + m[MOD_SH1:MOD_SH1 + 1]).astype(BF16)
    cos = cos_ref[...]
    sin = sin_ref[...]
    q_scale = DA_HEAD_DIM ** -0.5 * math.log2(math.e)
    for j in range(DA_HEADS):
        y2 = jnp.dot(t, wqk_ref[:, j * 256:(j + 1) * 256], preferred_element_type=F32)
        for hh in range(2):
            y = y2[:, hh * LANES:(hh + 1) * LANES]
            y = y * cos + pltpu.roll(y, 64, 1) * sin
            col = (2 * j + hh) * LANES
            if col < D_MODEL:
                q_ref[0, :, col:col + LANES] = (y * q_scale).astype(BF16)
            else:
                k_ref[0, :, col - D_MODEL:col - D_MODEL + LANES] = y.astype(BF16)
    v_ref[0] = jnp.dot(t, wv_ref[...], preferred_element_type=F32).astype(BF16)


def _proj_da(xa, mods, cos, sin, wqk, wv):
    B, LT, D = xa.shape
    nb = LT // TM
    nbl = nb - 1
    out = jax.ShapeDtypeStruct((B, LT, D), BF16)
    blk = pl.BlockSpec((1, TM, D), lambda b, i: (b, i, 0))
    return pl.pallas_call(
        _proj_da_kernel,
        grid=(B, nb),
        in_specs=[blk,
                  pl.BlockSpec((1, 8, D), lambda b, i: (2 * b + i // nbl, 0, 0)),
                  pl.BlockSpec((TM, LANES), lambda b, i: (i, 0)),
                  pl.BlockSpec((TM, LANES), lambda b, i: (i, 0)),
                  pl.BlockSpec((D, 2 * D), lambda b, i: (0, 0)),
                  pl.BlockSpec((D, D), lambda b, i: (0, 0))],
        out_specs=[blk, blk, blk],
        out_shape=[out, out, out],
        compiler_params=_cparams(2),
        name="da_proj",
    )(xa, mods, cos, sin, wqk, wv)


def _attn_kernel(lam_ref, q_ref, k_ref, v_ref, sw_ref, o_ref, vext_sc, m_sc, acc_sc, s_sc, *,
                 n_lat, n_ctx, lam_init):
    i = pl.program_id(2)
    tq = q_ref.shape[1]

    @pl.when(i == 0)
    def _():
        vext_sc[:, :DA_HEAD_W] = v_ref[0]
        vext_sc[:, DA_HEAD_W:] = jnp.ones((vext_sc.shape[0], DA_HEAD_W), BF16)

    q = q_ref[0]
    lane = lax.broadcasted_iota(jnp.int32, (1, DA_HEAD_W), 1)
    map0 = (lane % 64) < 32
    zero = jnp.zeros_like(q)
    qs = jnp.concatenate([jnp.where(map0, q, zero), jnp.where(map0, zero, q)], axis=0)
    m_sc[...] = jnp.full(m_sc.shape, -jnp.inf, F32)
    acc_sc[...] = jnp.zeros(acc_sc.shape, F32)

    def scores(off, tk):
        k = k_ref[0, pl.ds(off, tk), :]
        return lax.dot_general(qs, k, (((1,), (1,)), ((), ())), preferred_element_type=F32)

    def accumulate(s, off, tk):
        m_prev = m_sc[...]
        m_new = jnp.maximum(m_prev, jnp.max(s, axis=1, keepdims=True))
        alpha = jnp.exp2(m_prev - m_new)
        p = jnp.exp2(s - jnp.tile(m_new, (1, tk // LANES)))
        pv = jnp.dot(p.astype(BF16), vext_sc[pl.ds(off, tk), :], preferred_element_type=F32)
        acc_sc[...] = jnp.tile(alpha, (1, 2)) * acc_sc[...] + pv
        m_sc[...] = m_new

    n_q_lat = n_lat // tq
    tk = s_sc.shape[2]
    n_steps = (n_lat + n_ctx) // tk

    @pl.when(i < n_q_lat)
    def _():
        s_sc[0] = scores(0, tk)
        for t in range(n_steps):
            if t + 1 < n_steps:
                s_sc[(t + 1) % 2] = scores((t + 1) * tk, tk)
            accumulate(s_sc[t % 2], t * tk, tk)

    @pl.when(i >= n_q_lat)
    def _():
        accumulate(scores(n_lat, n_ctx), n_lat, n_ctx)

    acc = acc_sc[...]
    o0 = acc[:tq, :DA_HEAD_W] / acc[:tq, DA_HEAD_W:DA_HEAD_W + 1]
    o1 = acc[tq:, :DA_HEAD_W] / acc[tq:, DA_HEAD_W:DA_HEAD_W + 1]
    o = o0 - lam_ref[0] * o1
    o = o * lax.rsqrt(jnp.mean(o * o, axis=-1, keepdims=True) + NORM_EPS)
    o_ref[0] = (o * sw_ref[...] * (1.0 - lam_init)).astype(BF16)


def _diff_attention(lam, q, k, v, subln_w, lam_init):
    B, LT, D = q.shape
    nb = LT // TM
    tk = max(t for t in range(TM, ATT_TK_MAX + 1, TM) if LT % t == 0)
    kern = functools.partial(_attn_kernel, n_lat=LT - TM, n_ctx=TM, lam_init=lam_init)
    grid_spec = pltpu.PrefetchScalarGridSpec(
        num_scalar_prefetch=1,
        grid=(B, DA_HEADS, nb),
        in_specs=[pl.BlockSpec((1, TM, DA_HEAD_W), lambda b, h, i, lam: (b, i, h)),
                  pl.BlockSpec((1, LT, DA_HEAD_W), lambda b, h, i, lam: (b, 0, h)),
                  pl.BlockSpec((1, LT, DA_HEAD_W), lambda b, h, i, lam: (b, 0, h)),
                  pl.BlockSpec((1, DA_HEAD_W), lambda b, h, i, lam: (0, 0))],
        out_specs=pl.BlockSpec((1, TM, DA_HEAD_W), lambda b, h, i, lam: (b, i, h)),
        scratch_shapes=[pltpu.VMEM((LT, 2 * DA_HEAD_W), BF16),
                        pltpu.VMEM((2 * TM, LANES), F32),
                        pltpu.VMEM((2 * TM, 2 * DA_HEAD_W), F32),
                        pltpu.VMEM((2, 2 * TM, tk), F32)],
    )
    return pl.pallas_call(
        kern,
        grid_spec=grid_spec,
        out_shape=jax.ShapeDtypeStruct((B, LT, D), BF16),
        compiler_params=_cparams(3),
        name="diff_attn",
    )(lam, q, k, v, subln_w.reshape(1, DA_HEAD_W))


def _route_block(logits, cnt_sc):
    lane = lax.broadcasted_iota(jnp.int32, logits.shape, 1)
    lane_f = lane.astype(F32)
    work = jnp.where(lane < N_EXPERTS, logits, -jnp.inf)
    tops, idxs, hits = [], [], []
    for _ in range(TOP_K):
        mk = jnp.max(work, axis=1, keepdims=True)
        ik = jnp.min(jnp.where(work == mk, lane_f, float(LANES)), axis=1, keepdims=True)
        hit = lane_f == ik
        tops.append(mk)
        idxs.append(ik)
        hits.append(hit)
        work = jnp.where(hit, -jnp.inf, work)
    chosen = functools.reduce(jnp.logical_or, hits).astype(F32)
    n = logits.shape[0]
    row = lax.broadcasted_iota(jnp.int32, (n, n), 0)
    col = lax.broadcasted_iota(jnp.int32, (n, n), 1)
    before = jnp.dot((col < row).astype(BF16), chosen.astype(BF16), preferred_element_type=F32)
    rank_all = cnt_sc[...] + before
    cnt_sc[...] = cnt_sc[...] + jnp.sum(chosen, axis=0, keepdims=True)
    exps = [jnp.exp(t - tops[0]) for t in tops]
    denom = functools.reduce(jnp.add, exps)
    table = jnp.zeros(logits.shape, F32)
    for k in range(TOP_K):
        rk = jnp.sum(jnp.where(hits[k], rank_all, 0.0), axis=1, keepdims=True)
        table = jnp.where(lane == k, idxs[k], table)
        table = jnp.where(lane == TOP_K + k, rk, table)
        table = jnp.where(lane == 2 * TOP_K + k, exps[k] / denom, table)
    return table


def _post_mixer(pre, w_ref, x_ref, mod_ref, ln_ref, rw_ref, rb_ref, x1_ref, u_ref, rt_ref, cnt_ref,
                cnt_sc):
    @pl.when((pl.program_id(0) == 0) & (pl.program_id(1) == 0))
    def _():
        cnt_sc[...] = jnp.zeros(cnt_sc.shape, F32)

    m = mod_ref[0]
    y = jnp.dot(pre, w_ref[...], preferred_element_type=F32)
    z = DEEPNORM_ALPHA * x_ref[0] + m[MOD_G1:MOD_G1 + 1] * y
    mu = jnp.mean(z, axis=-1, keepdims=True)
    zc = z - mu
    x1 = zc * lax.rsqrt(jnp.mean(zc * zc, axis=-1, keepdims=True) + NORM_EPS)
    x1 = x1 * ln_ref[0:1] + ln_ref[1:2]
    x1_ref[0] = x1
    u = x1 * m[MOD_SC2:MOD_SC2 + 1] + m[MOD_SH2:MOD_SH2 + 1]
    for sl in range(ROW_SLABS):
        u_ref[0, :, sl, :] = u[:, sl * LANES:(sl + 1) * LANES]
    u_hi = u.astype(BF16)
    u_lo = (u - u_hi.astype(F32)).astype(BF16)
    d_hi = jnp.dot(u_hi, rw_ref[...], preferred_element_type=F32)
    d_lo = jnp.dot(u_lo, rw_ref[:, :LANES], preferred_element_type=F32)
    logits = d_hi[:, :LANES] + d_hi[:, LANES:] + d_lo + rb_ref[...]
    rt_ref[0] = _route_block(logits, cnt_sc)
    cnt_ref[...] = jnp.broadcast_to(cnt_sc[...], cnt_ref.shape)


def _out_da_kernel(o_ref, w_ref, x_ref, mod_ref, ln_ref, rw_ref, rb_ref, x1_ref, u_ref, rt_ref,
                   cnt_ref, cnt_sc):
    _post_mixer(o_ref[0], w_ref, x_ref, mod_ref, ln_ref, rw_ref, rb_ref, x1_ref, u_ref, rt_ref,
                cnt_ref, cnt_sc)


def _out_gla_kernel(of_ref, ob_ref, r_ref, nw_ref, w_ref, x_ref, mod_ref, ln_ref, rw_ref, rb_ref,
                    x1_ref, u_ref, rt_ref, cnt_ref, cnt_sc):
    parts = []
    for h in range(GLA_HEADS):
        sl = slice(h * GLA_DV_HEAD, (h + 1) * GLA_DV_HEAD)
        o = of_ref[0, :, sl] + ob_ref[0, :, sl]
        o = o * lax.rsqrt(jnp.mean(o * o, axis=-1, keepdims=True) + NORM_EPS) * nw_ref[...]
        r = r_ref[0, :, sl]
        parts.append((o * (r * jax.nn.sigmoid(r))).astype(BF16))
    pre = jnp.concatenate(parts, axis=1)
    _post_mixer(pre, w_ref, x_ref, mod_ref, ln_ref, rw_ref, rb_ref, x1_ref, u_ref, rt_ref, cnt_ref,
                cnt_sc)


def _mixer_out(kind, acts, w_out, xa, mods, lnp, rw, rb, nb_out, norm_w=None):
    B, LT, D = xa.shape
    nbl = LT // TM - 1
    blk = pl.BlockSpec((1, TM, D), lambda b, i: (b, i, 0))
    common_specs = [pl.BlockSpec((D, D), lambda b, i: (0, 0)),
                    blk,
                    pl.BlockSpec((1, 8, D), lambda b, i: (2 * b + i // nbl, 0, 0)),
                    pl.BlockSpec((2, D), lambda b, i: (0, 0)),
                    pl.BlockSpec((D, 2 * LANES), lambda b, i: (0, 0)),
                    pl.BlockSpec((1, LANES), lambda b, i: (0, 0))]
    lout = nb_out * TM
    out_shape = [jax.ShapeDtypeStruct((B, lout, D), F32),
                 jax.ShapeDtypeStruct((B, lout, ROW_SLABS, LANES), F32),
                 jax.ShapeDtypeStruct((B, lout, LANES), F32),
                 jax.ShapeDtypeStruct((8, LANES), F32)]
    out_specs = [blk, pl.BlockSpec((1, TM, ROW_SLABS, LANES), lambda b, i: (b, i, 0, 0)),
                 pl.BlockSpec((1, TM, LANES), lambda b, i: (b, i, 0)),
                 pl.BlockSpec((8, LANES), lambda b, i: (0, 0))]
    if kind == "da":
        kern = _out_da_kernel
        in_specs = [blk] + common_specs
        args = list(acts)
    else:
        kern = _out_gla_kernel
        in_specs = [blk, blk, blk, pl.BlockSpec((1, GLA_DV_HEAD), lambda b, i: (0, 0))] + common_specs
        args = list(acts) + [norm_w.reshape(1, GLA_DV_HEAD)]
    return pl.pallas_call(
        kern,
        grid=(B, nb_out),
        in_specs=in_specs,
        out_specs=out_specs,
        out_shape=out_shape,
        scratch_shapes=[pltpu.VMEM((1, LANES), F32)],
        compiler_params=_cparams(2),
        name="mixer_out_" + kind,
    )(*args, w_out, xa, mods, lnp, rw, rb)


def _proj_gla_kernel(x_ref, mod_ref, w_ref, wz_ref, wg_ref, bg_ref,
                     q_ref, k_ref, v_ref, r_ref, g_ref):
    m = mod_ref[0]
    t = (x_ref[0] * m[MOD_SC1:MOD_SC1 + 1] + m[MOD_SH1:MOD_SH1 + 1]).astype(BF16)
    c0, c1, c2, c3 = GLA_DK, 2 * GLA_DK, 2 * GLA_DK + GLA_DV, 2 * GLA_DK + 2 * GLA_DV
    q_ref[0] = jnp.dot(t, w_ref[:, :c0], preferred_element_type=F32) * (GLA_DK_HEAD ** -0.5)
    k_ref[0] = jnp.dot(t, w_ref[:, c0:c1], preferred_element_type=F32)
    v_ref[0] = jnp.dot(t, w_ref[:, c1:c2], preferred_element_type=F32).astype(BF16)
    r_ref[0] = jnp.dot(t, w_ref[:, c2:c3], preferred_element_type=F32)
    z = jnp.dot(t, wz_ref[...], preferred_element_type=F32)
    gl = jnp.dot(z, wg_ref[...], preferred_element_type=F32) + bg_ref[...]
    log_sig = jnp.minimum(gl, 0.0) - jnp.log1p(jnp.exp(-jnp.abs(gl)))
    g_ref[0] = log_sig * (1.0 / GLA_TAU)


def _proj_gla(xa, mods, w_main, wz, wg, bg):
    B, LT, D = xa.shape
    nb = LT // TM
    nbl = nb - 1
    blk = lambda w: pl.BlockSpec((1, TM, w), lambda b, i: (b, i, 0))
    return pl.pallas_call(
        _proj_gla_kernel,
        grid=(B, nb),
        in_specs=[blk(D),
                  pl.BlockSpec((1, 8, D), lambda b, i: (2 * b + i // nbl, 0, 0)),
                  pl.BlockSpec(w_main.shape, lambda b, i: (0, 0)),
                  pl.BlockSpec(wz.shape, lambda b, i: (0, 0)),
                  pl.BlockSpec(wg.shape, lambda b, i: (0, 0)),
                  pl.BlockSpec(bg.shape, lambda b, i: (0, 0))],
        out_specs=[blk(GLA_DK), blk(GLA_DK), blk(GLA_DV), blk(GLA_DV), blk(2 * GLA_DK)],
        out_shape=[jax.ShapeDtypeStruct((B, LT, GLA_DK), F32),
                   jax.ShapeDtypeStruct((B, LT, GLA_DK), F32),
                   jax.ShapeDtypeStruct((B, LT, GLA_DV), BF16),
                   jax.ShapeDtypeStruct((B, LT, GLA_DV), F32),
                   jax.ShapeDtypeStruct((B, LT, 2 * GLA_DK), F32)],
        compiler_params=_cparams(2),
        name="gla_proj",
    )(xa, mods, w_main, wz, wg, bg)


def _gla_scan_kernel(q_ref, k_ref, v_ref, g_ref, o_ref, st_sc, *, reverse):
    j = pl.program_id(1)

    @pl.when(j == 0)
    def _():
        st_sc[...] = jnp.zeros(st_sc.shape, F32)

    C = GLA_CHUNK
    n_chunks = TM // C

    def causal(n):
        row = lax.broadcasted_iota(jnp.int32, (n, n), 0)
        col = lax.broadcasted_iota(jnp.int32, (n, n), 1)
        return (row // C == col // C) & ((col >= row) if reverse else (col <= row))

    keep = causal(C)
    tri = causal(TM).astype(BF16)
    g = g_ref[0]
    g_hi = g.astype(BF16)
    rem = g - g_hi.astype(F32)
    g_mid = rem.astype(BF16)
    g_lo = (rem - g_mid.astype(F32)).astype(BF16)
    b_all = (jnp.dot(tri, g_hi, preferred_element_type=F32)
             + jnp.dot(tri, g_mid, preferred_element_type=F32)
             + jnp.dot(tri, g_lo, preferred_element_type=F32))

    states = [st_sc[h] for h in range(GLA_HEADS)]
    order = range(n_chunks - 1, -1, -1) if reverse else range(n_chunks)
    for c in order:
        rows = slice(c * C, (c + 1) * C)
        for h in range(GLA_HEADS):
            ks = slice(h * GLA_DK_HEAD, (h + 1) * GLA_DK_HEAD)
            vs = slice(h * GLA_DV_HEAD, (h + 1) * GLA_DV_HEAD)
            b = b_all[rows, ks]
            tot = b[0:1] if reverse else b[C - 1:C]
            q = q_ref[0, rows, ks]
            k = k_ref[0, rows, ks]
            q_in = (q * jnp.exp(b)).astype(BF16)
            k_in = (k * jnp.exp(-b)).astype(BF16)
            k_st = (k * jnp.exp(tot - b)).astype(BF16)
            att = lax.dot_general(q_in, k_in, (((1,), (1,)), ((), ())), preferred_element_type=F32)
            att = jnp.where(keep, att, 0.0).astype(BF16)
            v = v_ref[0, rows, vs]
            st = states[h]
            o = jnp.dot(att, v, preferred_element_type=F32)
            o = o + lax.dot_general(q_in, st.astype(BF16), (((1,), (1,)), ((), ())),
                                    preferred_element_type=F32)
            o_ref[0, rows, vs] = o
            ds = lax.dot_general(v, k_st, (((0,), (0,)), ((), ())), preferred_element_type=F32)
            states[h] = st * jnp.exp(tot) + ds
    for h in range(GLA_HEADS):
        st_sc[h] = states[h]


def _gla_scan(q, k, v, g, reverse):
    B, LT, _ = q.shape
    nb = LT // TM
    ctx_blk = nb - 1
    if reverse:
        order = lambda j: jnp.where(j == 0, ctx_blk, ctx_blk - j)
    else:
        order = lambda j: jnp.where(j == 0, ctx_blk, j - 1)
    gcol = 1 if reverse else 0
    return pl.pallas_call(
        functools.partial(_gla_scan_kernel, reverse=reverse),
        grid=(B, nb),
        in_specs=[pl.BlockSpec((1, TM, GLA_DK), lambda b, j: (b, order(j), 0)),
                  pl.BlockSpec((1, TM, GLA_DK), lambda b, j: (b, order(j), 0)),
                  pl.BlockSpec((1, TM, GLA_DV), lambda b, j: (b, order(j), 0)),
                  pl.BlockSpec((1, TM, GLA_DK), lambda b, j: (b, order(j), gcol))],
        out_specs=pl.BlockSpec((1, TM, GLA_DV), lambda b, j: (b, order(j), 0)),
        out_shape=jax.ShapeDtypeStruct((B, LT, GLA_DV), F32),
        scratch_shapes=[pltpu.VMEM((GLA_HEADS, GLA_DV_HEAD, GLA_DK_HEAD), F32)],
        compiler_params=_cparams(2),
        name="gla_scan_bwd" if reverse else "gla_scan_fwd",
    )(q, k, v, g)


def _moe_kernel(pe_ref, nu_ref, tok_ref, tokn_ref, dst_ref, dstp_ref, u_hbm, wgu_ref, bgu_ref, wd_ref,
                bd_ref, yg_hbm, wgu_sc, wd_sc, x0, x1, y0, y1, gsem, ssem, *, dump_row):
    j = pl.program_id(0)
    n_used = nu_ref[0]
    H = MOE_BLOCK
    xbuf, ybuf = (x0, x1), (y0, y1)

    def gather(tab_ref, half, slot):
        for r in range(H):
            t = tab_ref[0, 0, half * H + r]
            pltpu.make_async_copy(u_hbm.at[t], xbuf[slot].at[r], gsem.at[slot]).start()

    def gather_wait(slot):
        pltpu.make_async_copy(u_hbm.at[pl.ds(0, H)], xbuf[slot], gsem.at[slot]).wait()

    def scatter(row_of, slot):
        for r in range(H):
            pltpu.make_async_copy(ybuf[slot].at[r], yg_hbm.at[row_of(r)], ssem.at[slot]).start()

    def scatter_wait(slot):
        pltpu.make_async_copy(ybuf[slot], yg_hbm.at[pl.ds(0, H)], ssem.at[slot]).wait()

    def ffn(slot):
        x = jnp.concatenate([xbuf[slot][:, sl, :] for sl in range(ROW_SLABS)], axis=1)
        gu = jnp.dot(x.astype(BF16), wgu_sc[...], preferred_element_type=F32) + bgu_ref[0, 0]
        half = gu.shape[1] // 2
        glu = jnp.minimum(gu[:, :half], SWIGLU_LIMIT)
        lin = jnp.clip(gu[:, half:], -SWIGLU_LIMIT, SWIGLU_LIMIT)
        act = glu * jax.nn.sigmoid(SWIGLU_ALPHA * glu) * (lin + 1.0)
        y = jnp.dot(act.astype(BF16), wd_sc[...], preferred_element_type=F32) + bd_ref[0, 0]
        for sl in range(ROW_SLABS):
            ybuf[slot][:, sl, :] = y[:, sl * LANES:(sl + 1) * LANES]

    @pl.when(j < n_used)
    def _():
        @pl.when(j == 0)
        def _():
            y0[...] = jnp.zeros(y0.shape, F32)
            y1[...] = jnp.zeros(y1.shape, F32)
            gather(tok_ref, 0, 0)
            scatter(lambda r: dump_row + r, 0)

        @pl.when((j == 0) | (pe_ref[j] != pe_ref[jnp.maximum(j - 1, 0)]))
        def _():
            wgu_sc[...] = wgu_ref[0, 0].astype(BF16)
            wd_sc[...] = wd_ref[0, 0].astype(BF16)

        gather_wait(0)
        scatter_wait(0)
        scatter(lambda r: dstp_ref[0, 0, H + r], 1)
        gather(tok_ref, 1, 1)
        ffn(0)
        gather_wait(1)
        scatter_wait(1)
        scatter(lambda r: dst_ref[0, 0, r], 0)
        gather(tokn_ref, 0, 0)
        ffn(1)

        @pl.when(j == n_used - 1)
        def _():
            gather_wait(0)
            scatter_wait(0)
            scatter(lambda r: dst_ref[0, 0, H + r], 1)
            scatter_wait(1)


def _moe_experts(layer, pair_expert, n_used, tok_tab, dst_tab, dst_prev_tab, u, w_gu, b_gu, w_down,
                 b_down, n_out_rows, dump_row):
    n_pairs = tok_tab.shape[0]
    D = D_MODEL
    pair = 2 * MOE_BLOCK
    tab = lambda f: pl.BlockSpec((1, 1, pair), f, memory_space=pltpu.SMEM)
    grid_spec = pltpu.PrefetchScalarGridSpec(
        num_scalar_prefetch=2,
        grid=(n_pairs,),
        in_specs=[tab(lambda j, pe, nu: (j, 0, 0)),
                  tab(lambda j, pe, nu: (jnp.minimum(j + 1, n_pairs - 1), 0, 0)),
                  tab(lambda j, pe, nu: (j, 0, 0)),
                  tab(lambda j, pe, nu: (j, 0, 0)),
                  pl.BlockSpec(memory_space=pl.ANY),
                  pl.BlockSpec((1, 1, D, 2 * D), lambda j, pe, nu: (layer, pe[j], 0, 0)),
                  pl.BlockSpec((1, 1, 1, 2 * D), lambda j, pe, nu: (layer, pe[j], 0, 0)),
                  pl.BlockSpec((1, 1, D, D), lambda j, pe, nu: (layer, pe[j], 0, 0)),
                  pl.BlockSpec((1, 1, 1, D), lambda j, pe, nu: (layer, pe[j], 0, 0))],
        out_specs=pl.BlockSpec(memory_space=pl.ANY),
        scratch_shapes=[pltpu.VMEM((D, 2 * D), BF16), pltpu.VMEM((D, D), BF16),
                        *([pltpu.VMEM((MOE_BLOCK, ROW_SLABS, LANES), F32)] * 4),
                        pltpu.SemaphoreType.DMA((2,)), pltpu.SemaphoreType.DMA((2,))],
    )
    return pl.pallas_call(
        functools.partial(_moe_kernel, dump_row=dump_row),
        grid_spec=grid_spec,
        out_shape=jax.ShapeDtypeStruct((n_out_rows, ROW_SLABS, LANES), F32),
        compiler_params=pltpu.CompilerParams(dimension_semantics=("arbitrary",),
                                             vmem_limit_bytes=MOE_VMEM_LIMIT),
        name="moe_experts",
    )(pair_expert, n_used, tok_tab, tok_tab, dst_tab, dst_prev_tab, u, w_gu,
      b_gu.reshape(DEPTH, N_EXPERTS, 1, 2 * D), w_down, b_down.reshape(DEPTH, N_EXPERTS, 1, D))


def _moe_ffn(layer, u, route, counts, w_gu, b_gu, w_down, b_down):
    T = u.shape[0]
    pair = 2 * MOE_BLOCK
    n_assign = T * TOP_K
    expert = route[:, :TOP_K].astype(jnp.int32)
    rank = route[:, TOP_K:2 * TOP_K].astype(jnp.int32)
    cnt = counts[0, :N_EXPERTS].astype(jnp.int32)
    padded = (cnt + pair - 1) // pair * pair
    padded_end = jnp.cumsum(padded)
    padded_start = padded_end - padded
    pos = (padded_start[expert] + rank).T
    n_pairs = -(-n_assign // pair) + N_EXPERTS
    n_rows = n_pairs * pair
    pair_start = jnp.arange(n_pairs, dtype=jnp.int32) * pair
    pair_expert = jnp.minimum(jnp.sum(padded_end[None, :] <= pair_start[:, None], axis=1),
                              N_EXPERTS - 1).astype(jnp.int32)
    n_used = (padded_end[-1:] // pair).astype(jnp.int32)
    _, sorted_a = lax.sort((pos.reshape(-1), jnp.arange(n_assign, dtype=jnp.int32)), num_keys=1)
    row = jnp.arange(n_rows, dtype=jnp.int32)
    row_e = jnp.repeat(pair_expert, pair)
    row_rank = row - padded_start[row_e]
    compact = jnp.cumsum(cnt)[row_e] - cnt[row_e] + row_rank
    valid = row_rank < cnt[row_e]
    assign = sorted_a[jnp.minimum(compact, n_assign - 1)]
    tok_tab = jnp.where(valid, assign % T, 0).reshape(n_pairs, 1, pair)
    dst_tab = jnp.where(valid, assign, n_assign + row % pair).reshape(n_pairs, 1, pair)
    dump = (n_assign + jnp.arange(pair, dtype=jnp.int32)).reshape(1, 1, pair)
    dst_prev_tab = jnp.concatenate([dump, dst_tab[:-1]], axis=0)
    return _moe_experts(layer, pair_expert, n_used, tok_tab, dst_tab, dst_prev_tab, u, w_gu, b_gu,
                        w_down, b_down, n_assign + pair, n_assign)


def _final_ln_kernel(x_ref, *refs):
    y_refs = refs[:TOP_K]
    rt_ref, mod_ref, ln_ref, o_ref = refs[TOP_K:]
    m = mod_ref[0]
    rt = rt_ref[0]
    parts = []
    for sl in range(ROW_SLABS):
        fs = rt[:, 2 * TOP_K:2 * TOP_K + 1] * y_refs[0][:, sl, :]
        for k in range(1, TOP_K):
            fs = fs + rt[:, 2 * TOP_K + k:2 * TOP_K + k + 1] * y_refs[k][:, sl, :]
        parts.append(fs)
    f = jnp.concatenate(parts, axis=1)
    z = DEEPNORM_ALPHA * x_ref[0] + m[MOD_G2:MOD_G2 + 1] * f
    mu = jnp.mean(z, axis=-1, keepdims=True)
    zc = z - mu
    y = zc * lax.rsqrt(jnp.mean(zc * zc, axis=-1, keepdims=True) + NORM_EPS)
    o_ref[0] = y * ln_ref[0:1] + ln_ref[1:2]


def _final_ln(x1, yg, route, mods, lnp, nbl):
    B, L, D = x1.shape
    nblk = L // TM
    blk = pl.BlockSpec((1, TM, D), lambda b, i: (b, i, 0))
    choice = lambda k: pl.BlockSpec((TM, ROW_SLABS, LANES),
                                    lambda b, i: (k * B * nblk + b * nblk + i, 0, 0))
    return pl.pallas_call(
        _final_ln_kernel,
        grid=(B, nblk),
        in_specs=[blk] + [choice(k) for k in range(TOP_K)] + [
                  pl.BlockSpec((1, TM, LANES), lambda b, i: (b, i, 0)),
                  pl.BlockSpec((1, 8, D), lambda b, i: (2 * b + i // nbl, 0, 0)),
                  pl.BlockSpec((2, D), lambda b, i: (0, 0))],
        out_specs=blk,
        out_shape=jax.ShapeDtypeStruct((B, L, D), F32),
        compiler_params=_cparams(2),
        name="final_ln",
    )(x1, *([yg] * TOP_K), route, mods, lnp)


def _rope_tables(S, n_ctx):
    rows = S // GRID_W
    row = jnp.repeat(jnp.arange(rows), GRID_W).astype(F32)
    col = jnp.tile(jnp.arange(GRID_W), rows).astype(F32)
    inv = ROPE_BASE ** (-jnp.arange(ROPE_PAIRS_AXIS, dtype=F32) / ROPE_PAIRS_AXIS)
    ang = jnp.concatenate([row[:, None] * inv, col[:, None] * inv], -1)
    cos, sin = jnp.cos(ang), jnp.sin(ang)
    cos = jnp.concatenate([cos, jnp.ones((n_ctx, cos.shape[1]), F32)], axis=0)
    sin = jnp.concatenate([sin, jnp.zeros((n_ctx, sin.shape[1]), F32)], axis=0)
    return (jnp.concatenate([cos, cos, cos, cos], axis=1),
            jnp.concatenate([-sin, -sin, sin, sin], axis=1))


def _qk_column_perm():
    lane = np.arange(DA_HEAD_W)
    half, mp, jj = lane // 64, (lane % 64) // 32, lane % 32
    src = mp * DA_HEAD_DIM + half * 32 + jj
    head = np.arange(DA_HEADS)[:, None] * DA_HEAD_W
    perm = (head + src[None, :]).reshape(-1)
    return np.concatenate([perm, D_MODEL + perm])


def _split_router_w(w):
    w = jnp.pad(w, ((0, 0), (0, LANES - N_EXPERTS)))
    hi = w.astype(BF16)
    lo = (w - hi.astype(F32)).astype(BF16)
    return jnp.concatenate([hi, lo], axis=1)


def _layer_mods(mod_rows, B):
    D = D_MODEL
    parts = mod_rows.reshape(8, 6, D)
    sh1, sc1, g1, sh2, sc2, g2 = (parts[:, n] for n in range(6))
    tab = jnp.stack([1.0 + sc1, sh1, g1, 1.0 + sc2, sh2, g2, jnp.zeros_like(g1),
                     jnp.zeros_like(g1)], axis=1)
    lat = tab[:B]
    ctx = jnp.broadcast_to(tab[B:B + 1], (B, 8, D))
    return jnp.stack([lat, ctx], axis=1).reshape(2 * B, 8, D)


def kernel(x, c, ctx, c_ctx, ada_w, ada_b, ln_g, ln_b, da_w_in, da_w_out, da_lambda, da_subln_w,
           gla_w_in, gla_w_gate, gla_b_gate, gla_norm_w, gla_w_out, router_w, router_b,
           moe_w_gu, moe_b_gu, moe_w_down, moe_b_down):
    B, S, D = x.shape
    n_ctx = ctx.shape[1]
    assert D == D_MODEL and n_ctx == TM and S % TM == 0 and S % GRID_W == 0 and B + 1 <= 8
    nbl = S // TM
    nb = nbl + 1

    cc = jnp.concatenate([c, c_ctx[None, :], jnp.zeros((8 - B - 1, D), F32)], axis=0)
    mod_all = _ada_mods(cc, ada_w, ada_b)
    xa = jnp.concatenate([x, ctx], axis=1)

    rw = [_split_router_w(router_w[i]) for i in range(DEPTH)]
    rb = [jnp.pad(router_b[i], (0, LANES - N_EXPERTS)).reshape(1, LANES) for i in range(DEPTH)]
    lnp = [[jnp.stack([ln_g[i, n], ln_b[i, n]]) for n in range(2)] for i in range(DEPTH)]

    mods = _layer_mods(mod_all[0], B)
    w_in = da_w_in[0]
    wqk = w_in[:, _qk_column_perm()].astype(BF16)
    wv = w_in[:, 2 * D:].astype(BF16)
    cos, sin = _rope_tables(S, n_ctx)
    q, k, v = _proj_da(xa, mods, cos, sin, wqk, wv)
    lam_init = _lambda_init(0)
    lv = da_lambda[0].astype(F32)
    lam = (jnp.exp(jnp.sum(lv[0] * lv[1])) - jnp.exp(jnp.sum(lv[2] * lv[3])) + lam_init).reshape(1)
    o = _diff_attention(lam, q, k, v, da_subln_w[0], lam_init)
    x1, u, route, counts = _mixer_out("da", [o], da_w_out[0].astype(BF16), xa, mods, lnp[0][0],
                                      rw[0], rb[0], nb)
    yg = _moe_ffn(0, u.reshape(B * nb * TM, ROW_SLABS, LANES), route.reshape(B * nb * TM, LANES), counts,
                  moe_w_gu, moe_b_gu, moe_w_down, moe_b_down)
    xa = _final_ln(x1, yg, route, mods, lnp[0][1], nbl)

    mods = _layer_mods(mod_all[1], B)
    gw = gla_w_in[0]
    c3 = 2 * GLA_DK + 2 * GLA_DV
    w_main = gw[:, :c3].astype(BF16)
    wz = jnp.pad(gw[:, c3:], ((0, 0), (0, LANES - 2 * GLA_GATE_RANK))).astype(BF16)
    wg = jnp.zeros((LANES, 2 * GLA_DK), F32)
    wg = wg.at[:GLA_GATE_RANK, :GLA_DK].set(gla_w_gate[0, 0])
    wg = wg.at[GLA_GATE_RANK:2 * GLA_GATE_RANK, GLA_DK:].set(gla_w_gate[0, 1])
    bg = gla_b_gate[0].reshape(1, 2 * GLA_DK)
    gq, gk, gv, gr, gg = _proj_gla(xa, mods, w_main, wz, wg, bg)
    of = _gla_scan(gq, gk, gv, gg, reverse=False)
    ob = _gla_scan(gq, gk, gv, gg, reverse=True)
    x1, u, route, counts = _mixer_out("gla", [of, ob, gr], gla_w_out[0].astype(BF16), xa, mods,
                                      lnp[1][0], rw[1], rb[1], nbl, norm_w=gla_norm_w[0])
    yg = _moe_ffn(1, u.reshape(B * S, ROW_SLABS, LANES), route.reshape(B * S, LANES), counts,
                  moe_w_gu, moe_b_gu, moe_w_down, moe_b_down)
    return _final_ln(x1, yg, route, mods, lnp[1][1], nbl)
```

```python
import functools
import math

import numpy as np
import jax
import jax.numpy as jnp
from jax import lax
from jax.experimental import pallas as pl
from jax.experimental.pallas import tpu as pltpu

F32 = jnp.float32
BF16 = jnp.bfloat16

D_MODEL = 1024
DEPTH = 2
GRID_W = 64

DA_HEADS = 8
DA_HEAD_DIM = 64
DA_HEAD_W = 2 * DA_HEAD_DIM
ROPE_BASE = 10000.0
ROPE_PAIRS_AXIS = DA_HEAD_DIM // 4

GLA_HEADS = 4
GLA_DK = D_MODEL // 2
GLA_DV = D_MODEL
GLA_DK_HEAD = GLA_DK // GLA_HEADS
GLA_DV_HEAD = GLA_DV // GLA_HEADS
GLA_GATE_RANK = 16
GLA_TAU = 16.0
GLA_CHUNK = 64

N_EXPERTS = 32
TOP_K = 4
SWIGLU_ALPHA = 1.702
SWIGLU_LIMIT = 7.0
MOE_BLOCK = 256

DEEPNORM_ALPHA = (2.0 * DEPTH) ** 0.25
NORM_EPS = 1e-5

LANES = 128
TM = 256
ATT_TK_MAX = 2816
VMEM_LIMIT = 48 * 1024 * 1024
MOE_VMEM_LIMIT = 56 * 1024 * 1024

MOD_SC1, MOD_SH1, MOD_G1, MOD_SC2, MOD_SH2, MOD_G2 = range(6)


def _cparams(n_axes):
    return pltpu.CompilerParams(dimension_semantics=("arbitrary",) * n_axes,
                                vmem_limit_bytes=VMEM_LIMIT)


def _lambda_init(layer_idx):
    return 0.8 - 0.6 * math.exp(-0.3 * layer_idx)


def _ada_kernel(c_ref, w_ref, b_ref, o_ref):
    c = c_ref[...]
    s = c * jax.nn.sigmoid(c)
    o_ref[0] = jnp.dot(s, w_ref[0], preferred_element_type=F32) + b_ref[0]


def _ada_mods(cc, ada_w, ada_b):
    nt = 1536
    n6 = 6 * D_MODEL
    return pl.pallas_call(
        _ada_kernel,
        grid=(DEPTH, n6 // nt),
        in_specs=[pl.BlockSpec((8, D_MODEL), lambda l, j: (0, 0)),
                  pl.BlockSpec((1, D_MODEL, nt), lambda l, j: (l, 0, j)),
                  pl.BlockSpec((1, 1, nt), lambda l, j: (l, 0, j))],
        out_specs=pl.BlockSpec((1, 8, nt), lambda l, j: (l, 0, j)),
        out_shape=jax.ShapeDtypeStruct((DEPTH, 8, n6), F32),
        compiler_params=_cparams(2),
        name="ada_mods",
    )(cc, ada_w, ada_b.reshape(DEPTH, 1, n6))


def _proj_da_kernel(x_ref, mod_ref, cos_ref, sin_ref, wqk_ref, wv_ref, q_ref, k_ref, v_ref):
    m = mod_ref[0]
    t = (x_ref[0] * m[MOD_SC1:MOD_SC1 + 1] + m[MOD_SH1:MOD_SH1 + 1]).astype(BF16)
    cos = cos_ref[...]
    sin = sin_ref[...]
    q_scale = DA_HEAD_DIM ** -0.5 * math.log2(math.e)
    for j in range(DA_HEADS):
        y2 = jnp.dot(t, wqk_ref[:, j * 256:(j + 1) * 256], preferred_element_type=F32)
        for hh in range(2):
            y = y2[:, hh * LANES:(hh + 1) * LANES]
            y = y * cos + pltpu.roll(y, 64, 1) * sin
            col = (2 * j + hh) * LANES
            if col < D_MODEL:
                q_ref[0, :, col:col + LANES] = (y * q_scale).astype(BF16)
            else:
                k_ref[0, :, col - D_MODEL:col - D_MODEL + LANES] = y.astype(BF16)
    v_ref[0] = jnp.dot(t, wv_ref[...], preferred_element_type=F32).astype(BF16)


def _proj_da(xa, mods, cos, sin, wqk, wv):
    B, LT, D = xa.shape
    nb = LT // TM
    nbl = nb - 1
    out = jax.ShapeDtypeStruct((B, LT, D), BF16)
    blk = pl.BlockSpec((1, TM, D), lambda b, i: (b, i, 0))
    return pl.pallas_call(
        _proj_da_kernel,
        grid=(B, nb),
        in_specs=[blk,
                  pl.BlockSpec((1, 8, D), lambda b, i: (2 * b + i // nbl, 0, 0)),
                  pl.BlockSpec((TM, LANES), lambda b, i: (i, 0)),
                  pl.BlockSpec((TM, LANES), lambda b, i: (i, 0)),
                  pl.BlockSpec((D, 2 * D), lambda b, i: (0, 0)),
                  pl.BlockSpec((D, D), lambda b, i: (0, 0))],
        out_specs=[blk, blk, blk],
        out_shape=[out, out, out],
        compiler_params=_cparams(2),
        name="da_proj",
    )(xa, mods, cos, sin, wqk, wv)


def _attn_kernel(lam_ref, q_ref, k_ref, v_ref, sw_ref, o_ref, vext_sc, m_sc, acc_sc, s_sc, *,
                 n_lat, n_ctx, lam_init):
    i = pl.program_id(2)
    tq = q_ref.shape[1]

    @pl.when(i == 0)
    def _():
        vext_sc[:, :DA_HEAD_W] = v_ref[0]
        vext_sc[:, DA_HEAD_W:] = jnp.ones((vext_sc.shape[0], DA_HEAD_W), BF16)

    q = q_ref[0]
    lane = lax.broadcasted_iota(jnp.int32, (1, DA_HEAD_W), 1)
    map0 = (lane % 64) < 32
    zero = jnp.zeros_like(q)
    qs = jnp.concatenate([jnp.where(map0, q, zero), jnp.where(map0, zero, q)], axis=0)
    m_sc[...] = jnp.full(m_sc.shape, -jnp.inf, F32)
    acc_sc[...] = jnp.zeros(acc_sc.shape, F32)

    def scores(off, tk):
        k = k_ref[0, pl.ds(off, tk), :]
        return lax.dot_general(qs, k, (((1,), (1,)), ((), ())), preferred_element_type=F32)

    def accumulate(s, off, tk):
        m_prev = m_sc[...]
        m_new = jnp.maximum(m_prev, jnp.max(s, axis=1, keepdims=True))
        alpha = jnp.exp2(m_prev - m_new)
        p = jnp.exp2(s - jnp.tile(m_new, (1, tk // LANES)))
        pv = jnp.dot(p.astype(BF16), vext_sc[pl.ds(off, tk), :], preferred_element_type=F32)
        acc_sc[...] = jnp.tile(alpha, (1, 2)) * acc_sc[...] + pv
        m_sc[...] = m_new

    n_q_lat = n_lat // tq
    tk = s_sc.shape[2]
    n_steps = (n_lat + n_ctx) // tk

    @pl.when(i < n_q_lat)
    def _():
        s_sc[0] = scores(0, tk)
        for t in range(n_steps):
            if t + 1 < n_steps:
                s_sc[(t + 1) % 2] = scores((t + 1) * tk, tk)
            accumulate(s_sc[t % 2], t * tk, tk)

    @pl.when(i >= n_q_lat)
    def _():
        accumulate(scores(n_lat, n_ctx), n_lat, n_ctx)

    acc = acc_sc[...]
    o0 = acc[:tq, :DA_HEAD_W] / acc[:tq, DA_HEAD_W:DA_HEAD_W + 1]
    o1 = acc[tq:, :DA_HEAD_W] / acc[tq:, DA_HEAD_W:DA_HEAD_W + 1]
    o = o0 - lam_ref[0] * o1
    o = o * lax.rsqrt(jnp.mean(o * o, axis=-1, keepdims=True) + NORM_EPS)
    o_ref[0] = (o * sw_ref[...] * (1.0 - lam_init)).astype(BF16)


def _diff_attention(lam, q, k, v, subln_w, lam_init):
    B, LT, D = q.shape
    nb = LT // TM
    tk = max(t for t in range(TM, ATT_TK_MAX + 1, TM) if LT % t == 0)
    kern = functools.partial(_attn_kernel, n_lat=LT - TM, n_ctx=TM, lam_init=lam_init)
    grid_spec = pltpu.PrefetchScalarGridSpec(
        num_scalar_prefetch=1,
        grid=(B, DA_HEADS, nb),
        in_specs=[pl.BlockSpec((1, TM, DA_HEAD_W), lambda b, h, i, lam: (b, i, h)),
                  pl.BlockSpec((1, LT, DA_HEAD_W), lambda b, h, i, lam: (b, 0, h)),
                  pl.BlockSpec((1, LT, DA_HEAD_W), lambda b, h, i, lam: (b, 0, h)),
                  pl.BlockSpec((1, DA_HEAD_W), lambda b, h, i, lam: (0, 0))],
        out_specs=pl.BlockSpec((1, TM, DA_HEAD_W), lambda b, h, i, lam: (b, i, h)),
        scratch_shapes=[pltpu.VMEM((LT, 2 * DA_HEAD_W), BF16),
                        pltpu.VMEM((2 * TM, LANES), F32),
                        pltpu.VMEM((2 * TM, 2 * DA_HEAD_W), F32),
                        pltpu.VMEM((2, 2 * TM, tk), F32)],
    )
    return pl.pallas_call(
        kern,
        grid_spec=grid_spec,
        out_shape=jax.ShapeDtypeStruct((B, LT, D), BF16),
        compiler_params=_cparams(3),
        name="diff_attn",
    )(lam, q, k, v, subln_w.reshape(1, DA_HEAD_W))


def _route_block(logits, cnt_sc):
    lane = lax.broadcasted_iota(jnp.int32, logits.shape, 1)
    lane_f = lane.astype(F32)
    work = jnp.where(lane < N_EXPERTS, logits, -jnp.inf)
    tops, idxs, hits = [], [], []
    for _ in range(TOP_K):
        mk = jnp.max(work, axis=1, keepdims=True)
        ik = jnp.min(jnp.where(work == mk, lane_f, float(LANES)), axis=1, keepdims=True)
        hit = lane_f == ik
        tops.append(mk)
        idxs.append(ik)
        hits.append(hit)
        work = jnp.where(hit, -jnp.inf, work)
    chosen = functools.reduce(jnp.logical_or, hits).astype(F32)
    n = logits.shape[0]
    row = lax.broadcasted_iota(jnp.int32, (n, n), 0)
    col = lax.broadcasted_iota(jnp.int32, (n, n), 1)
    before = jnp.dot((col < row).astype(BF16), chosen.astype(BF16), preferred_element_type=F32)
    rank_all = cnt_sc[...] + before
    cnt_sc[...] = cnt_sc[...] + jnp.sum(chosen, axis=0, keepdims=True)
    exps = [jnp.exp(t - tops[0]) for t in tops]
    denom = functools.reduce(jnp.add, exps)
    table = jnp.zeros(logits.shape, F32)
    for k in range(TOP_K):
        rk = jnp.sum(jnp.where(hits[k], rank_all, 0.0), axis=1, keepdims=True)
        table = jnp.where(lane == k, idxs[k], table)
        table = jnp.where(lane == TOP_K + k, rk, table)
        table = jnp.where(lane == 2 * TOP_K + k, exps[k] / denom, table)
    return table


def _post_mixer(pre, w_ref, x_ref, mod_ref, ln_ref, rw_ref, rb_ref, x1_ref, u_ref, rt_ref, cnt_ref,
                cnt_sc):
    @pl.when((pl.program_id(0) == 0) & (pl.program_id(1) == 0))
    def _():
        cnt_sc[...] = jnp.zeros(cnt_sc.shape, F32)

    m = mod_ref[0]
    y = jnp.dot(pre, w_ref[...], preferred_element_type=F32)
    z = DEEPNORM_ALPHA * x_ref[0] + m[MOD_G1:MOD_G1 + 1] * y
    mu = jnp.mean(z, axis=-1, keepdims=True)
    zc = z - mu
    x1 = zc * lax.rsqrt(jnp.mean(zc * zc, axis=-1, keepdims=True) + NORM_EPS)
    x1 = x1 * ln_ref[0:1] + ln_ref[1:2]
    x1_ref[0] = x1
    u = x1 * m[MOD_SC2:MOD_SC2 + 1] + m[MOD_SH2:MOD_SH2 + 1]
    u_ref[0] = u
    u_hi = u.astype(BF16)
    u_lo = (u - u_hi.astype(F32)).astype(BF16)
    d_hi = jnp.dot(u_hi, rw_ref[...], preferred_element_type=F32)
    d_lo = jnp.dot(u_lo, rw_ref[:, :LANES], preferred_element_type=F32)
    logits = d_hi[:, :LANES] + d_hi[:, LANES:] + d_lo + rb_ref[...]
    rt_ref[0] = _route_block(logits, cnt_sc)
    cnt_ref[...] = jnp.broadcast_to(cnt_sc[...], cnt_ref.shape)


def _out_da_kernel(o_ref, w_ref, x_ref, mod_ref, ln_ref, rw_ref, rb_ref, x1_ref, u_ref, rt_ref,
                   cnt_ref, cnt_sc):
    _post_mixer(o_ref[0], w_ref, x_ref, mod_ref, ln_ref, rw_ref, rb_ref, x1_ref, u_ref, rt_ref,
                cnt_ref, cnt_sc)


def _out_gla_kernel(of_ref, ob_ref, r_ref, nw_ref, w_ref, x_ref, mod_ref, ln_ref, rw_ref, rb_ref,
                    x1_ref, u_ref, rt_ref, cnt_ref, cnt_sc):
    parts = []
    for h in range(GLA_HEADS):
        sl = slice(h * GLA_DV_HEAD, (h + 1) * GLA_DV_HEAD)
        o = of_ref[0, :, sl] + ob_ref[0, :, sl]
        o = o * lax.rsqrt(jnp.mean(o * o, axis=-1, keepdims=True) + NORM_EPS) * nw_ref[...]
        r = r_ref[0, :, sl]
        parts.append((o * (r * jax.nn.sigmoid(r))).astype(BF16))
    pre = jnp.concatenate(parts, axis=1)
    _post_mixer(pre, w_ref, x_ref, mod_ref, ln_ref, rw_ref, rb_ref, x1_ref, u_ref, rt_ref, cnt_ref,
                cnt_sc)


def _mixer_out(kind, acts, w_out, xa, mods, lnp, rw, rb, nb_out, norm_w=None):
    B, LT, D = xa.shape
    nbl = LT // TM - 1
    blk = pl.BlockSpec((1, TM, D), lambda b, i: (b, i, 0))
    common_specs = [pl.BlockSpec((D, D), lambda b, i: (0, 0)),
                    blk,
                    pl.BlockSpec((1, 8, D), lambda b, i: (2 * b + i // nbl, 0, 0)),
                    pl.BlockSpec((2, D), lambda b, i: (0, 0)),
                    pl.BlockSpec((D, 2 * LANES), lambda b, i: (0, 0)),
                    pl.BlockSpec((1, LANES), lambda b, i: (0, 0))]
    lout = nb_out * TM
    out_shape = [jax.ShapeDtypeStruct((B, lout, D), F32),
                 jax.ShapeDtypeStruct((B, lout, D), F32),
                 jax.ShapeDtypeStruct((B, lout, LANES), F32),
                 jax.ShapeDtypeStruct((8, LANES), F32)]
    out_specs = [blk, blk, pl.BlockSpec((1, TM, LANES), lambda b, i: (b, i, 0)),
                 pl.BlockSpec((8, LANES), lambda b, i: (0, 0))]
    if kind == "da":
        kern = _out_da_kernel
        in_specs = [blk] + common_specs
        args = list(acts)
    else:
        kern = _out_gla_kernel
        in_specs = [blk, blk, blk, pl.BlockSpec((1, GLA_DV_HEAD), lambda b, i: (0, 0))] + common_specs
        args = list(acts) + [norm_w.reshape(1, GLA_DV_HEAD)]
    return pl.pallas_call(
        kern,
        grid=(B, nb_out),
        in_specs=in_specs,
        out_specs=out_specs,
        out_shape=out_shape,
        scratch_shapes=[pltpu.VMEM((1, LANES), F32)],
        compiler_params=_cparams(2),
        name="mixer_out_" + kind,
    )(*args, w_out, xa, mods, lnp, rw, rb)


def _proj_gla_kernel(x_ref, mod_ref, w_ref, wz_ref, wg_ref, bg_ref,
                     q_ref, k_ref, v_ref, r_ref, g_ref):
    m = mod_ref[0]
    t = (x_ref[0] * m[MOD_SC1:MOD_SC1 + 1] + m[MOD_SH1:MOD_SH1 + 1]).astype(BF16)
    c0, c1, c2, c3 = GLA_DK, 2 * GLA_DK, 2 * GLA_DK + GLA_DV, 2 * GLA_DK + 2 * GLA_DV
    q_ref[0] = jnp.dot(t, w_ref[:, :c0], preferred_element_type=F32) * (GLA_DK_HEAD ** -0.5)
    k_ref[0] = jnp.dot(t, w_ref[:, c0:c1], preferred_element_type=F32)
    v_ref[0] = jnp.dot(t, w_ref[:, c1:c2], preferred_element_type=F32).astype(BF16)
    r_ref[0] = jnp.dot(t, w_ref[:, c2:c3], preferred_element_type=F32)
    z = jnp.dot(t, wz_ref[...], preferred_element_type=F32)
    gl = jnp.dot(z, wg_ref[...], preferred_element_type=F32) + bg_ref[...]
    log_sig = jnp.minimum(gl, 0.0) - jnp.log1p(jnp.exp(-jnp.abs(gl)))
    g_ref[0] = log_sig * (1.0 / GLA_TAU)


def _proj_gla(xa, mods, w_main, wz, wg, bg):
    B, LT, D = xa.shape
    nb = LT // TM
    nbl = nb - 1
    blk = lambda w: pl.BlockSpec((1, TM, w), lambda b, i: (b, i, 0))
    return pl.pallas_call(
        _proj_gla_kernel,
        grid=(B, nb),
        in_specs=[blk(D),
                  pl.BlockSpec((1, 8, D), lambda b, i: (2 * b + i // nbl, 0, 0)),
                  pl.BlockSpec(w_main.shape, lambda b, i: (0, 0)),
                  pl.BlockSpec(wz.shape, lambda b, i: (0, 0)),
                  pl.BlockSpec(wg.shape, lambda b, i: (0, 0)),
                  pl.BlockSpec(bg.shape, lambda b, i: (0, 0))],
        out_specs=[blk(GLA_DK), blk(GLA_DK), blk(GLA_DV), blk(GLA_DV), blk(2 * GLA_DK)],
        out_shape=[jax.ShapeDtypeStruct((B, LT, GLA_DK), F32),
                   jax.ShapeDtypeStruct((B, LT, GLA_DK), F32),
                   jax.ShapeDtypeStruct((B, LT, GLA_DV), BF16),
                   jax.ShapeDtypeStruct((B, LT, GLA_DV), F32),
                   jax.ShapeDtypeStruct((B, LT, 2 * GLA_DK), F32)],
        compiler_params=_cparams(2),
        name="gla_proj",
    )(xa, mods, w_main, wz, wg, bg)


def _gla_scan_kernel(q_ref, k_ref, v_ref, g_ref, o_ref, st_sc, *, reverse):
    j = pl.program_id(1)

    @pl.when(j == 0)
    def _():
        st_sc[...] = jnp.zeros(st_sc.shape, F32)

    C = GLA_CHUNK
    n_chunks = TM // C

    def causal(n):
        row = lax.broadcasted_iota(jnp.int32, (n, n), 0)
        col = lax.broadcasted_iota(jnp.int32, (n, n), 1)
        return (row // C == col // C) & ((col >= row) if reverse else (col <= row))

    keep = causal(C)
    tri = causal(TM).astype(BF16)
    g = g_ref[0]
    g_hi = g.astype(BF16)
    rem = g - g_hi.astype(F32)
    g_mid = rem.astype(BF16)
    g_lo = (rem - g_mid.astype(F32)).astype(BF16)
    b_all = (jnp.dot(tri, g_hi, preferred_element_type=F32)
             + jnp.dot(tri, g_mid, preferred_element_type=F32)
             + jnp.dot(tri, g_lo, preferred_element_type=F32))

    states = [st_sc[h] for h in range(GLA_HEADS)]
    order = range(n_chunks - 1, -1, -1) if reverse else range(n_chunks)
    for c in order:
        rows = slice(c * C, (c + 1) * C)
        for h in range(GLA_HEADS):
            ks = slice(h * GLA_DK_HEAD, (h + 1) * GLA_DK_HEAD)
            vs = slice(h * GLA_DV_HEAD, (h + 1) * GLA_DV_HEAD)
            b = b_all[rows, ks]
            tot = b[0:1] if reverse else b[C - 1:C]
            q = q_ref[0, rows, ks]
            k = k_ref[0, rows, ks]
            q_in = (q * jnp.exp(b)).astype(BF16)
            k_in = (k * jnp.exp(-b)).astype(BF16)
            k_st = (k * jnp.exp(tot - b)).astype(BF16)
            att = lax.dot_general(q_in, k_in, (((1,), (1,)), ((), ())), preferred_element_type=F32)
            att = jnp.where(keep, att, 0.0).astype(BF16)
            v = v_ref[0, rows, vs]
            st = states[h]
            o = jnp.dot(att, v, preferred_element_type=F32)
            o = o + lax.dot_general(q_in, st.astype(BF16), (((1,), (1,)), ((), ())),
                                    preferred_element_type=F32)
            o_ref[0, rows, vs] = o
            ds = lax.dot_general(v, k_st, (((0,), (0,)), ((), ())), preferred_element_type=F32)
            states[h] = st * jnp.exp(tot) + ds
    for h in range(GLA_HEADS):
        st_sc[h] = states[h]


def _gla_scan(q, k, v, g, reverse):
    B, LT, _ = q.shape
    nb = LT // TM
    ctx_blk = nb - 1
    if reverse:
        order = lambda j: jnp.where(j == 0, ctx_blk, ctx_blk - j)
    else:
        order = lambda j: jnp.where(j == 0, ctx_blk, j - 1)
    gcol = 1 if reverse else 0
    return pl.pallas_call(
        functools.partial(_gla_scan_kernel, reverse=reverse),
        grid=(B, nb),
        in_specs=[pl.BlockSpec((1, TM, GLA_DK), lambda b, j: (b, order(j), 0)),
                  pl.BlockSpec((1, TM, GLA_DK), lambda b, j: (b, order(j), 0)),
                  pl.BlockSpec((1, TM, GLA_DV), lambda b, j: (b, order(j), 0)),
                  pl.BlockSpec((1, TM, GLA_DK), lambda b, j: (b, order(j), gcol))],
        out_specs=pl.BlockSpec((1, TM, GLA_DV), lambda b, j: (b, order(j), 0)),
        out_shape=jax.ShapeDtypeStruct((B, LT, GLA_DV), F32),
        scratch_shapes=[pltpu.VMEM((GLA_HEADS, GLA_DV_HEAD, GLA_DK_HEAD), F32)],
        compiler_params=_cparams(2),
        name="gla_scan_bwd" if reverse else "gla_scan_fwd",
    )(q, k, v, g)


def _moe_kernel(pe_ref, nu_ref, tok_ref, tokn_ref, u_hbm, wgu_ref, bgu_ref, wd_ref, bd_ref, y_ref,
                wgu_sc, wd_sc, x0, x1, gsem):
    j = pl.program_id(0)
    n_used = nu_ref[0]
    H = MOE_BLOCK
    xbuf = (x0, x1)

    def gather(tab_ref, half, slot):
        for r in range(H):
            t = tab_ref[0, 0, half * H + r]
            pltpu.make_async_copy(u_hbm.at[pl.ds(t, 1), :], xbuf[slot].at[pl.ds(r, 1), :],
                                  gsem.at[slot]).start()

    def gather_wait(slot):
        pltpu.make_async_copy(u_hbm.at[pl.ds(0, H), :], xbuf[slot], gsem.at[slot]).wait()

    def ffn(slot):
        gu = jnp.dot(xbuf[slot][...].astype(BF16), wgu_sc[...], preferred_element_type=F32)
        gu = gu + bgu_ref[0, 0]
        half = gu.shape[1] // 2
        glu = jnp.minimum(gu[:, :half], SWIGLU_LIMIT)
        lin = jnp.clip(gu[:, half:], -SWIGLU_LIMIT, SWIGLU_LIMIT)
        act = glu * jax.nn.sigmoid(SWIGLU_ALPHA * glu) * (lin + 1.0)
        y = jnp.dot(act.astype(BF16), wd_sc[...], preferred_element_type=F32) + bd_ref[0, 0]
        y_ref[slot * H:(slot + 1) * H, :] = y.astype(y_ref.dtype)

    @pl.when(j < n_used)
    def _():
        @pl.when(j == 0)
        def _():
            gather(tok_ref, 0, 0)

        @pl.when((j == 0) | (pe_ref[j] != pe_ref[jnp.maximum(j - 1, 0)]))
        def _():
            wgu_sc[...] = wgu_ref[0, 0].astype(BF16)
            wd_sc[...] = wd_ref[0, 0].astype(BF16)

        gather_wait(0)
        gather(tok_ref, 1, 1)
        ffn(0)
        gather_wait(1)
        gather(tokn_ref, 0, 0)
        ffn(1)

        @pl.when(j == n_used - 1)
        def _():
            gather_wait(0)

    @pl.when(j >= n_used)
    def _():
        y_ref[...] = jnp.zeros(y_ref.shape, y_ref.dtype)


def _moe_experts(layer, pair_expert, n_used, tok_tab, u, w_gu, b_gu, w_down, b_down):
    n_pairs = tok_tab.shape[0]
    D = u.shape[1]
    pair = 2 * MOE_BLOCK
    tab = lambda f: pl.BlockSpec((1, 1, pair), f, memory_space=pltpu.SMEM)
    grid_spec = pltpu.PrefetchScalarGridSpec(
        num_scalar_prefetch=2,
        grid=(n_pairs,),
        in_specs=[tab(lambda j, pe, nu: (j, 0, 0)),
                  tab(lambda j, pe, nu: (jnp.minimum(j + 1, n_pairs - 1), 0, 0)),
                  pl.BlockSpec(memory_space=pl.ANY),
                  pl.BlockSpec((1, 1, D, 2 * D), lambda j, pe, nu: (layer, pe[j], 0, 0)),
                  pl.BlockSpec((1, 1, 1, 2 * D), lambda j, pe, nu: (layer, pe[j], 0, 0)),
                  pl.BlockSpec((1, 1, D, D), lambda j, pe, nu: (layer, pe[j], 0, 0)),
                  pl.BlockSpec((1, 1, 1, D), lambda j, pe, nu: (layer, pe[j], 0, 0))],
        out_specs=pl.BlockSpec((pair, D), lambda j, pe, nu: (j, 0)),
        scratch_shapes=[pltpu.VMEM((D, 2 * D), BF16), pltpu.VMEM((D, D), BF16),
                        pltpu.VMEM((MOE_BLOCK, D), F32), pltpu.VMEM((MOE_BLOCK, D), F32),
                        pltpu.SemaphoreType.DMA((2,))],
    )
    return pl.pallas_call(
        _moe_kernel,
        grid_spec=grid_spec,
        out_shape=jax.ShapeDtypeStruct((n_pairs * pair, D), BF16),
        compiler_params=pltpu.CompilerParams(dimension_semantics=("arbitrary",),
                                             vmem_limit_bytes=MOE_VMEM_LIMIT),
        name="moe_experts",
    )(pair_expert, n_used, tok_tab, tok_tab, u, w_gu, b_gu.reshape(DEPTH, N_EXPERTS, 1, 2 * D),
      w_down, b_down.reshape(DEPTH, N_EXPERTS, 1, D))


def _moe_ffn(layer, u, route, counts, w_gu, b_gu, w_down, b_down):
    T, D = u.shape
    pair = 2 * MOE_BLOCK
    n_assign = T * TOP_K
    expert = route[:, :TOP_K].astype(jnp.int32)
    rank = route[:, TOP_K:2 * TOP_K].astype(jnp.int32)
    cnt = counts[0, :N_EXPERTS].astype(jnp.int32)
    padded = (cnt + pair - 1) // pair * pair
    padded_end = jnp.cumsum(padded)
    padded_start = padded_end - padded
    pos = (padded_start[expert] + rank).T
    n_pairs = -(-n_assign // pair) + N_EXPERTS
    n_rows = n_pairs * pair
    pair_start = jnp.arange(n_pairs, dtype=jnp.int32) * pair
    pair_expert = jnp.minimum(jnp.sum(padded_end[None, :] <= pair_start[:, None], axis=1),
                              N_EXPERTS - 1).astype(jnp.int32)
    n_used = (padded_end[-1:] // pair).astype(jnp.int32)
    tok = jnp.tile(jnp.arange(T, dtype=jnp.int32), TOP_K)
    _, sorted_tok = lax.sort((pos.reshape(-1), tok), num_keys=1)
    row = jnp.arange(n_rows, dtype=jnp.int32)
    row_e = jnp.repeat(pair_expert, pair)
    row_rank = row - padded_start[row_e]
    compact = jnp.cumsum(cnt)[row_e] - cnt[row_e] + row_rank
    tok_tab = jnp.where(row_rank < cnt[row_e], sorted_tok[jnp.minimum(compact, n_assign - 1)], 0)
    yb = _moe_experts(layer, pair_expert, n_used, tok_tab.reshape(n_pairs, 1, pair), u, w_gu, b_gu,
                      w_down, b_down)
    return yb[pos.reshape(-1)].reshape(TOP_K, T, D)


def _final_ln_kernel(x_ref, y_ref, rt_ref, mod_ref, ln_ref, o_ref):
    m = mod_ref[0]
    rt = rt_ref[0]
    f = rt[:, 2 * TOP_K:2 * TOP_K + 1] * y_ref[0].astype(F32)
    for k in range(1, TOP_K):
        f = f + rt[:, 2 * TOP_K + k:2 * TOP_K + k + 1] * y_ref[k].astype(F32)
    z = DEEPNORM_ALPHA * x_ref[0] + m[MOD_G2:MOD_G2 + 1] * f
    mu = jnp.mean(z, axis=-1, keepdims=True)
    zc = z - mu
    y = zc * lax.rsqrt(jnp.mean(zc * zc, axis=-1, keepdims=True) + NORM_EPS)
    o_ref[0] = y * ln_ref[0:1] + ln_ref[1:2]


def _final_ln(x1, yg, route, mods, lnp, nbl):
    B, L, D = x1.shape
    nblk = L // TM
    blk = pl.BlockSpec((1, TM, D), lambda b, i: (b, i, 0))
    return pl.pallas_call(
        _final_ln_kernel,
        grid=(B, nblk),
        in_specs=[blk,
                  pl.BlockSpec((TOP_K, TM, D), lambda b, i: (0, b * nblk + i, 0)),
                  pl.BlockSpec((1, TM, LANES), lambda b, i: (b, i, 0)),
                  pl.BlockSpec((1, 8, D), lambda b, i: (2 * b + i // nbl, 0, 0)),
                  pl.BlockSpec((2, D), lambda b, i: (0, 0))],
        out_specs=blk,
        out_shape=jax.ShapeDtypeStruct((B, L, D), F32),
        compiler_params=_cparams(2),
        name="final_ln",
    )(x1, yg, route, mods, lnp)


def _rope_tables(S, n_ctx):
    rows = S // GRID_W
    row = jnp.repeat(jnp.arange(rows), GRID_W).astype(F32)
    col = jnp.tile(jnp.arange(GRID_W), rows).astype(F32)
    inv = ROPE_BASE ** (-jnp.arange(ROPE_PAIRS_AXIS, dtype=F32) / ROPE_PAIRS_AXIS)
    ang = jnp.concatenate([row[:, None] * inv, col[:, None] * inv], -1)
    cos, sin = jnp.cos(ang), jnp.sin(ang)
    cos = jnp.concatenate([cos, jnp.ones((n_ctx, cos.shape[1]), F32)], axis=0)
    sin = jnp.concatenate([sin, jnp.zeros((n_ctx, sin.shape[1]), F32)], axis=0)
    return (jnp.concatenate([cos, cos, cos, cos], axis=1),
            jnp.concatenate([-sin, -sin, sin, sin], axis=1))


def _qk_column_perm():
    lane = np.arange(DA_HEAD_W)
    half, mp, jj = lane // 64, (lane % 64) // 32, lane % 32
    src = mp * DA_HEAD_DIM + half * 32 + jj
    head = np.arange(DA_HEADS)[:, None] * DA_HEAD_W
    perm = (head + src[None, :]).reshape(-1)
    return np.concatenate([perm, D_MODEL + perm])


def _split_router_w(w):
    w = jnp.pad(w, ((0, 0), (0, LANES - N_EXPERTS)))
    hi = w.astype(BF16)
    lo = (w - hi.astype(F32)).astype(BF16)
    return jnp.concatenate([hi, lo], axis=1)


def _layer_mods(mod_rows, B):
    D = D_MODEL
    parts = mod_rows.reshape(8, 6, D)
    sh1, sc1, g1, sh2, sc2, g2 = (parts[:, n] for n in range(6))
    tab = jnp.stack([1.0 + sc1, sh1, g1, 1.0 + sc2, sh2, g2, jnp.zeros_like(g1),
                     jnp.zeros_like(g1)], axis=1)
    lat = tab[:B]
    ctx = jnp.broadcast_to(tab[B:B + 1], (B, 8, D))
    return jnp.stack([lat, ctx], axis=1).reshape(2 * B, 8, D)


def kernel(x, c, ctx, c_ctx, ada_w, ada_b, ln_g, ln_b, da_w_in, da_w_out, da_lambda, da_subln_w,
           gla_w_in, gla_w_gate, gla_b_gate, gla_norm_w, gla_w_out, router_w, router_b,
           moe_w_gu, moe_b_gu, moe_w_down, moe_b_down):
    B, S, D = x.shape
    n_ctx = ctx.shape[1]
    assert D == D_MODEL and n_ctx == TM and S % TM == 0 and S % GRID_W == 0 and B + 1 <= 8
    nbl = S // TM
    nb = nbl + 1

    cc = jnp.concatenate([c, c_ctx[None, :], jnp.zeros((8 - B - 1, D), F32)], axis=0)
    mod_all = _ada_mods(cc, ada_w, ada_b)
    xa = jnp.concatenate([x, ctx], axis=1)

    rw = [_split_router_w(router_w[i]) for i in range(DEPTH)]
    rb = [jnp.pad(router_b[i], (0, LANES - N_EXPERTS)).reshape(1, LANES) for i in range(DEPTH)]
    lnp = [[jnp.stack([ln_g[i, n], ln_b[i, n]]) for n in range(2)] for i in range(DEPTH)]

    mods = _layer_mods(mod_all[0], B)
    w_in = da_w_in[0]
    wqk = w_in[:, _qk_column_perm()].astype(BF16)
    wv = w_in[:, 2 * D:].astype(BF16)
    cos, sin = _rope_tables(S, n_ctx)
    q, k, v = _proj_da(xa, mods, cos, sin, wqk, wv)
    lam_init = _lambda_init(0)
    lv = da_lambda[0].astype(F32)
    lam = (jnp.exp(jnp.sum(lv[0] * lv[1])) - jnp.exp(jnp.sum(lv[2] * lv[3])) + lam_init).reshape(1)
    o = _diff_attention(lam, q, k, v, da_subln_w[0], lam_init)
    x1, u, route, counts = _mixer_out("da", [o], da_w_out[0].astype(BF16), xa, mods, lnp[0][0],
                                      rw[0], rb[0], nb)
    yg = _moe_ffn(0, u.reshape(B * nb * TM, D), route.reshape(B * nb * TM, LANES), counts,
                  moe_w_gu, moe_b_gu, moe_w_down, moe_b_down)
    xa = _final_ln(x1, yg, route, mods, lnp[0][1], nbl)

    mods = _layer_mods(mod_all[1], B)
    gw = gla_w_in[0]
    c3 = 2 * GLA_DK + 2 * GLA_DV
    w_main = gw[:, :c3].astype(BF16)
    wz = jnp.pad(gw[:, c3:], ((0, 0), (0, LANES - 2 * GLA_GATE_RANK))).astype(BF16)
    wg = jnp.zeros((LANES, 2 * GLA_DK), F32)
    wg = wg.at[:GLA_GATE_RANK, :GLA_DK].set(gla_w_gate[0, 0])
    wg = wg.at[GLA_GATE_RANK:2 * GLA_GATE_RANK, GLA_DK:].set(gla_w_gate[0, 1])
    bg = gla_b_gate[0].reshape(1, 2 * GLA_DK)
    gq, gk, gv, gr, gg = _proj_gla(xa, mods, w_main, wz, wg, bg)
    of = _gla_scan(gq, gk, gv, gg, reverse=False)
    ob = _gla_scan(gq, gk, gv, gg, reverse=True)
    x1, u, route, counts = _mixer_out("gla", [of, ob, gr], gla_w_out[0].astype(BF16), xa, mods,
                                      lnp[1][0], rw[1], rb[1], nbl, norm_w=gla_norm_w[0])
    yg = _moe_ffn(1, u.reshape(B * S, D), route.reshape(B * S, LANES), counts,
                  moe_w_gu, moe_b_gu, moe_w_down, moe_b_down)
    return _final_ln(x1, yg, route, mods, lnp[1][1], nbl)
```

```python
import functools
import math

import numpy as np
import jax
import jax.numpy as jnp
from jax import lax
from jax.experimental import pallas as pl
from jax.experimental.pallas import tpu as pltpu

F32 = jnp.float32
BF16 = jnp.bfloat16

D_MODEL = 1024
DEPTH = 2
GRID_W = 64

DA_HEADS = 8
DA_HEAD_DIM = 64
DA_HEAD_W = 2 * DA_HEAD_DIM
ROPE_BASE = 10000.0
ROPE_PAIRS_AXIS = DA_HEAD_DIM // 4

GLA_HEADS = 4
GLA_DK = D_MODEL // 2
GLA_DV = D_MODEL
GLA_DK_HEAD = GLA_DK // GLA_HEADS
GLA_DV_HEAD = GLA_DV // GLA_HEADS
GLA_GATE_RANK = 16
GLA_TAU = 16.0
GLA_CHUNK = 64

N_EXPERTS = 32
TOP_K = 4
SWIGLU_ALPHA = 1.702
SWIGLU_LIMIT = 7.0
MOE_BLOCK = 256

DEEPNORM_ALPHA = (2.0 * DEPTH) ** 0.25
NORM_EPS = 1e-5

LANES = 128
TM = 256
ATT_TK_MAX = 2816
VMEM_LIMIT = 48 * 1024 * 1024
MOE_VMEM_LIMIT = 56 * 1024 * 1024

MOD_SC1, MOD_SH1, MOD_G1, MOD_SC2, MOD_SH2, MOD_G2 = range(6)


def _cparams(n_axes):
    return pltpu.CompilerParams(dimension_semantics=("arbitrary",) * n_axes,
                                vmem_limit_bytes=VMEM_LIMIT)


def _lambda_init(layer_idx):
    return 0.8 - 0.6 * math.exp(-0.3 * layer_idx)


def _ada_kernel(c_ref, w_ref, b_ref, o_ref):
    c = c_ref[...]
    s = c * jax.nn.sigmoid(c)
    o_ref[0] = jnp.dot(s, w_ref[0], preferred_element_type=F32) + b_ref[0]


def _ada_mods(cc, ada_w, ada_b):
    nt = 1536
    n6 = 6 * D_MODEL
    return pl.pallas_call(
        _ada_kernel,
        grid=(DEPTH, n6 // nt),
        in_specs=[pl.BlockSpec((8, D_MODEL), lambda l, j: (0, 0)),
                  pl.BlockSpec((1, D_MODEL, nt), lambda l, j: (l, 0, j)),
                  pl.BlockSpec((1, 1, nt), lambda l, j: (l, 0, j))],
        out_specs=pl.BlockSpec((1, 8, nt), lambda l, j: (l, 0, j)),
        out_shape=jax.ShapeDtypeStruct((DEPTH, 8, n6), F32),
        compiler_params=_cparams(2),
        name="ada_mods",
    )(cc, ada_w, ada_b.reshape(DEPTH, 1, n6))


def _proj_da_kernel(x_ref, mod_ref, cos_ref, sin_ref, wqk_ref, wv_ref, q_ref, k_ref, v_ref):
    m = mod_ref[0]
    t = (x_ref[0] * m[MOD_SC1:MOD_SC1 + 1] + m[MOD_SH1:MOD_SH1 + 1]).astype(BF16)
    cos = cos_ref[...]
    sin = sin_ref[...]
    q_scale = DA_HEAD_DIM ** -0.5 * math.log2(math.e)
    for j in range(DA_HEADS):
        y2 = jnp.dot(t, wqk_ref[:, j * 256:(j + 1) * 256], preferred_element_type=F32)
        for hh in range(2):
            y = y2[:, hh * LANES:(hh + 1) * LANES]
            y = y * cos + pltpu.roll(y, 64, 1) * sin
            col = (2 * j + hh) * LANES
            if col < D_MODEL:
                q_ref[0, :, col:col + LANES] = (y * q_scale).astype(BF16)
            else:
                k_ref[0, :, col - D_MODEL:col - D_MODEL + LANES] = y.astype(BF16)
    v_ref[0] = jnp.dot(t, wv_ref[...], preferred_element_type=F32).astype(BF16)


def _proj_da(xa, mods, cos, sin, wqk, wv):
    B, LT, D = xa.shape
    nb = LT // TM
    nbl = nb - 1
    out = jax.ShapeDtypeStruct((B, LT, D), BF16)
    blk = pl.BlockSpec((1, TM, D), lambda b, i: (b, i, 0))
    return pl.pallas_call(
        _proj_da_kernel,
        grid=(B, nb),
        in_specs=[blk,
                  pl.BlockSpec((1, 8, D), lambda b, i: (2 * b + i // nbl, 0, 0)),
                  pl.BlockSpec((TM, LANES), lambda b, i: (i, 0)),
                  pl.BlockSpec((TM, LANES), lambda b, i: (i, 0)),
                  pl.BlockSpec((D, 2 * D), lambda b, i: (0, 0)),
                  pl.BlockSpec((D, D), lambda b, i: (0, 0))],
        out_specs=[blk, blk, blk],
        out_shape=[out, out, out],
        compiler_params=_cparams(2),
        name="da_proj",
    )(xa, mods, cos, sin, wqk, wv)


def _attn_kernel(lam_ref, q_ref, k_ref, v_ref, sw_ref, o_ref, vext_sc, m_sc, acc_sc, s_sc, *,
                 n_lat, n_ctx, lam_init):
    i = pl.program_id(2)
    tq = q_ref.shape[1]

    @pl.when(i == 0)
    def _():
        vext_sc[:, :DA_HEAD_W] = v_ref[0]
        vext_sc[:, DA_HEAD_W:] = jnp.ones((vext_sc.shape[0], DA_HEAD_W), BF16)

    q = q_ref[0]
    lane = lax.broadcasted_iota(jnp.int32, (1, DA_HEAD_W), 1)
    map0 = (lane % 64) < 32
    zero = jnp.zeros_like(q)
    qs = jnp.concatenate([jnp.where(map0, q, zero), jnp.where(map0, zero, q)], axis=0)
    m_sc[...] = jnp.full(m_sc.shape, -jnp.inf, F32)
    acc_sc[...] = jnp.zeros(acc_sc.shape, F32)

    def scores(off, tk):
        k = k_ref[0, pl.ds(off, tk), :]
        return lax.dot_general(qs, k, (((1,), (1,)), ((), ())), preferred_element_type=F32)

    def accumulate(s, off, tk):
        m_prev = m_sc[...]
        m_new = jnp.maximum(m_prev, jnp.max(s, axis=1, keepdims=True))
        alpha = jnp.exp2(m_prev - m_new)
        p = jnp.exp2(s - jnp.tile(m_new, (1, tk // LANES)))
        pv = jnp.dot(p.astype(BF16), vext_sc[pl.ds(off, tk), :], preferred_element_type=F32)
        acc_sc[...] = jnp.tile(alpha, (1, 2)) * acc_sc[...] + pv
        m_sc[...] = m_new

    n_q_lat = n_lat // tq
    tk = s_sc.shape[2]
    n_steps = (n_lat + n_ctx) // tk

    @pl.when(i < n_q_lat)
    def _():
        s_sc[0] = scores(0, tk)
        for t in range(n_steps):
            if t + 1 < n_steps:
                s_sc[(t + 1) % 2] = scores((t + 1) * tk, tk)
            accumulate(s_sc[t % 2], t * tk, tk)

    @pl.when(i >= n_q_lat)
    def _():
        accumulate(scores(n_lat, n_ctx), n_lat, n_ctx)

    acc = acc_sc[...]
    o0 = acc[:tq, :DA_HEAD_W] / acc[:tq, DA_HEAD_W:DA_HEAD_W + 1]
    o1 = acc[tq:, :DA_HEAD_W] / acc[tq:, DA_HEAD_W:DA_HEAD_W + 1]
    o = o0 - lam_ref[0] * o1
    o = o * lax.rsqrt(jnp.mean(o * o, axis=-1, keepdims=True) + NORM_EPS)
    o_ref[0] = (o * sw_ref[...] * (1.0 - lam_init)).astype(BF16)


def _diff_attention(lam, q, k, v, subln_w, lam_init):
    B, LT, D = q.shape
    nb = LT // TM
    tk = max(t for t in range(TM, ATT_TK_MAX + 1, TM) if LT % t == 0)
    kern = functools.partial(_attn_kernel, n_lat=LT - TM, n_ctx=TM, lam_init=lam_init)
    grid_spec = pltpu.PrefetchScalarGridSpec(
        num_scalar_prefetch=1,
        grid=(B, DA_HEADS, nb),
        in_specs=[pl.BlockSpec((1, TM, DA_HEAD_W), lambda b, h, i, lam: (b, i, h)),
                  pl.BlockSpec((1, LT, DA_HEAD_W), lambda b, h, i, lam: (b, 0, h)),
                  pl.BlockSpec((1, LT, DA_HEAD_W), lambda b, h, i, lam: (b, 0, h)),
                  pl.BlockSpec((1, DA_HEAD_W), lambda b, h, i, lam: (0, 0))],
        out_specs=pl.BlockSpec((1, TM, DA_HEAD_W), lambda b, h, i, lam: (b, i, h)),
        scratch_shapes=[pltpu.VMEM((LT, 2 * DA_HEAD_W), BF16),
                        pltpu.VMEM((2 * TM, LANES), F32),
                        pltpu.VMEM((2 * TM, 2 * DA_HEAD_W), F32),
                        pltpu.VMEM((2, 2 * TM, tk), F32)],
    )
    return pl.pallas_call(
        kern,
        grid_spec=grid_spec,
        out_shape=jax.ShapeDtypeStruct((B, LT, D), BF16),
        compiler_params=_cparams(3),
        name="diff_attn",
    )(lam, q, k, v, subln_w.reshape(1, DA_HEAD_W))


def _route_block(logits, cnt_sc):
    lane = lax.broadcasted_iota(jnp.int32, logits.shape, 1)
    lane_f = lane.astype(F32)
    work = jnp.where(lane < N_EXPERTS, logits, -jnp.inf)
    tops, idxs, hits = [], [], []
    for _ in range(TOP_K):
        mk = jnp.max(work, axis=1, keepdims=True)
        ik = jnp.min(jnp.where(work == mk, lane_f, float(LANES)), axis=1, keepdims=True)
        hit = lane_f == ik
        tops.append(mk)
        idxs.append(ik)
        hits.append(hit)
        work = jnp.where(hit, -jnp.inf, work)
    chosen = functools.reduce(jnp.logical_or, hits).astype(F32)
    n = logits.shape[0]
    row = lax.broadcasted_iota(jnp.int32, (n, n), 0)
    col = lax.broadcasted_iota(jnp.int32, (n, n), 1)
    before = jnp.dot((col < row).astype(BF16), chosen.astype(BF16), preferred_element_type=F32)
    rank_all = cnt_sc[...] + before
    cnt_sc[...] = cnt_sc[...] + jnp.sum(chosen, axis=0, keepdims=True)
    exps = [jnp.exp(t - tops[0]) for t in tops]
    denom = functools.reduce(jnp.add, exps)
    table = jnp.zeros(logits.shape, F32)
    for k in range(TOP_K):
        rk = jnp.sum(jnp.where(hits[k], rank_all, 0.0), axis=1, keepdims=True)
        table = jnp.where(lane == k, idxs[k], table)
        table = jnp.where(lane == TOP_K + k, rk, table)
        table = jnp.where(lane == 2 * TOP_K + k, exps[k] / denom, table)
    return table


def _post_mixer(pre, w_ref, x_ref, mod_ref, ln_ref, rw_ref, rb_ref, x1_ref, u_ref, rt_ref, cnt_ref,
                cnt_sc):
    @pl.when((pl.program_id(0) == 0) & (pl.program_id(1) == 0))
    def _():
        cnt_sc[...] = jnp.zeros(cnt_sc.shape, F32)

    m = mod_ref[0]
    y = jnp.dot(pre, w_ref[...], preferred_element_type=F32)
    z = DEEPNORM_ALPHA * x_ref[0] + m[MOD_G1:MOD_G1 + 1] * y
    mu = jnp.mean(z, axis=-1, keepdims=True)
    zc = z - mu
    x1 = zc * lax.rsqrt(jnp.mean(zc * zc, axis=-1, keepdims=True) + NORM_EPS)
    x1 = x1 * ln_ref[0:1] + ln_ref[1:2]
    x1_ref[0] = x1
    u = x1 * m[MOD_SC2:MOD_SC2 + 1] + m[MOD_SH2:MOD_SH2 + 1]
    u_ref[0] = u
    u_hi = u.astype(BF16)
    u_lo = (u - u_hi.astype(F32)).astype(BF16)
    d_hi = jnp.dot(u_hi, rw_ref[...], preferred_element_type=F32)
    d_lo = jnp.dot(u_lo, rw_ref[:, :LANES], preferred_element_type=F32)
    logits = d_hi[:, :LANES] + d_hi[:, LANES:] + d_lo + rb_ref[...]
    rt_ref[0] = _route_block(logits, cnt_sc)
    cnt_ref[...] = jnp.broadcast_to(cnt_sc[...], cnt_ref.shape)


def _out_da_kernel(o_ref, w_ref, x_ref, mod_ref, ln_ref, rw_ref, rb_ref, x1_ref, u_ref, rt_ref,
                   cnt_ref, cnt_sc):
    _post_mixer(o_ref[0], w_ref, x_ref, mod_ref, ln_ref, rw_ref, rb_ref, x1_ref, u_ref, rt_ref,
                cnt_ref, cnt_sc)


def _out_gla_kernel(of_ref, ob_ref, r_ref, nw_ref, w_ref, x_ref, mod_ref, ln_ref, rw_ref, rb_ref,
                    x1_ref, u_ref, rt_ref, cnt_ref, cnt_sc):
    parts = []
    for h in range(GLA_HEADS):
        sl = slice(h * GLA_DV_HEAD, (h + 1) * GLA_DV_HEAD)
        o = of_ref[0, :, sl] + ob_ref[0, :, sl]
        o = o * lax.rsqrt(jnp.mean(o * o, axis=-1, keepdims=True) + NORM_EPS) * nw_ref[...]
        r = r_ref[0, :, sl]
        parts.append((o * (r * jax.nn.sigmoid(r))).astype(BF16))
    pre = jnp.concatenate(parts, axis=1)
    _post_mixer(pre, w_ref, x_ref, mod_ref, ln_ref, rw_ref, rb_ref, x1_ref, u_ref, rt_ref, cnt_ref,
                cnt_sc)


def _mixer_out(kind, acts, w_out, xa, mods, lnp, rw, rb, nb_out, norm_w=None):
    B, LT, D = xa.shape
    nbl = LT // TM - 1
    blk = pl.BlockSpec((1, TM, D), lambda b, i: (b, i, 0))
    common_specs = [pl.BlockSpec((D, D), lambda b, i: (0, 0)),
                    blk,
                    pl.BlockSpec((1, 8, D), lambda b, i: (2 * b + i // nbl, 0, 0)),
                    pl.BlockSpec((2, D), lambda b, i: (0, 0)),
                    pl.BlockSpec((D, 2 * LANES), lambda b, i: (0, 0)),
                    pl.BlockSpec((1, LANES), lambda b, i: (0, 0))]
    lout = nb_out * TM
    out_shape = [jax.ShapeDtypeStruct((B, lout, D), F32),
                 jax.ShapeDtypeStruct((B, lout, D), F32),
                 jax.ShapeDtypeStruct((B, lout, LANES), F32),
                 jax.ShapeDtypeStruct((8, LANES), F32)]
    out_specs = [blk, blk, pl.BlockSpec((1, TM, LANES), lambda b, i: (b, i, 0)),
                 pl.BlockSpec((8, LANES), lambda b, i: (0, 0))]
    if kind == "da":
        kern = _out_da_kernel
        in_specs = [blk] + common_specs
        args = list(acts)
    else:
        kern = _out_gla_kernel
        in_specs = [blk, blk, blk, pl.BlockSpec((1, GLA_DV_HEAD), lambda b, i: (0, 0))] + common_specs
        args = list(acts) + [norm_w.reshape(1, GLA_DV_HEAD)]
    return pl.pallas_call(
        kern,
        grid=(B, nb_out),
        in_specs=in_specs,
        out_specs=out_specs,
        out_shape=out_shape,
        scratch_shapes=[pltpu.VMEM((1, LANES), F32)],
        compiler_params=_cparams(2),
        name="mixer_out_" + kind,
    )(*args, w_out, xa, mods, lnp, rw, rb)


def _proj_gla_kernel(x_ref, mod_ref, w_ref, wz_ref, wg_ref, bg_ref,
                     q_ref, k_ref, v_ref, r_ref, g_ref):
    m = mod_ref[0]
    t = (x_ref[0] * m[MOD_SC1:MOD_SC1 + 1] + m[MOD_SH1:MOD_SH1 + 1]).astype(BF16)
    c0, c1, c2, c3 = GLA_DK, 2 * GLA_DK, 2 * GLA_DK + GLA_DV, 2 * GLA_DK + 2 * GLA_DV
    q_ref[0] = jnp.dot(t, w_ref[:, :c0], preferred_element_type=F32) * (GLA_DK_HEAD ** -0.5)
    k_ref[0] = jnp.dot(t, w_ref[:, c0:c1], preferred_element_type=F32)
    v_ref[0] = jnp.dot(t, w_ref[:, c1:c2], preferred_element_type=F32).astype(BF16)
    r_ref[0] = jnp.dot(t, w_ref[:, c2:c3], preferred_element_type=F32)
    z = jnp.dot(t, wz_ref[...], preferred_element_type=F32)
    gl = jnp.dot(z, wg_ref[...], preferred_element_type=F32) + bg_ref[...]
    log_sig = jnp.minimum(gl, 0.0) - jnp.log1p(jnp.exp(-jnp.abs(gl)))
    g_ref[0] = log_sig * (1.0 / GLA_TAU)


def _proj_gla(xa, mods, w_main, wz, wg, bg):
    B, LT, D = xa.shape
    nb = LT // TM
    nbl = nb - 1
    blk = lambda w: pl.BlockSpec((1, TM, w), lambda b, i: (b, i, 0))
    return pl.pallas_call(
        _proj_gla_kernel,
        grid=(B, nb),
        in_specs=[blk(D),
                  pl.BlockSpec((1, 8, D), lambda b, i: (2 * b + i // nbl, 0, 0)),
                  pl.BlockSpec(w_main.shape, lambda b, i: (0, 0)),
                  pl.BlockSpec(wz.shape, lambda b, i: (0, 0)),
                  pl.BlockSpec(wg.shape, lambda b, i: (0, 0)),
                  pl.BlockSpec(bg.shape, lambda b, i: (0, 0))],
        out_specs=[blk(GLA_DK), blk(GLA_DK), blk(GLA_DV), blk(GLA_DV), blk(2 * GLA_DK)],
        out_shape=[jax.ShapeDtypeStruct((B, LT, GLA_DK), F32),
                   jax.ShapeDtypeStruct((B, LT, GLA_DK), F32),
                   jax.ShapeDtypeStruct((B, LT, GLA_DV), BF16),
                   jax.ShapeDtypeStruct((B, LT, GLA_DV), F32),
                   jax.ShapeDtypeStruct((B, LT, 2 * GLA_DK), F32)],
        compiler_params=_cparams(2),
        name="gla_proj",
    )(xa, mods, w_main, wz, wg, bg)


def _gla_scan_kernel(q_ref, k_ref, v_ref, g_ref, o_ref, st_sc, *, reverse):
    j = pl.program_id(1)

    @pl.when(j == 0)
    def _():
        st_sc[...] = jnp.zeros(st_sc.shape, F32)

    C = GLA_CHUNK
    n_chunks = TM // C

    def causal(n):
        row = lax.broadcasted_iota(jnp.int32, (n, n), 0)
        col = lax.broadcasted_iota(jnp.int32, (n, n), 1)
        return (row // C == col // C) & ((col >= row) if reverse else (col <= row))

    keep = causal(C)
    tri = causal(TM).astype(BF16)
    g = g_ref[0]
    g_hi = g.astype(BF16)
    rem = g - g_hi.astype(F32)
    g_mid = rem.astype(BF16)
    g_lo = (rem - g_mid.astype(F32)).astype(BF16)
    b_all = (jnp.dot(tri, g_hi, preferred_element_type=F32)
             + jnp.dot(tri, g_mid, preferred_element_type=F32)
             + jnp.dot(tri, g_lo, preferred_element_type=F32))

    states = [st_sc[h] for h in range(GLA_HEADS)]
    order = range(n_chunks - 1, -1, -1) if reverse else range(n_chunks)
    for c in order:
        rows = slice(c * C, (c + 1) * C)
        for h in range(GLA_HEADS):
            ks = slice(h * GLA_DK_HEAD, (h + 1) * GLA_DK_HEAD)
            vs = slice(h * GLA_DV_HEAD, (h + 1) * GLA_DV_HEAD)
            b = b_all[rows, ks]
            tot = b[0:1] if reverse else b[C - 1:C]
            q = q_ref[0, rows, ks]
            k = k_ref[0, rows, ks]
            q_in = (q * jnp.exp(b)).astype(BF16)
            k_in = (k * jnp.exp(-b)).astype(BF16)
            k_st = (k * jnp.exp(tot - b)).astype(BF16)
            att = lax.dot_general(q_in, k_in, (((1,), (1,)), ((), ())), preferred_element_type=F32)
            att = jnp.where(keep, att, 0.0).astype(BF16)
            v = v_ref[0, rows, vs]
            st = states[h]
            o = jnp.dot(att, v, preferred_element_type=F32)
            o = o + lax.dot_general(q_in, st.astype(BF16), (((1,), (1,)), ((), ())),
                                    preferred_element_type=F32)
            o_ref[0, rows, vs] = o
            ds = lax.dot_general(v, k_st, (((0,), (0,)), ((), ())), preferred_element_type=F32)
            states[h] = st * jnp.exp(tot) + ds
    for h in range(GLA_HEADS):
        st_sc[h] = states[h]


def _gla_scan(q, k, v, g, reverse):
    B, LT, _ = q.shape
    nb = LT // TM
    ctx_blk = nb - 1
    if reverse:
        order = lambda j: jnp.where(j == 0, ctx_blk, ctx_blk - j)
    else:
        order = lambda j: jnp.where(j == 0, ctx_blk, j - 1)
    gcol = 1 if reverse else 0
    return pl.pallas_call(
        functools.partial(_gla_scan_kernel, reverse=reverse),
        grid=(B, nb),
        in_specs=[pl.BlockSpec((1, TM, GLA_DK), lambda b, j: (b, order(j), 0)),
                  pl.BlockSpec((1, TM, GLA_DK), lambda b, j: (b, order(j), 0)),
                  pl.BlockSpec((1, TM, GLA_DV), lambda b, j: (b, order(j), 0)),
                  pl.BlockSpec((1, TM, GLA_DK), lambda b, j: (b, order(j), gcol))],
        out_specs=pl.BlockSpec((1, TM, GLA_DV), lambda b, j: (b, order(j), 0)),
        out_shape=jax.ShapeDtypeStruct((B, LT, GLA_DV), F32),
        scratch_shapes=[pltpu.VMEM((GLA_HEADS, GLA_DV_HEAD, GLA_DK_HEAD), F32)],
        compiler_params=_cparams(2),
        name="gla_scan_bwd" if reverse else "gla_scan_fwd",
    )(q, k, v, g)


def _moe_kernel(pe_ref, nu_ref, tok_ref, tokn_ref, u_hbm, wgu_ref, bgu_ref, wd_ref, bd_ref, y_ref,
                wgu_sc, wd_sc, x0, x1, gsem):
    j = pl.program_id(0)
    n_used = nu_ref[0]
    H = MOE_BLOCK
    xbuf = (x0, x1)

    def gather(tab_ref, half, slot):
        for r in range(H):
            t = tab_ref[0, 0, half * H + r]
            pltpu.make_async_copy(u_hbm.at[pl.ds(t, 1), :], xbuf[slot].at[pl.ds(r, 1), :],
                                  gsem.at[slot]).start(priority=r % 2)

    def gather_wait(slot):
        pltpu.make_async_copy(u_hbm.at[pl.ds(0, H), :], xbuf[slot], gsem.at[slot]).wait()

    def ffn(slot):
        gu = jnp.dot(xbuf[slot][...].astype(BF16), wgu_sc[...], preferred_element_type=F32)
        gu = gu + bgu_ref[0, 0]
        half = gu.shape[1] // 2
        glu = jnp.minimum(gu[:, :half], SWIGLU_LIMIT)
        lin = jnp.clip(gu[:, half:], -SWIGLU_LIMIT, SWIGLU_LIMIT)
        act = glu * jax.nn.sigmoid(SWIGLU_ALPHA * glu) * (lin + 1.0)
        y = jnp.dot(act.astype(BF16), wd_sc[...], preferred_element_type=F32) + bd_ref[0, 0]
        y_ref[slot * H:(slot + 1) * H, :] = y.astype(y_ref.dtype)

    @pl.when(j < n_used)
    def _():
        @pl.when(j == 0)
        def _():
            gather(tok_ref, 0, 0)

        @pl.when((j == 0) | (pe_ref[j] != pe_ref[jnp.maximum(j - 1, 0)]))
        def _():
            wgu_sc[...] = wgu_ref[0, 0].astype(BF16)
            wd_sc[...] = wd_ref[0, 0].astype(BF16)

        gather_wait(0)
        gather(tok_ref, 1, 1)
        ffn(0)
        gather_wait(1)
        gather(tokn_ref, 0, 0)
        ffn(1)

        @pl.when(j == n_used - 1)
        def _():
            gather_wait(0)

    @pl.when(j >= n_used)
    def _():
        y_ref[...] = jnp.zeros(y_ref.shape, y_ref.dtype)


def _moe_experts(layer, pair_expert, n_used, tok_tab, u, w_gu, b_gu, w_down, b_down):
    n_pairs = tok_tab.shape[0]
    D = u.shape[1]
    pair = 2 * MOE_BLOCK
    tab = lambda f: pl.BlockSpec((1, 1, pair), f, memory_space=pltpu.SMEM)
    grid_spec = pltpu.PrefetchScalarGridSpec(
        num_scalar_prefetch=2,
        grid=(n_pairs,),
        in_specs=[tab(lambda j, pe, nu: (j, 0, 0)),
                  tab(lambda j, pe, nu: (jnp.minimum(j + 1, n_pairs - 1), 0, 0)),
                  pl.BlockSpec(memory_space=pl.ANY),
                  pl.BlockSpec((1, 1, D, 2 * D), lambda j, pe, nu: (layer, pe[j], 0, 0)),
                  pl.BlockSpec((1, 1, 1, 2 * D), lambda j, pe, nu: (layer, pe[j], 0, 0)),
                  pl.BlockSpec((1, 1, D, D), lambda j, pe, nu: (layer, pe[j], 0, 0)),
                  pl.BlockSpec((1, 1, 1, D), lambda j, pe, nu: (layer, pe[j], 0, 0))],
        out_specs=pl.BlockSpec((pair, D), lambda j, pe, nu: (j, 0)),
        scratch_shapes=[pltpu.VMEM((D, 2 * D), BF16), pltpu.VMEM((D, D), BF16),
                        pltpu.VMEM((MOE_BLOCK, D), F32), pltpu.VMEM((MOE_BLOCK, D), F32),
                        pltpu.SemaphoreType.DMA((2,))],
    )
    return pl.pallas_call(
        _moe_kernel,
        grid_spec=grid_spec,
        out_shape=jax.ShapeDtypeStruct((n_pairs * pair, D), BF16),
        compiler_params=pltpu.CompilerParams(dimension_semantics=("arbitrary",),
                                             vmem_limit_bytes=MOE_VMEM_LIMIT),
        name="moe_experts",
    )(pair_expert, n_used, tok_tab, tok_tab, u, w_gu, b_gu.reshape(DEPTH, N_EXPERTS, 1, 2 * D),
      w_down, b_down.reshape(DEPTH, N_EXPERTS, 1, D))


def _moe_ffn(layer, u, route, counts, w_gu, b_gu, w_down, b_down):
    T, D = u.shape
    pair = 2 * MOE_BLOCK
    n_assign = T * TOP_K
    expert = route[:, :TOP_K].astype(jnp.int32)
    rank = route[:, TOP_K:2 * TOP_K].astype(jnp.int32)
    cnt = counts[0, :N_EXPERTS].astype(jnp.int32)
    padded = (cnt + pair - 1) // pair * pair
    padded_end = jnp.cumsum(padded)
    padded_start = padded_end - padded
    pos = (padded_start[expert] + rank).T
    n_pairs = -(-n_assign // pair) + N_EXPERTS
    n_rows = n_pairs * pair
    pair_start = jnp.arange(n_pairs, dtype=jnp.int32) * pair
    pair_expert = jnp.minimum(jnp.sum(padded_end[None, :] <= pair_start[:, None], axis=1),
                              N_EXPERTS - 1).astype(jnp.int32)
    n_used = (padded_end[-1:] // pair).astype(jnp.int32)
    tok = jnp.tile(jnp.arange(T, dtype=jnp.int32), TOP_K)
    _, sorted_tok = lax.sort((pos.reshape(-1), tok), num_keys=1)
    row = jnp.arange(n_rows, dtype=jnp.int32)
    row_e = jnp.repeat(pair_expert, pair)
    row_rank = row - padded_start[row_e]
    compact = jnp.cumsum(cnt)[row_e] - cnt[row_e] + row_rank
    tok_tab = jnp.where(row_rank < cnt[row_e], sorted_tok[jnp.minimum(compact, n_assign - 1)], 0)
    yb = _moe_experts(layer, pair_expert, n_used, tok_tab.reshape(n_pairs, 1, pair), u, w_gu, b_gu,
                      w_down, b_down)
    return yb[pos.reshape(-1)].reshape(TOP_K, T, D)


def _final_ln_kernel(x_ref, y_ref, rt_ref, mod_ref, ln_ref, o_ref):
    m = mod_ref[0]
    rt = rt_ref[0]
    f = rt[:, 2 * TOP_K:2 * TOP_K + 1] * y_ref[0].astype(F32)
    for k in range(1, TOP_K):
        f = f + rt[:, 2 * TOP_K + k:2 * TOP_K + k + 1] * y_ref[k].astype(F32)
    z = DEEPNORM_ALPHA * x_ref[0] + m[MOD_G2:MOD_G2 + 1] * f
    mu = jnp.mean(z, axis=-1, keepdims=True)
    zc = z - mu
    y = zc * lax.rsqrt(jnp.mean(zc * zc, axis=-1, keepdims=True) + NORM_EPS)
    o_ref[0] = y * ln_ref[0:1] + ln_ref[1:2]


def _final_ln(x1, yg, route, mods, lnp, nbl):
    B, L, D = x1.shape
    nblk = L // TM
    blk = pl.BlockSpec((1, TM, D), lambda b, i: (b, i, 0))
    return pl.pallas_call(
        _final_ln_kernel,
        grid=(B, nblk),
        in_specs=[blk,
                  pl.BlockSpec((TOP_K, TM, D), lambda b, i: (0, b * nblk + i, 0)),
                  pl.BlockSpec((1, TM, LANES), lambda b, i: (b, i, 0)),
                  pl.BlockSpec((1, 8, D), lambda b, i: (2 * b + i // nbl, 0, 0)),
                  pl.BlockSpec((2, D), lambda b, i: (0, 0))],
        out_specs=blk,
        out_shape=jax.ShapeDtypeStruct((B, L, D), F32),
        compiler_params=_cparams(2),
        name="final_ln",
    )(x1, yg, route, mods, lnp)


def _rope_tables(S, n_ctx):
    rows = S // GRID_W
    row = jnp.repeat(jnp.arange(rows), GRID_W).astype(F32)
    col = jnp.tile(jnp.arange(GRID_W), rows).astype(F32)
    inv = ROPE_BASE ** (-jnp.arange(ROPE_PAIRS_AXIS, dtype=F32) / ROPE_PAIRS_AXIS)
    ang = jnp.concatenate([row[:, None] * inv, col[:, None] * inv], -1)
    cos, sin = jnp.cos(ang), jnp.sin(ang)
    cos = jnp.concatenate([cos, jnp.ones((n_ctx, cos.shape[1]), F32)], axis=0)
    sin = jnp.concatenate([sin, jnp.zeros((n_ctx, sin.shape[1]), F32)], axis=0)
    return (jnp.concatenate([cos, cos, cos, cos], axis=1),
            jnp.concatenate([-sin, -sin, sin, sin], axis=1))


def _qk_column_perm():
    lane = np.arange(DA_HEAD_W)
    half, mp, jj = lane // 64, (lane % 64) // 32, lane % 32
    src = mp * DA_HEAD_DIM + half * 32 + jj
    head = np.arange(DA_HEADS)[:, None] * DA_HEAD_W
    perm = (head + src[None, :]).reshape(-1)
    return np.concatenate([perm, D_MODEL + perm])


def _split_router_w(w):
    w = jnp.pad(w, ((0, 0), (0, LANES - N_EXPERTS)))
    hi = w.astype(BF16)
    lo = (w - hi.astype(F32)).astype(BF16)
    return jnp.concatenate([hi, lo], axis=1)


def _layer_mods(mod_rows, B):
    D = D_MODEL
    parts = mod_rows.reshape(8, 6, D)
    sh1, sc1, g1, sh2, sc2, g2 = (parts[:, n] for n in range(6))
    tab = jnp.stack([1.0 + sc1, sh1, g1, 1.0 + sc2, sh2, g2, jnp.zeros_like(g1),
                     jnp.zeros_like(g1)], axis=1)
    lat = tab[:B]
    ctx = jnp.broadcast_to(tab[B:B + 1], (B, 8, D))
    return jnp.stack([lat, ctx], axis=1).reshape(2 * B, 8, D)


def kernel(x, c, ctx, c_ctx, ada_w, ada_b, ln_g, ln_b, da_w_in, da_w_out, da_lambda, da_subln_w,
           gla_w_in, gla_w_gate, gla_b_gate, gla_norm_w, gla_w_out, router_w, router_b,
           moe_w_gu, moe_b_gu, moe_w_down, moe_b_down):
    B, S, D = x.shape
    n_ctx = ctx.shape[1]
    assert D == D_MODEL and n_ctx == TM and S % TM == 0 and S % GRID_W == 0 and B + 1 <= 8
    nbl = S // TM
    nb = nbl + 1

    cc = jnp.concatenate([c, c_ctx[None, :], jnp.zeros((8 - B - 1, D), F32)], axis=0)
    mod_all = _ada_mods(cc, ada_w, ada_b)
    xa = jnp.concatenate([x, ctx], axis=1)

    rw = [_split_router_w(router_w[i]) for i in range(DEPTH)]
    rb = [jnp.pad(router_b[i], (0, LANES - N_EXPERTS)).reshape(1, LANES) for i in range(DEPTH)]
    lnp = [[jnp.stack([ln_g[i, n], ln_b[i, n]]) for n in range(2)] for i in range(DEPTH)]

    mods = _layer_mods(mod_all[0], B)
    w_in = da_w_in[0]
    wqk = w_in[:, _qk_column_perm()].astype(BF16)
    wv = w_in[:, 2 * D:].astype(BF16)
    cos, sin = _rope_tables(S, n_ctx)
    q, k, v = _proj_da(xa, mods, cos, sin, wqk, wv)
    lam_init = _lambda_init(0)
    lv = da_lambda[0].astype(F32)
    lam = (jnp.exp(jnp.sum(lv[0] * lv[1])) - jnp.exp(jnp.sum(lv[2] * lv[3])) + lam_init).reshape(1)
    o = _diff_attention(lam, q, k, v, da_subln_w[0], lam_init)
    x1, u, route, counts = _mixer_out("da", [o], da_w_out[0].astype(BF16), xa, mods, lnp[0][0],
                                      rw[0], rb[0], nb)
    yg = _moe_ffn(0, u.reshape(B * nb * TM, D), route.reshape(B * nb * TM, LANES), counts,
                  moe_w_gu, moe_b_gu, moe_w_down, moe_b_down)
    xa = _final_ln(x1, yg, route, mods, lnp[0][1], nbl)

    mods = _layer_mods(mod_all[1], B)
    gw = gla_w_in[0]
    c3 = 2 * GLA_DK + 2 * GLA_DV
    w_main = gw[:, :c3].astype(BF16)
    wz = jnp.pad(gw[:, c3:], ((0, 0), (0, LANES - 2 * GLA_GATE_RANK))).astype(BF16)
    wg = jnp.zeros((LANES, 2 * GLA_DK), F32)
    wg = wg.at[:GLA_GATE_RANK, :GLA_DK].set(gla_w_gate[0, 0])
    wg = wg.at[GLA_GATE_RANK:2 * GLA_GATE_RANK, GLA_DK:].set(gla_w_gate[0, 1])
    bg = gla_b_gate[0].reshape(1, 2 * GLA_DK)
    gq, gk, gv, gr, gg = _proj_gla(xa, mods, w_main, wz, wg, bg)
    of = _gla_scan(gq, gk, gv, gg, reverse=False)
    ob = _gla_scan(gq, gk, gv, gg, reverse=True)
    x1, u, route, counts = _mixer_out("gla", [of, ob, gr], gla_w_out[0].astype(BF16), xa, mods,
                                      lnp[1][0], rw[1], rb[1], nbl, norm_w=gla_norm_w[0])
    yg = _moe_ffn(1, u.reshape(B * S, D), route.reshape(B * S, LANES), counts,
                  moe_w_gu, moe_b_gu, moe_w_down, moe_b_down)
    return _final_ln(x1, yg, route, mods, lnp[1][1], nbl)
```

```python
import functools
import math

import numpy as np
import jax
import jax.numpy as jnp
from jax import lax
from jax.experimental import pallas as pl
from jax.experimental.pallas import tpu as pltpu

F32 = jnp.float32
BF16 = jnp.bfloat16

D_MODEL = 1024
DEPTH = 2
GRID_W = 64

DA_HEADS = 8
DA_HEAD_DIM = 64
DA_HEAD_W = 2 * DA_HEAD_DIM
ROPE_BASE = 10000.0
ROPE_PAIRS_AXIS = DA_HEAD_DIM // 4

GLA_HEADS = 4
GLA_DK = D_MODEL // 2
GLA_DV = D_MODEL
GLA_DK_HEAD = GLA_DK // GLA_HEADS
GLA_DV_HEAD = GLA_DV // GLA_HEADS
GLA_GATE_RANK = 16
GLA_TAU = 16.0
GLA_CHUNK = 64

N_EXPERTS = 32
TOP_K = 4
SWIGLU_ALPHA = 1.702
SWIGLU_LIMIT = 7.0
MOE_BLOCK = 256

DEEPNORM_ALPHA = (2.0 * DEPTH) ** 0.25
NORM_EPS = 1e-5

LANES = 128
TM = 256
ATT_TK_MAX = 2816
VMEM_LIMIT = 48 * 1024 * 1024
MOE_VMEM_LIMIT = 56 * 1024 * 1024

MOD_SC1, MOD_SH1, MOD_G1, MOD_SC2, MOD_SH2, MOD_G2 = range(6)


def _cparams(n_axes):
    return pltpu.CompilerParams(dimension_semantics=("arbitrary",) * n_axes,
                                vmem_limit_bytes=VMEM_LIMIT)


def _lambda_init(layer_idx):
    return 0.8 - 0.6 * math.exp(-0.3 * layer_idx)


def _ada_kernel(c_ref, w_ref, b_ref, o_ref):
    c = c_ref[...]
    s = c * jax.nn.sigmoid(c)
    o_ref[0] = jnp.dot(s, w_ref[0], preferred_element_type=F32) + b_ref[0]


def _ada_mods(cc, ada_w, ada_b):
    nt = 1536
    n6 = 6 * D_MODEL
    return pl.pallas_call(
        _ada_kernel,
        grid=(DEPTH, n6 // nt),
        in_specs=[pl.BlockSpec((8, D_MODEL), lambda l, j: (0, 0)),
                  pl.BlockSpec((1, D_MODEL, nt), lambda l, j: (l, 0, j)),
                  pl.BlockSpec((1, 1, nt), lambda l, j: (l, 0, j))],
        out_specs=pl.BlockSpec((1, 8, nt), lambda l, j: (l, 0, j)),
        out_shape=jax.ShapeDtypeStruct((DEPTH, 8, n6), F32),
        compiler_params=_cparams(2),
        name="ada_mods",
    )(cc, ada_w, ada_b.reshape(DEPTH, 1, n6))


def _proj_da_kernel(x_ref, mod_ref, cos_ref, sin_ref, wqk_ref, wv_ref, q_ref, k_ref, v_ref):
    m = mod_ref[0]
    t = (x_ref[0] * m[MOD_SC1:MOD_SC1 + 1] + m[MOD_SH1:MOD_SH1 + 1]).astype(BF16)
    cos = cos_ref[...]
    sin = sin_ref[...]
    q_scale = DA_HEAD_DIM ** -0.5 * math.log2(math.e)
    for j in range(DA_HEADS):
        y2 = jnp.dot(t, wqk_ref[:, j * 256:(j + 1) * 256], preferred_element_type=F32)
        for hh in range(2):
            y = y2[:, hh * LANES:(hh + 1) * LANES]
            y = y * cos + pltpu.roll(y, 64, 1) * sin
            col = (2 * j + hh) * LANES
            if col < D_MODEL:
                q_ref[0, :, col:col + LANES] = (y * q_scale).astype(BF16)
            else:
                k_ref[0, :, col - D_MODEL:col - D_MODEL + LANES] = y.astype(BF16)
    v_ref[0] = jnp.dot(t, wv_ref[...], preferred_element_type=F32).astype(BF16)


def _proj_da(xa, mods, cos, sin, wqk, wv):
    B, LT, D = xa.shape
    nb = LT // TM
    nbl = nb - 1
    out = jax.ShapeDtypeStruct((B, LT, D), BF16)
    blk = pl.BlockSpec((1, TM, D), lambda b, i: (b, i, 0))
    return pl.pallas_call(
        _proj_da_kernel,
        grid=(B, nb),
        in_specs=[blk,
                  pl.BlockSpec((1, 8, D), lambda b, i: (2 * b + i // nbl, 0, 0)),
                  pl.BlockSpec((TM, LANES), lambda b, i: (i, 0)),
                  pl.BlockSpec((TM, LANES), lambda b, i: (i, 0)),
                  pl.BlockSpec((D, 2 * D), lambda b, i: (0, 0)),
                  pl.BlockSpec((D, D), lambda b, i: (0, 0))],
        out_specs=[blk, blk, blk],
        out_shape=[out, out, out],
        compiler_params=_cparams(2),
        name="da_proj",
    )(xa, mods, cos, sin, wqk, wv)


def _attn_kernel(lam_ref, q_ref, k_ref, v_ref, sw_ref, o_ref, vext_sc, m_sc, acc_sc, s_sc, *,
                 n_lat, n_ctx, lam_init):
    i = pl.program_id(2)
    tq = q_ref.shape[1]

    @pl.when(i == 0)
    def _():
        vext_sc[:, :DA_HEAD_W] = v_ref[0]
        vext_sc[:, DA_HEAD_W:] = jnp.ones((vext_sc.shape[0], DA_HEAD_W), BF16)

    q = q_ref[0]
    lane = lax.broadcasted_iota(jnp.int32, (1, DA_HEAD_W), 1)
    map0 = (lane % 64) < 32
    zero = jnp.zeros_like(q)
    qs = jnp.concatenate([jnp.where(map0, q, zero), jnp.where(map0, zero, q)], axis=0)
    m_sc[...] = jnp.full(m_sc.shape, -jnp.inf, F32)
    acc_sc[...] = jnp.zeros(acc_sc.shape, F32)

    def scores(off, tk):
        k = k_ref[0, pl.ds(off, tk), :]
        return lax.dot_general(qs, k, (((1,), (1,)), ((), ())), preferred_element_type=F32)

    def accumulate(s, off, tk):
        m_prev = m_sc[...]
        m_new = jnp.maximum(m_prev, jnp.max(s, axis=1, keepdims=True))
        alpha = jnp.exp2(m_prev - m_new)
        p = jnp.exp2(s - jnp.tile(m_new, (1, tk // LANES)))
        pv = jnp.dot(p.astype(BF16), vext_sc[pl.ds(off, tk), :], preferred_element_type=F32)
        acc_sc[...] = jnp.tile(alpha, (1, 2)) * acc_sc[...] + pv
        m_sc[...] = m_new

    n_q_lat = n_lat // tq
    tk = s_sc.shape[2]
    n_steps = (n_lat + n_ctx) // tk

    @pl.when(i < n_q_lat)
    def _():
        s_sc[0] = scores(0, tk)
        for t in range(n_steps):
            if t + 1 < n_steps:
                s_sc[(t + 1) % 2] = scores((t + 1) * tk, tk)
            accumulate(s_sc[t % 2], t * tk, tk)

    @pl.when(i >= n_q_lat)
    def _():
        accumulate(scores(n_lat, n_ctx), n_lat, n_ctx)

    acc = acc_sc[...]
    o0 = acc[:tq, :DA_HEAD_W] / acc[:tq, DA_HEAD_W:DA_HEAD_W + 1]
    o1 = acc[tq:, :DA_HEAD_W] / acc[tq:, DA_HEAD_W:DA_HEAD_W + 1]
    o = o0 - lam_ref[0] * o1
    o = o * lax.rsqrt(jnp.mean(o * o, axis=-1, keepdims=True) + NORM_EPS)
    o_ref[0] = (o * sw_ref[...] * (1.0 - lam_init)).astype(BF16)


def _diff_attention(lam, q, k, v, subln_w, lam_init):
    B, LT, D = q.shape
    nb = LT // TM
    tk = max(t for t in range(TM, ATT_TK_MAX + 1, TM) if LT % t == 0)
    kern = functools.partial(_attn_kernel, n_lat=LT - TM, n_ctx=TM, lam_init=lam_init)
    grid_spec = pltpu.PrefetchScalarGridSpec(
        num_scalar_prefetch=1,
        grid=(B, DA_HEADS, nb),
        in_specs=[pl.BlockSpec((1, TM, DA_HEAD_W), lambda b, h, i, lam: (b, i, h)),
                  pl.BlockSpec((1, LT, DA_HEAD_W), lambda b, h, i, lam: (b, 0, h)),
                  pl.BlockSpec((1, LT, DA_HEAD_W), lambda b, h, i, lam: (b, 0, h)),
                  pl.BlockSpec((1, DA_HEAD_W), lambda b, h, i, lam: (0, 0))],
        out_specs=pl.BlockSpec((1, TM, DA_HEAD_W), lambda b, h, i, lam: (b, i, h)),
        scratch_shapes=[pltpu.VMEM((LT, 2 * DA_HEAD_W), BF16),
                        pltpu.VMEM((2 * TM, LANES), F32),
                        pltpu.VMEM((2 * TM, 2 * DA_HEAD_W), F32),
                        pltpu.VMEM((2, 2 * TM, tk), F32)],
    )
    return pl.pallas_call(
        kern,
        grid_spec=grid_spec,
        out_shape=jax.ShapeDtypeStruct((B, LT, D), BF16),
        compiler_params=_cparams(3),
        name="diff_attn",
    )(lam, q, k, v, subln_w.reshape(1, DA_HEAD_W))


def _route_block(logits, cnt_sc):
    lane = lax.broadcasted_iota(jnp.int32, logits.shape, 1)
    lane_f = lane.astype(F32)
    work = jnp.where(lane < N_EXPERTS, logits, -jnp.inf)
    tops, idxs, hits = [], [], []
    for _ in range(TOP_K):
        mk = jnp.max(work, axis=1, keepdims=True)
        ik = jnp.min(jnp.where(work == mk, lane_f, float(LANES)), axis=1, keepdims=True)
        hit = lane_f == ik
        tops.append(mk)
        idxs.append(ik)
        hits.append(hit)
        work = jnp.where(hit, -jnp.inf, work)
    chosen = functools.reduce(jnp.logical_or, hits).astype(F32)
    n = logits.shape[0]
    row = lax.broadcasted_iota(jnp.int32, (n, n), 0)
    col = lax.broadcasted_iota(jnp.int32, (n, n), 1)
    before = jnp.dot((col < row).astype(BF16), chosen.astype(BF16), preferred_element_type=F32)
    rank_all = cnt_sc[...] + before
    cnt_sc[...] = cnt_sc[...] + jnp.sum(chosen, axis=0, keepdims=True)
    exps = [jnp.exp(t - tops[0]) for t in tops]
    denom = functools.reduce(jnp.add, exps)
    table = jnp.zeros(logits.shape, F32)
    for k in range(TOP_K):
        rk = jnp.sum(jnp.where(hits[k], rank_all, 0.0), axis=1, keepdims=True)
        table = jnp.where(lane == k, idxs[k], table)
        table = jnp.where(lane == TOP_K + k, rk, table)
        table = jnp.where(lane == 2 * TOP_K + k, exps[k] / denom, table)
    return table


def _post_mixer(pre, w_ref, x_ref, mod_ref, ln_ref, rw_ref, rb_ref, x1_ref, u_ref, rt_ref, cnt_ref,
                cnt_sc):
    @pl.when((pl.program_id(0) == 0) & (pl.program_id(1) == 0))
    def _():
        cnt_sc[...] = jnp.zeros(cnt_sc.shape, F32)

    m = mod_ref[0]
    y = jnp.dot(pre, w_ref[...], preferred_element_type=F32)
    z = DEEPNORM_ALPHA * x_ref[0] + m[MOD_G1:MOD_G1 + 1] * y
    mu = jnp.mean(z, axis=-1, keepdims=True)
    zc = z - mu
    x1 = zc * lax.rsqrt(jnp.mean(zc * zc, axis=-1, keepdims=True) + NORM_EPS)
    x1 = x1 * ln_ref[0:1] + ln_ref[1:2]
    x1_ref[0] = x1
    u = x1 * m[MOD_SC2:MOD_SC2 + 1] + m[MOD_SH2:MOD_SH2 + 1]
    u_ref[0] = u
    u_hi = u.astype(BF16)
    u_lo = (u - u_hi.astype(F32)).astype(BF16)
    d_hi = jnp.dot(u_hi, rw_ref[...], preferred_element_type=F32)
    d_lo = jnp.dot(u_lo, rw_ref[:, :LANES], preferred_element_type=F32)
    logits = d_hi[:, :LANES] + d_hi[:, LANES:] + d_lo + rb_ref[...]
    rt_ref[0] = _route_block(logits, cnt_sc)
    cnt_ref[...] = jnp.broadcast_to(cnt_sc[...], cnt_ref.shape)


def _out_da_kernel(o_ref, w_ref, x_ref, mod_ref, ln_ref, rw_ref, rb_ref, x1_ref, u_ref, rt_ref,
                   cnt_ref, cnt_sc):
    _post_mixer(o_ref[0], w_ref, x_ref, mod_ref, ln_ref, rw_ref, rb_ref, x1_ref, u_ref, rt_ref,
                cnt_ref, cnt_sc)


def _out_gla_kernel(of_ref, ob_ref, r_ref, nw_ref, w_ref, x_ref, mod_ref, ln_ref, rw_ref, rb_ref,
                    x1_ref, u_ref, rt_ref, cnt_ref, cnt_sc):
    parts = []
    for h in range(GLA_HEADS):
        sl = slice(h * GLA_DV_HEAD, (h + 1) * GLA_DV_HEAD)
        o = of_ref[0, :, sl] + ob_ref[0, :, sl]
        o = o * lax.rsqrt(jnp.mean(o * o, axis=-1, keepdims=True) + NORM_EPS) * nw_ref[...]
        r = r_ref[0, :, sl]
        parts.append((o * (r * jax.nn.sigmoid(r))).astype(BF16))
    pre = jnp.concatenate(parts, axis=1)
    _post_mixer(pre, w_ref, x_ref, mod_ref, ln_ref, rw_ref, rb_ref, x1_ref, u_ref, rt_ref, cnt_ref,
                cnt_sc)


def _mixer_out(kind, acts, w_out, xa, mods, lnp, rw, rb, nb_out, norm_w=None):
    B, LT, D = xa.shape
    nbl = LT // TM - 1
    blk = pl.BlockSpec((1, TM, D), lambda b, i: (b, i, 0))
    common_specs = [pl.BlockSpec((D, D), lambda b, i: (0, 0)),
                    blk,
                    pl.BlockSpec((1, 8, D), lambda b, i: (2 * b + i // nbl, 0, 0)),
                    pl.BlockSpec((2, D), lambda b, i: (0, 0)),
                    pl.BlockSpec((D, 2 * LANES), lambda b, i: (0, 0)),
                    pl.BlockSpec((1, LANES), lambda b, i: (0, 0))]
    lout = nb_out * TM
    out_shape = [jax.ShapeDtypeStruct((B, lout, D), F32),
                 jax.ShapeDtypeStruct((B, lout, D), F32),
                 jax.ShapeDtypeStruct((B, lout, LANES), F32),
                 jax.ShapeDtypeStruct((8, LANES), F32)]
    out_specs = [blk, blk, pl.BlockSpec((1, TM, LANES), lambda b, i: (b, i, 0)),
                 pl.BlockSpec((8, LANES), lambda b, i: (0, 0))]
    if kind == "da":
        kern = _out_da_kernel
        in_specs = [blk] + common_specs
        args = list(acts)
    else:
        kern = _out_gla_kernel
        in_specs = [blk, blk, blk, pl.BlockSpec((1, GLA_DV_HEAD), lambda b, i: (0, 0))] + common_specs
        args = list(acts) + [norm_w.reshape(1, GLA_DV_HEAD)]
    return pl.pallas_call(
        kern,
        grid=(B, nb_out),
        in_specs=in_specs,
        out_specs=out_specs,
        out_shape=out_shape,
        scratch_shapes=[pltpu.VMEM((1, LANES), F32)],
        compiler_params=_cparams(2),
        name="mixer_out_" + kind,
    )(*args, w_out, xa, mods, lnp, rw, rb)


def _proj_gla_kernel(x_ref, mod_ref, w_ref, wz_ref, wg_ref, bg_ref,
                     q_ref, k_ref, v_ref, r_ref, g_ref):
    m = mod_ref[0]
    t = (x_ref[0] * m[MOD_SC1:MOD_SC1 + 1] + m[MOD_SH1:MOD_SH1 + 1]).astype(BF16)
    c0, c1, c2, c3 = GLA_DK, 2 * GLA_DK, 2 * GLA_DK + GLA_DV, 2 * GLA_DK + 2 * GLA_DV
    q_ref[0] = jnp.dot(t, w_ref[:, :c0], preferred_element_type=F32) * (GLA_DK_HEAD ** -0.5)
    k_ref[0] = jnp.dot(t, w_ref[:, c0:c1], preferred_element_type=F32)
    v_ref[0] = jnp.dot(t, w_ref[:, c1:c2], preferred_element_type=F32).astype(BF16)
    r_ref[0] = jnp.dot(t, w_ref[:, c2:c3], preferred_element_type=F32)
    z = jnp.dot(t, wz_ref[...], preferred_element_type=F32)
    gl = jnp.dot(z, wg_ref[...], preferred_element_type=F32) + bg_ref[...]
    log_sig = jnp.minimum(gl, 0.0) - jnp.log1p(jnp.exp(-jnp.abs(gl)))
    g_ref[0] = log_sig * (1.0 / GLA_TAU)


def _proj_gla(xa, mods, w_main, wz, wg, bg):
    B, LT, D = xa.shape
    nb = LT // TM
    nbl = nb - 1
    blk = lambda w: pl.BlockSpec((1, TM, w), lambda b, i: (b, i, 0))
    return pl.pallas_call(
        _proj_gla_kernel,
        grid=(B, nb),
        in_specs=[blk(D),
                  pl.BlockSpec((1, 8, D), lambda b, i: (2 * b + i // nbl, 0, 0)),
                  pl.BlockSpec(w_main.shape, lambda b, i: (0, 0)),
                  pl.BlockSpec(wz.shape, lambda b, i: (0, 0)),
                  pl.BlockSpec(wg.shape, lambda b, i: (0, 0)),
                  pl.BlockSpec(bg.shape, lambda b, i: (0, 0))],
        out_specs=[blk(GLA_DK), blk(GLA_DK), blk(GLA_DV), blk(GLA_DV), blk(2 * GLA_DK)],
        out_shape=[jax.ShapeDtypeStruct((B, LT, GLA_DK), F32),
                   jax.ShapeDtypeStruct((B, LT, GLA_DK), F32),
                   jax.ShapeDtypeStruct((B, LT, GLA_DV), BF16),
                   jax.ShapeDtypeStruct((B, LT, GLA_DV), F32),
                   jax.ShapeDtypeStruct((B, LT, 2 * GLA_DK), F32)],
        compiler_params=_cparams(2),
        name="gla_proj",
    )(xa, mods, w_main, wz, wg, bg)


def _gla_scan_kernel(q_ref, k_ref, v_ref, g_ref, o_ref, st_sc, *, reverse):
    j = pl.program_id(1)

    @pl.when(j == 0)
    def _():
        st_sc[...] = jnp.zeros(st_sc.shape, F32)

    C = GLA_CHUNK
    n_chunks = TM // C

    def causal(n):
        row = lax.broadcasted_iota(jnp.int32, (n, n), 0)
        col = lax.broadcasted_iota(jnp.int32, (n, n), 1)
        return (row // C == col // C) & ((col >= row) if reverse else (col <= row))

    keep = causal(C)
    tri = causal(TM).astype(BF16)
    g = g_ref[0]
    g_hi = g.astype(BF16)
    rem = g - g_hi.astype(F32)
    g_mid = rem.astype(BF16)
    g_lo = (rem - g_mid.astype(F32)).astype(BF16)
    b_all = (jnp.dot(tri, g_hi, preferred_element_type=F32)
             + jnp.dot(tri, g_mid, preferred_element_type=F32)
             + jnp.dot(tri, g_lo, preferred_element_type=F32))

    states = [st_sc[h] for h in range(GLA_HEADS)]
    order = range(n_chunks - 1, -1, -1) if reverse else range(n_chunks)
    for c in order:
        rows = slice(c * C, (c + 1) * C)
        for h in range(GLA_HEADS):
            ks = slice(h * GLA_DK_HEAD, (h + 1) * GLA_DK_HEAD)
            vs = slice(h * GLA_DV_HEAD, (h + 1) * GLA_DV_HEAD)
            b = b_all[rows, ks]
            tot = b[0:1] if reverse else b[C - 1:C]
            q = q_ref[0, rows, ks]
            k = k_ref[0, rows, ks]
            q_in = (q * jnp.exp(b)).astype(BF16)
            k_in = (k * jnp.exp(-b)).astype(BF16)
            k_st = (k * jnp.exp(tot - b)).astype(BF16)
            att = lax.dot_general(q_in, k_in, (((1,), (1,)), ((), ())), preferred_element_type=F32)
            att = jnp.where(keep, att, 0.0).astype(BF16)
            v = v_ref[0, rows, vs]
            st = states[h]
            o = jnp.dot(att, v, preferred_element_type=F32)
            o = o + lax.dot_general(q_in, st.astype(BF16), (((1,), (1,)), ((), ())),
                                    preferred_element_type=F32)
            o_ref[0, rows, vs] = o
            ds = lax.dot_general(v, k_st, (((0,), (0,)), ((), ())), preferred_element_type=F32)
            states[h] = st * jnp.exp(tot) + ds
    for h in range(GLA_HEADS):
        st_sc[h] = states[h]


def _gla_scan(q, k, v, g, reverse):
    B, LT, _ = q.shape
    nb = LT // TM
    ctx_blk = nb - 1
    if reverse:
        order = lambda j: jnp.where(j == 0, ctx_blk, ctx_blk - j)
    else:
        order = lambda j: jnp.where(j == 0, ctx_blk, j - 1)
    gcol = 1 if reverse else 0
    return pl.pallas_call(
        functools.partial(_gla_scan_kernel, reverse=reverse),
        grid=(B, nb),
        in_specs=[pl.BlockSpec((1, TM, GLA_DK), lambda b, j: (b, order(j), 0)),
                  pl.BlockSpec((1, TM, GLA_DK), lambda b, j: (b, order(j), 0)),
                  pl.BlockSpec((1, TM, GLA_DV), lambda b, j: (b, order(j), 0)),
                  pl.BlockSpec((1, TM, GLA_DK), lambda b, j: (b, order(j), gcol))],
        out_specs=pl.BlockSpec((1, TM, GLA_DV), lambda b, j: (b, order(j), 0)),
        out_shape=jax.ShapeDtypeStruct((B, LT, GLA_DV), F32),
        scratch_shapes=[pltpu.VMEM((GLA_HEADS, GLA_DV_HEAD, GLA_DK_HEAD), F32)],
        compiler_params=_cparams(2),
        name="gla_scan_bwd" if reverse else "gla_scan_fwd",
    )(q, k, v, g)


def _moe_kernel(pe_ref, nu_ref, tok_ref, tokn_ref, u_hbm, wgu_ref, bgu_ref, wd_ref, bd_ref, y_ref,
                wgu_sc, wd_sc, x0, x1, x2, x3, gsem):
    j = pl.program_id(0)
    n_used = nu_ref[0]
    H = MOE_BLOCK
    xbuf = (x0, x1, x2, x3)

    def gather(tab_ref, half, buf):
        for r in range(H):
            t = tab_ref[0, 0, half * H + r]
            pltpu.make_async_copy(u_hbm.at[pl.ds(t, 1), :], xbuf[buf].at[pl.ds(r, 1), :],
                                  gsem.at[buf]).start(priority=r % 2)

    def gather_wait(buf):
        pltpu.make_async_copy(u_hbm.at[pl.ds(0, H), :], xbuf[buf], gsem.at[buf]).wait()

    def ffn(half, buf):
        gu = jnp.dot(xbuf[buf][...].astype(BF16), wgu_sc[...], preferred_element_type=F32)
        gu = gu + bgu_ref[0, 0]
        mid = gu.shape[1] // 2
        glu = jnp.minimum(gu[:, :mid], SWIGLU_LIMIT)
        lin = jnp.clip(gu[:, mid:], -SWIGLU_LIMIT, SWIGLU_LIMIT)
        act = glu * jax.nn.sigmoid(SWIGLU_ALPHA * glu) * (lin + 1.0)
        y = jnp.dot(act.astype(BF16), wd_sc[...], preferred_element_type=F32) + bd_ref[0, 0]
        y_ref[half * H:(half + 1) * H, :] = y.astype(y_ref.dtype)

    def pair_step(cur, nxt):
        gather_wait(cur)
        gather(tokn_ref, 0, nxt)
        ffn(0, cur)
        gather_wait(cur + 1)
        gather(tokn_ref, 1, nxt + 1)
        ffn(1, cur + 1)

        @pl.when(j == n_used - 1)
        def _():
            gather_wait(nxt)
            gather_wait(nxt + 1)

    @pl.when(j < n_used)
    def _():
        @pl.when(j == 0)
        def _():
            gather(tok_ref, 0, 0)
            gather(tok_ref, 1, 1)

        @pl.when((j == 0) | (pe_ref[j] != pe_ref[jnp.maximum(j - 1, 0)]))
        def _():
            wgu_sc[...] = wgu_ref[0, 0].astype(BF16)
            wd_sc[...] = wd_ref[0, 0].astype(BF16)

        @pl.when(j % 2 == 0)
        def _():
            pair_step(0, 2)

        @pl.when(j % 2 == 1)
        def _():
            pair_step(2, 0)

    @pl.when(j >= n_used)
    def _():
        y_ref[...] = jnp.zeros(y_ref.shape, y_ref.dtype)


def _moe_experts(layer, pair_expert, n_used, tok_tab, u, w_gu, b_gu, w_down, b_down):
    n_pairs = tok_tab.shape[0]
    D = u.shape[1]
    pair = 2 * MOE_BLOCK
    tab = lambda f: pl.BlockSpec((1, 1, pair), f, memory_space=pltpu.SMEM)
    grid_spec = pltpu.PrefetchScalarGridSpec(
        num_scalar_prefetch=2,
        grid=(n_pairs,),
        in_specs=[tab(lambda j, pe, nu: (j, 0, 0)),
                  tab(lambda j, pe, nu: (jnp.minimum(j + 1, n_pairs - 1), 0, 0)),
                  pl.BlockSpec(memory_space=pl.ANY),
                  pl.BlockSpec((1, 1, D, 2 * D), lambda j, pe, nu: (layer, pe[j], 0, 0)),
                  pl.BlockSpec((1, 1, 1, 2 * D), lambda j, pe, nu: (layer, pe[j], 0, 0)),
                  pl.BlockSpec((1, 1, D, D), lambda j, pe, nu: (layer, pe[j], 0, 0)),
                  pl.BlockSpec((1, 1, 1, D), lambda j, pe, nu: (layer, pe[j], 0, 0))],
        out_specs=pl.BlockSpec((pair, D), lambda j, pe, nu: (j, 0)),
        scratch_shapes=[pltpu.VMEM((D, 2 * D), BF16), pltpu.VMEM((D, D), BF16),
                        *([pltpu.VMEM((MOE_BLOCK, D), F32)] * 4),
                        pltpu.SemaphoreType.DMA((4,))],
    )
    return pl.pallas_call(
        _moe_kernel,
        grid_spec=grid_spec,
        out_shape=jax.ShapeDtypeStruct((n_pairs * pair, D), BF16),
        compiler_params=pltpu.CompilerParams(dimension_semantics=("arbitrary",),
                                             vmem_limit_bytes=MOE_VMEM_LIMIT),
        name="moe_experts",
    )(pair_expert, n_used, tok_tab, tok_tab, u, w_gu, b_gu.reshape(DEPTH, N_EXPERTS, 1, 2 * D),
      w_down, b_down.reshape(DEPTH, N_EXPERTS, 1, D))


def _moe_ffn(layer, u, route, counts, w_gu, b_gu, w_down, b_down):
    T, D = u.shape
    pair = 2 * MOE_BLOCK
    n_assign = T * TOP_K
    expert = route[:, :TOP_K].astype(jnp.int32)
    rank = route[:, TOP_K:2 * TOP_K].astype(jnp.int32)
    cnt = counts[0, :N_EXPERTS].astype(jnp.int32)
    padded = (cnt + pair - 1) // pair * pair
    padded_end = jnp.cumsum(padded)
    padded_start = padded_end - padded
    pos = (padded_start[expert] + rank).T
    n_pairs = -(-n_assign // pair) + N_EXPERTS
    n_rows = n_pairs * pair
    pair_start = jnp.arange(n_pairs, dtype=jnp.int32) * pair
    pair_expert = jnp.minimum(jnp.sum(padded_end[None, :] <= pair_start[:, None], axis=1),
                              N_EXPERTS - 1).astype(jnp.int32)
    n_used = (padded_end[-1:] // pair).astype(jnp.int32)
    tok = jnp.tile(jnp.arange(T, dtype=jnp.int32), TOP_K)
    _, sorted_tok = lax.sort((pos.reshape(-1), tok), num_keys=1)
    row = jnp.arange(n_rows, dtype=jnp.int32)
    row_e = jnp.repeat(pair_expert, pair)
    row_rank = row - padded_start[row_e]
    compact = jnp.cumsum(cnt)[row_e] - cnt[row_e] + row_rank
    tok_tab = jnp.where(row_rank < cnt[row_e], sorted_tok[jnp.minimum(compact, n_assign - 1)], 0)
    yb = _moe_experts(layer, pair_expert, n_used, tok_tab.reshape(n_pairs, 1, pair), u, w_gu, b_gu,
                      w_down, b_down)
    return yb[pos.reshape(-1)].reshape(TOP_K, T, D)


def _final_ln_kernel(x_ref, y_ref, rt_ref, mod_ref, ln_ref, o_ref):
    m = mod_ref[0]
    rt = rt_ref[0]
    f = rt[:, 2 * TOP_K:2 * TOP_K + 1] * y_ref[0].astype(F32)
    for k in range(1, TOP_K):
        f = f + rt[:, 2 * TOP_K + k:2 * TOP_K + k + 1] * y_ref[k].astype(F32)
    z = DEEPNORM_ALPHA * x_ref[0] + m[MOD_G2:MOD_G2 + 1] * f
    mu = jnp.mean(z, axis=-1, keepdims=True)
    zc = z - mu
    y = zc * lax.rsqrt(jnp.mean(zc * zc, axis=-1, keepdims=True) + NORM_EPS)
    o_ref[0] = y * ln_ref[0:1] + ln_ref[1:2]


def _final_ln(x1, yg, route, mods, lnp, nbl):
    B, L, D = x1.shape
    nblk = L // TM
    blk = pl.BlockSpec((1, TM, D), lambda b, i: (b, i, 0))
    return pl.pallas_call(
        _final_ln_kernel,
        grid=(B, nblk),
        in_specs=[blk,
                  pl.BlockSpec((TOP_K, TM, D), lambda b, i: (0, b * nblk + i, 0)),
                  pl.BlockSpec((1, TM, LANES), lambda b, i: (b, i, 0)),
                  pl.BlockSpec((1, 8, D), lambda b, i: (2 * b + i // nbl, 0, 0)),
                  pl.BlockSpec((2, D), lambda b, i: (0, 0))],
        out_specs=blk,
        out_shape=jax.ShapeDtypeStruct((B, L, D), F32),
        compiler_params=_cparams(2),
        name="final_ln",
    )(x1, yg, route, mods, lnp)


def _rope_tables(S, n_ctx):
    rows = S // GRID_W
    row = jnp.repeat(jnp.arange(rows), GRID_W).astype(F32)
    col = jnp.tile(jnp.arange(GRID_W), rows).astype(F32)
    inv = ROPE_BASE ** (-jnp.arange(ROPE_PAIRS_AXIS, dtype=F32) / ROPE_PAIRS_AXIS)
    ang = jnp.concatenate([row[:, None] * inv, col[:, None] * inv], -1)
    cos, sin = jnp.cos(ang), jnp.sin(ang)
    cos = jnp.concatenate([cos, jnp.ones((n_ctx, cos.shape[1]), F32)], axis=0)
    sin = jnp.concatenate([sin, jnp.zeros((n_ctx, sin.shape[1]), F32)], axis=0)
    return (jnp.concatenate([cos, cos, cos, cos], axis=1),
            jnp.concatenate([-sin, -sin, sin, sin], axis=1))


def _qk_column_perm():
    lane = np.arange(DA_HEAD_W)
    half, mp, jj = lane // 64, (lane % 64) // 32, lane % 32
    src = mp * DA_HEAD_DIM + half * 32 + jj
    head = np.arange(DA_HEADS)[:, None] * DA_HEAD_W
    perm = (head + src[None, :]).reshape(-1)
    return np.concatenate([perm, D_MODEL + perm])


def _split_router_w(w):
    w = jnp.pad(w, ((0, 0), (0, LANES - N_EXPERTS)))
    hi = w.astype(BF16)
    lo = (w - hi.astype(F32)).astype(BF16)
    return jnp.concatenate([hi, lo], axis=1)


def _layer_mods(mod_rows, B):
    D = D_MODEL
    parts = mod_rows.reshape(8, 6, D)
    sh1, sc1, g1, sh2, sc2, g2 = (parts[:, n] for n in range(6))
    tab = jnp.stack([1.0 + sc1, sh1, g1, 1.0 + sc2, sh2, g2, jnp.zeros_like(g1),
                     jnp.zeros_like(g1)], axis=1)
    lat = tab[:B]
    ctx = jnp.broadcast_to(tab[B:B + 1], (B, 8, D))
    return jnp.stack([lat, ctx], axis=1).reshape(2 * B, 8, D)


def kernel(x, c, ctx, c_ctx, ada_w, ada_b, ln_g, ln_b, da_w_in, da_w_out, da_lambda, da_subln_w,
           gla_w_in, gla_w_gate, gla_b_gate, gla_norm_w, gla_w_out, router_w, router_b,
           moe_w_gu, moe_b_gu, moe_w_down, moe_b_down):
    B, S, D = x.shape
    n_ctx = ctx.shape[1]
    assert D == D_MODEL and n_ctx == TM and S % TM == 0 and S % GRID_W == 0 and B + 1 <= 8
    nbl = S // TM
    nb = nbl + 1

    cc = jnp.concatenate([c, c_ctx[None, :], jnp.zeros((8 - B - 1, D), F32)], axis=0)
    mod_all = _ada_mods(cc, ada_w, ada_b)
    xa = jnp.concatenate([x, ctx], axis=1)

    rw = [_split_router_w(router_w[i]) for i in range(DEPTH)]
    rb = [jnp.pad(router_b[i], (0, LANES - N_EXPERTS)).reshape(1, LANES) for i in range(DEPTH)]
    lnp = [[jnp.stack([ln_g[i, n], ln_b[i, n]]) for n in range(2)] for i in range(DEPTH)]

    mods = _layer_mods(mod_all[0], B)
    w_in = da_w_in[0]
    wqk = w_in[:, _qk_column_perm()].astype(BF16)
    wv = w_in[:, 2 * D:].astype(BF16)
    cos, sin = _rope_tables(S, n_ctx)
    q, k, v = _proj_da(xa, mods, cos, sin, wqk, wv)
    lam_init = _lambda_init(0)
    lv = da_lambda[0].astype(F32)
    lam = (jnp.exp(jnp.sum(lv[0] * lv[1])) - jnp.exp(jnp.sum(lv[2] * lv[3])) + lam_init).reshape(1)
    o = _diff_attention(lam, q, k, v, da_subln_w[0], lam_init)
    x1, u, route, counts = _mixer_out("da", [o], da_w_out[0].astype(BF16), xa, mods, lnp[0][0],
                                      rw[0], rb[0], nb)
    yg = _moe_ffn(0, u.reshape(B * nb * TM, D), route.reshape(B * nb * TM, LANES), counts,
                  moe_w_gu, moe_b_gu, moe_w_down, moe_b_down)
    xa = _final_ln(x1, yg, route, mods, lnp[0][1], nbl)

    mods = _layer_mods(mod_all[1], B)
    gw = gla_w_in[0]
    c3 = 2 * GLA_DK + 2 * GLA_DV
    w_main = gw[:, :c3].astype(BF16)
    wz = jnp.pad(gw[:, c3:], ((0, 0), (0, LANES - 2 * GLA_GATE_RANK))).astype(BF16)
    wg = jnp.zeros((LANES, 2 * GLA_DK), F32)
    wg = wg.at[:GLA_GATE_RANK, :GLA_DK].set(gla_w_gate[0, 0])
    wg = wg.at[GLA_GATE_RANK:2 * GLA_GATE_RANK, GLA_DK:].set(gla_w_gate[0, 1])
    bg = gla_b_gate[0].reshape(1, 2 * GLA_DK)
    gq, gk, gv, gr, gg = _proj_gla(xa, mods, w_main, wz, wg, bg)
    of = _gla_scan(gq, gk, gv, gg, reverse=False)
    ob = _gla_scan(gq, gk, gv, gg, reverse=True)
    x1, u, route, counts = _mixer_out("gla", [of, ob, gr], gla_w_out[0].astype(BF16), xa, mods,
                                      lnp[1][0], rw[1], rb[1], nbl, norm_w=gla_norm_w[0])
    yg = _moe_ffn(1, u.reshape(B * S, D), route.reshape(B * S, LANES), counts,
                  moe_w_gu, moe_b_gu, moe_w_down, moe_b_down)
    return _final_ln(x1, yg, route, mods, lnp[1][1], nbl)
```

```python
import functools
import math

import numpy as np
import jax
import jax.numpy as jnp
from jax import lax
from jax.experimental import pallas as pl
from jax.experimental.pallas import tpu as pltpu

F32 = jnp.float32
BF16 = jnp.bfloat16

D_MODEL = 1024
DEPTH = 2
GRID_W = 64

DA_HEADS = 8
DA_HEAD_DIM = 64
DA_HEAD_W = 2 * DA_HEAD_DIM
ROPE_BASE = 10000.0
ROPE_PAIRS_AXIS = DA_HEAD_DIM // 4

GLA_HEADS = 4
GLA_DK = D_MODEL // 2
GLA_DV = D_MODEL
GLA_DK_HEAD = GLA_DK // GLA_HEADS
GLA_DV_HEAD = GLA_DV // GLA_HEADS
GLA_GATE_RANK = 16
GLA_TAU = 16.0
GLA_CHUNK = 64

N_EXPERTS = 32
TOP_K = 4
SWIGLU_ALPHA = 1.702
SWIGLU_LIMIT = 7.0
MOE_BLOCK = 256

DEEPNORM_ALPHA = (2.0 * DEPTH) ** 0.25
NORM_EPS = 1e-5

LANES = 128
TM = 256
ATT_TK_MAX = 2816
VMEM_LIMIT = 48 * 1024 * 1024
MOE_VMEM_LIMIT = 56 * 1024 * 1024

MOD_SC1, MOD_SH1, MOD_G1, MOD_SC2, MOD_SH2, MOD_G2 = range(6)


def _cparams(n_axes):
    return pltpu.CompilerParams(dimension_semantics=("arbitrary",) * n_axes,
                                vmem_limit_bytes=VMEM_LIMIT)


def _lambda_init(layer_idx):
    return 0.8 - 0.6 * math.exp(-0.3 * layer_idx)


def _ada_kernel(c_ref, w_ref, b_ref, o_ref):
    c = c_ref[...]
    s = c * jax.nn.sigmoid(c)
    o_ref[0] = jnp.dot(s, w_ref[0], preferred_element_type=F32) + b_ref[0]


def _ada_mods(cc, ada_w, ada_b):
    nt = 1536
    n6 = 6 * D_MODEL
    return pl.pallas_call(
        _ada_kernel,
        grid=(DEPTH, n6 // nt),
        in_specs=[pl.BlockSpec((8, D_MODEL), lambda l, j: (0, 0)),
                  pl.BlockSpec((1, D_MODEL, nt), lambda l, j: (l, 0, j)),
                  pl.BlockSpec((1, 1, nt), lambda l, j: (l, 0, j))],
        out_specs=pl.BlockSpec((1, 8, nt), lambda l, j: (l, 0, j)),
        out_shape=jax.ShapeDtypeStruct((DEPTH, 8, n6), F32),
        compiler_params=_cparams(2),
        name="ada_mods",
    )(cc, ada_w, ada_b.reshape(DEPTH, 1, n6))


def _proj_da_kernel(x_ref, mod_ref, cos_ref, sin_ref, wqk_ref, wv_ref, q_ref, k_ref, v_ref):
    m = mod_ref[0]
    t = (x_ref[0] * m[MOD_SC1:MOD_SC1 + 1] + m[MOD_SH1:MOD_SH1 + 1]).astype(BF16)
    cos = cos_ref[...]
    sin = sin_ref[...]
    q_scale = DA_HEAD_DIM ** -0.5 * math.log2(math.e)
    for j in range(DA_HEADS):
        y2 = jnp.dot(t, wqk_ref[:, j * 256:(j + 1) * 256], preferred_element_type=F32)
        for hh in range(2):
            y = y2[:, hh * LANES:(hh + 1) * LANES]
            y = y * cos + pltpu.roll(y, 64, 1) * sin
            col = (2 * j + hh) * LANES
            if col < D_MODEL:
                q_ref[0, :, col:col + LANES] = (y * q_scale).astype(BF16)
            else:
                k_ref[0, :, col - D_MODEL:col - D_MODEL + LANES] = y.astype(BF16)
    v_ref[0] = jnp.dot(t, wv_ref[...], preferred_element_type=F32).astype(BF16)


def _proj_da(xa, mods, cos, sin, wqk, wv):
    B, LT, D = xa.shape
    nb = LT // TM
    nbl = nb - 1
    out = jax.ShapeDtypeStruct((B, LT, D), BF16)
    blk = pl.BlockSpec((1, TM, D), lambda b, i: (b, i, 0))
    return pl.pallas_call(
        _proj_da_kernel,
        grid=(B, nb),
        in_specs=[blk,
                  pl.BlockSpec((1, 8, D), lambda b, i: (2 * b + i // nbl, 0, 0)),
                  pl.BlockSpec((TM, LANES), lambda b, i: (i, 0)),
                  pl.BlockSpec((TM, LANES), lambda b, i: (i, 0)),
                  pl.BlockSpec((D, 2 * D), lambda b, i: (0, 0)),
                  pl.BlockSpec((D, D), lambda b, i: (0, 0))],
        out_specs=[blk, blk, blk],
        out_shape=[out, out, out],
        compiler_params=_cparams(2),
        name="da_proj",
    )(xa, mods, cos, sin, wqk, wv)


def _attn_kernel(lam_ref, q_ref, k_ref, v_ref, sw_ref, o_ref, vext_sc, m_sc, acc_sc, s_sc, *,
                 n_lat, n_ctx, lam_init):
    i = pl.program_id(2)
    tq = q_ref.shape[1]

    @pl.when(i == 0)
    def _():
        vext_sc[:, :DA_HEAD_W] = v_ref[0]
        vext_sc[:, DA_HEAD_W:] = jnp.ones((vext_sc.shape[0], DA_HEAD_W), BF16)

    q = q_ref[0]
    lane = lax.broadcasted_iota(jnp.int32, (1, DA_HEAD_W), 1)
    map0 = (lane % 64) < 32
    zero = jnp.zeros_like(q)
    qs = jnp.concatenate([jnp.where(map0, q, zero), jnp.where(map0, zero, q)], axis=0)
    m_sc[...] = jnp.full(m_sc.shape, -jnp.inf, F32)
    acc_sc[...] = jnp.zeros(acc_sc.shape, F32)

    def scores(off, tk):
        k = k_ref[0, pl.ds(off, tk), :]
        return lax.dot_general(qs, k, (((1,), (1,)), ((), ())), preferred_element_type=F32)

    def accumulate(s, off, tk):
        m_prev = m_sc[...]
        m_new = jnp.maximum(m_prev, jnp.max(s, axis=1, keepdims=True))
        alpha = jnp.exp2(m_prev - m_new)
        p = jnp.exp2(s - jnp.tile(m_new, (1, tk // LANES)))
        pv = jnp.dot(p.astype(BF16), vext_sc[pl.ds(off, tk), :], preferred_element_type=F32)
        acc_sc[...] = jnp.tile(alpha, (1, 2)) * acc_sc[...] + pv
        m_sc[...] = m_new

    n_q_lat = n_lat // tq
    tk = s_sc.shape[2]
    n_steps = (n_lat + n_ctx) // tk

    @pl.when(i < n_q_lat)
    def _():
        s_sc[0] = scores(0, tk)
        for t in range(n_steps):
            if t + 1 < n_steps:
                s_sc[(t + 1) % 2] = scores((t + 1) * tk, tk)
            accumulate(s_sc[t % 2], t * tk, tk)

    @pl.when(i >= n_q_lat)
    def _():
        accumulate(scores(n_lat, n_ctx), n_lat, n_ctx)

    acc = acc_sc[...]
    o0 = acc[:tq, :DA_HEAD_W] / acc[:tq, DA_HEAD_W:DA_HEAD_W + 1]
    o1 = acc[tq:, :DA_HEAD_W] / acc[tq:, DA_HEAD_W:DA_HEAD_W + 1]
    o = o0 - lam_ref[0] * o1
    o = o * lax.rsqrt(jnp.mean(o * o, axis=-1, keepdims=True) + NORM_EPS)
    o_ref[0] = (o * sw_ref[...] * (1.0 - lam_init)).astype(BF16)


def _diff_attention(lam, q, k, v, subln_w, lam_init):
    B, LT, D = q.shape
    nb = LT // TM
    tk = max(t for t in range(TM, ATT_TK_MAX + 1, TM) if LT % t == 0)
    kern = functools.partial(_attn_kernel, n_lat=LT - TM, n_ctx=TM, lam_init=lam_init)
    grid_spec = pltpu.PrefetchScalarGridSpec(
        num_scalar_prefetch=1,
        grid=(B, DA_HEADS, nb),
        in_specs=[pl.BlockSpec((1, TM, DA_HEAD_W), lambda b, h, i, lam: (b, i, h)),
                  pl.BlockSpec((1, LT, DA_HEAD_W), lambda b, h, i, lam: (b, 0, h)),
                  pl.BlockSpec((1, LT, DA_HEAD_W), lambda b, h, i, lam: (b, 0, h)),
                  pl.BlockSpec((1, DA_HEAD_W), lambda b, h, i, lam: (0, 0))],
        out_specs=pl.BlockSpec((1, TM, DA_HEAD_W), lambda b, h, i, lam: (b, i, h)),
        scratch_shapes=[pltpu.VMEM((LT, 2 * DA_HEAD_W), BF16),
                        pltpu.VMEM((2 * TM, LANES), F32),
                        pltpu.VMEM((2 * TM, 2 * DA_HEAD_W), F32),
                        pltpu.VMEM((2, 2 * TM, tk), F32)],
    )
    return pl.pallas_call(
        kern,
        grid_spec=grid_spec,
        out_shape=jax.ShapeDtypeStruct((B, LT, D), BF16),
        compiler_params=_cparams(3),
        name="diff_attn",
    )(lam, q, k, v, subln_w.reshape(1, DA_HEAD_W))


def _route_block(logits, cnt_sc):
    lane = lax.broadcasted_iota(jnp.int32, logits.shape, 1)
    lane_f = lane.astype(F32)
    work = jnp.where(lane < N_EXPERTS, logits, -jnp.inf)
    tops, idxs, hits = [], [], []
    for _ in range(TOP_K):
        mk = jnp.max(work, axis=1, keepdims=True)
        ik = jnp.min(jnp.where(work == mk, lane_f, float(LANES)), axis=1, keepdims=True)
        hit = lane_f == ik
        tops.append(mk)
        idxs.append(ik)
        hits.append(hit)
        work = jnp.where(hit, -jnp.inf, work)
    chosen = functools.reduce(jnp.logical_or, hits).astype(F32)
    n = logits.shape[0]
    row = lax.broadcasted_iota(jnp.int32, (n, n), 0)
    col = lax.broadcasted_iota(jnp.int32, (n, n), 1)
    before = jnp.dot((col < row).astype(BF16), chosen.astype(BF16), preferred_element_type=F32)
    rank_all = cnt_sc[...] + before
    cnt_sc[...] = cnt_sc[...] + jnp.sum(chosen, axis=0, keepdims=True)
    exps = [jnp.exp(t - tops[0]) for t in tops]
    denom = functools.reduce(jnp.add, exps)
    table = jnp.zeros(logits.shape, F32)
    for k in range(TOP_K):
        rk = jnp.sum(jnp.where(hits[k], rank_all, 0.0), axis=1, keepdims=True)
        table = jnp.where(lane == k, idxs[k], table)
        table = jnp.where(lane == TOP_K + k, rk, table)
        table = jnp.where(lane == 2 * TOP_K + k, exps[k] / denom, table)
    return table


def _post_mixer(pre, w_ref, x_ref, mod_ref, ln_ref, rw_ref, rb_ref, x1_ref, u_ref, rt_ref, cnt_ref,
                cnt_sc):
    @pl.when((pl.program_id(0) == 0) & (pl.program_id(1) == 0))
    def _():
        cnt_sc[...] = jnp.zeros(cnt_sc.shape, F32)

    m = mod_ref[0]
    y = jnp.dot(pre, w_ref[...], preferred_element_type=F32)
    z = DEEPNORM_ALPHA * x_ref[0] + m[MOD_G1:MOD_G1 + 1] * y
    mu = jnp.mean(z, axis=-1, keepdims=True)
    zc = z - mu
    x1 = zc * lax.rsqrt(jnp.mean(zc * zc, axis=-1, keepdims=True) + NORM_EPS)
    x1 = x1 * ln_ref[0:1] + ln_ref[1:2]
    x1_ref[0] = x1
    u = x1 * m[MOD_SC2:MOD_SC2 + 1] + m[MOD_SH2:MOD_SH2 + 1]
    u_ref[0] = u
    u_hi = u.astype(BF16)
    u_lo = (u - u_hi.astype(F32)).astype(BF16)
    d_hi = jnp.dot(u_hi, rw_ref[...], preferred_element_type=F32)
    d_lo = jnp.dot(u_lo, rw_ref[:, :LANES], preferred_element_type=F32)
    logits = d_hi[:, :LANES] + d_hi[:, LANES:] + d_lo + rb_ref[...]
    rt_ref[0] = _route_block(logits, cnt_sc)
    cnt_ref[...] = jnp.broadcast_to(cnt_sc[...], cnt_ref.shape)


def _out_da_kernel(o_ref, w_ref, x_ref, mod_ref, ln_ref, rw_ref, rb_ref, x1_ref, u_ref, rt_ref,
                   cnt_ref, cnt_sc):
    _post_mixer(o_ref[0], w_ref, x_ref, mod_ref, ln_ref, rw_ref, rb_ref, x1_ref, u_ref, rt_ref,
                cnt_ref, cnt_sc)


def _out_gla_kernel(of_ref, ob_ref, r_ref, nw_ref, w_ref, x_ref, mod_ref, ln_ref, rw_ref, rb_ref,
                    x1_ref, u_ref, rt_ref, cnt_ref, cnt_sc):
    parts = []
    for h in range(GLA_HEADS):
        sl = slice(h * GLA_DV_HEAD, (h + 1) * GLA_DV_HEAD)
        o = of_ref[0, :, sl] + ob_ref[0, :, sl]
        o = o * lax.rsqrt(jnp.mean(o * o, axis=-1, keepdims=True) + NORM_EPS) * nw_ref[...]
        r = r_ref[0, :, sl]
        parts.append((o * (r * jax.nn.sigmoid(r))).astype(BF16))
    pre = jnp.concatenate(parts, axis=1)
    _post_mixer(pre, w_ref, x_ref, mod_ref, ln_ref, rw_ref, rb_ref, x1_ref, u_ref, rt_ref, cnt_ref,
                cnt_sc)


def _mixer_out(kind, acts, w_out, xa, mods, lnp, rw, rb, nb_out, norm_w=None):
    B, LT, D = xa.shape
    nbl = LT // TM - 1
    blk = pl.BlockSpec((1, TM, D), lambda b, i: (b, i, 0))
    common_specs = [pl.BlockSpec((D, D), lambda b, i: (0, 0)),
                    blk,
                    pl.BlockSpec((1, 8, D), lambda b, i: (2 * b + i // nbl, 0, 0)),
                    pl.BlockSpec((2, D), lambda b, i: (0, 0)),
                    pl.BlockSpec((D, 2 * LANES), lambda b, i: (0, 0)),
                    pl.BlockSpec((1, LANES), lambda b, i: (0, 0))]
    lout = nb_out * TM
    out_shape = [jax.ShapeDtypeStruct((B, lout, D), F32),
                 jax.ShapeDtypeStruct((B, lout, D), F32),
                 jax.ShapeDtypeStruct((B, lout, LANES), F32),
                 jax.ShapeDtypeStruct((8, LANES), F32)]
    out_specs = [blk, blk, pl.BlockSpec((1, TM, LANES), lambda b, i: (b, i, 0)),
                 pl.BlockSpec((8, LANES), lambda b, i: (0, 0))]
    if kind == "da":
        kern = _out_da_kernel
        in_specs = [blk] + common_specs
        args = list(acts)
    else:
        kern = _out_gla_kernel
        in_specs = [blk, blk, blk, pl.BlockSpec((1, GLA_DV_HEAD), lambda b, i: (0, 0))] + common_specs
        args = list(acts) + [norm_w.reshape(1, GLA_DV_HEAD)]
    return pl.pallas_call(
        kern,
        grid=(B, nb_out),
        in_specs=in_specs,
        out_specs=out_specs,
        out_shape=out_shape,
        scratch_shapes=[pltpu.VMEM((1, LANES), F32)],
        compiler_params=_cparams(2),
        name="mixer_out_" + kind,
    )(*args, w_out, xa, mods, lnp, rw, rb)


def _proj_gla_kernel(x_ref, mod_ref, w_ref, wz_ref, wg_ref, bg_ref,
                     q_ref, k_ref, v_ref, r_ref, g_ref):
    m = mod_ref[0]
    t = (x_ref[0] * m[MOD_SC1:MOD_SC1 + 1] + m[MOD_SH1:MOD_SH1 + 1]).astype(BF16)
    c0, c1, c2, c3 = GLA_DK, 2 * GLA_DK, 2 * GLA_DK + GLA_DV, 2 * GLA_DK + 2 * GLA_DV
    q_ref[0] = jnp.dot(t, w_ref[:, :c0], preferred_element_type=F32) * (GLA_DK_HEAD ** -0.5)
    k_ref[0] = jnp.dot(t, w_ref[:, c0:c1], preferred_element_type=F32)
    v_ref[0] = jnp.dot(t, w_ref[:, c1:c2], preferred_element_type=F32).astype(BF16)
    r_ref[0] = jnp.dot(t, w_ref[:, c2:c3], preferred_element_type=F32)
    z = jnp.dot(t, wz_ref[...], preferred_element_type=F32)
    gl = jnp.dot(z, wg_ref[...], preferred_element_type=F32) + bg_ref[...]
    log_sig = jnp.minimum(gl, 0.0) - jnp.log1p(jnp.exp(-jnp.abs(gl)))
    g_ref[0] = log_sig * (1.0 / GLA_TAU)


def _proj_gla(xa, mods, w_main, wz, wg, bg):
    B, LT, D = xa.shape
    nb = LT // TM
    nbl = nb - 1
    blk = lambda w: pl.BlockSpec((1, TM, w), lambda b, i: (b, i, 0))
    return pl.pallas_call(
        _proj_gla_kernel,
        grid=(B, nb),
        in_specs=[blk(D),
                  pl.BlockSpec((1, 8, D), lambda b, i: (2 * b + i // nbl, 0, 0)),
                  pl.BlockSpec(w_main.shape, lambda b, i: (0, 0)),
                  pl.BlockSpec(wz.shape, lambda b, i: (0, 0)),
                  pl.BlockSpec(wg.shape, lambda b, i: (0, 0)),
                  pl.BlockSpec(bg.shape, lambda b, i: (0, 0))],
        out_specs=[blk(GLA_DK), blk(GLA_DK), blk(GLA_DV), blk(GLA_DV), blk(2 * GLA_DK)],
        out_shape=[jax.ShapeDtypeStruct((B, LT, GLA_DK), F32),
                   jax.ShapeDtypeStruct((B, LT, GLA_DK), F32),
                   jax.ShapeDtypeStruct((B, LT, GLA_DV), BF16),
                   jax.ShapeDtypeStruct((B, LT, GLA_DV), F32),
                   jax.ShapeDtypeStruct((B, LT, 2 * GLA_DK), F32)],
        compiler_params=_cparams(2),
        name="gla_proj",
    )(xa, mods, w_main, wz, wg, bg)


def _gla_scan_kernel(q_ref, k_ref, v_ref, g_ref, o_ref, st_sc, *, reverse):
    j = pl.program_id(1)

    @pl.when(j == 0)
    def _():
        st_sc[...] = jnp.zeros(st_sc.shape, F32)

    C = GLA_CHUNK
    n_chunks = TM // C

    def causal(n):
        row = lax.broadcasted_iota(jnp.int32, (n, n), 0)
        col = lax.broadcasted_iota(jnp.int32, (n, n), 1)
        return (row // C == col // C) & ((col >= row) if reverse else (col <= row))

    keep = causal(C)
    tri = causal(TM).astype(BF16)
    g = g_ref[0]
    g_hi = g.astype(BF16)
    rem = g - g_hi.astype(F32)
    g_mid = rem.astype(BF16)
    g_lo = (rem - g_mid.astype(F32)).astype(BF16)
    b_all = (jnp.dot(tri, g_hi, preferred_element_type=F32)
             + jnp.dot(tri, g_mid, preferred_element_type=F32)
             + jnp.dot(tri, g_lo, preferred_element_type=F32))

    states = [st_sc[h] for h in range(GLA_HEADS)]
    order = range(n_chunks - 1, -1, -1) if reverse else range(n_chunks)
    for c in order:
        rows = slice(c * C, (c + 1) * C)
        for h in range(GLA_HEADS):
            ks = slice(h * GLA_DK_HEAD, (h + 1) * GLA_DK_HEAD)
            vs = slice(h * GLA_DV_HEAD, (h + 1) * GLA_DV_HEAD)
            b = b_all[rows, ks]
            tot = b[0:1] if reverse else b[C - 1:C]
            q = q_ref[0, rows, ks]
            k = k_ref[0, rows, ks]
            q_in = (q * jnp.exp(b)).astype(BF16)
            k_in = (k * jnp.exp(-b)).astype(BF16)
            k_st = (k * jnp.exp(tot - b)).astype(BF16)
            att = lax.dot_general(q_in, k_in, (((1,), (1,)), ((), ())), preferred_element_type=F32)
            att = jnp.where(keep, att, 0.0).astype(BF16)
            v = v_ref[0, rows, vs]
            st = states[h]
            o = jnp.dot(att, v, preferred_element_type=F32)
            o = o + lax.dot_general(q_in, st.astype(BF16), (((1,), (1,)), ((), ())),
                                    preferred_element_type=F32)
            o_ref[0, rows, vs] = o
            ds = lax.dot_general(v, k_st, (((0,), (0,)), ((), ())), preferred_element_type=F32)
            states[h] = st * jnp.exp(tot) + ds
    for h in range(GLA_HEADS):
        st_sc[h] = states[h]


def _gla_scan(q, k, v, g, reverse):
    B, LT, _ = q.shape
    nb = LT // TM
    ctx_blk = nb - 1
    if reverse:
        order = lambda j: jnp.where(j == 0, ctx_blk, ctx_blk - j)
    else:
        order = lambda j: jnp.where(j == 0, ctx_blk, j - 1)
    gcol = 1 if reverse else 0
    return pl.pallas_call(
        functools.partial(_gla_scan_kernel, reverse=reverse),
        grid=(B, nb),
        in_specs=[pl.BlockSpec((1, TM, GLA_DK), lambda b, j: (b, order(j), 0)),
                  pl.BlockSpec((1, TM, GLA_DK), lambda b, j: (b, order(j), 0)),
                  pl.BlockSpec((1, TM, GLA_DV), lambda b, j: (b, order(j), 0)),
                  pl.BlockSpec((1, TM, GLA_DK), lambda b, j: (b, order(j), gcol))],
        out_specs=pl.BlockSpec((1, TM, GLA_DV), lambda b, j: (b, order(j), 0)),
        out_shape=jax.ShapeDtypeStruct((B, LT, GLA_DV), F32),
        scratch_shapes=[pltpu.VMEM((GLA_HEADS, GLA_DV_HEAD, GLA_DK_HEAD), F32)],
        compiler_params=_cparams(2),
        name="gla_scan_bwd" if reverse else "gla_scan_fwd",
    )(q, k, v, g)


def _moe_kernel(pe_ref, nu_ref, tok_ref, tokn_ref, u_hbm, wgu_ref, bgu_ref, wd_ref, bd_ref, y_ref,
                wgu_sc, wd_sc, x0, x1, x2, x3, gsem):
    j = pl.program_id(0)
    n_used = nu_ref[0]
    H = MOE_BLOCK
    xbuf = (x0, x1, x2, x3)

    def gather(tab_ref, half, buf):
        for r in range(H):
            t = tab_ref[0, 0, half * H + r]
            pltpu.make_async_copy(u_hbm.at[pl.ds(t, 1), :], xbuf[buf].at[pl.ds(r, 1), :],
                                  gsem.at[buf]).start(priority=r % 2)

    def gather_wait(buf):
        pltpu.make_async_copy(u_hbm.at[pl.ds(0, H), :], xbuf[buf], gsem.at[buf]).wait()

    def ffn(half, buf):
        gu = jnp.dot(xbuf[buf][...].astype(BF16), wgu_sc[...], preferred_element_type=F32)
        gu = gu + bgu_ref[0, 0]
        mid = gu.shape[1] // 2
        glu = jnp.minimum(gu[:, :mid], SWIGLU_LIMIT)
        lin = jnp.clip(gu[:, mid:], -SWIGLU_LIMIT, SWIGLU_LIMIT)
        act = glu * jax.nn.sigmoid(SWIGLU_ALPHA * glu) * (lin + 1.0)
        y = jnp.dot(act.astype(BF16), wd_sc[...], preferred_element_type=F32) + bd_ref[0, 0]
        y_ref[half * H:(half + 1) * H, :] = y.astype(y_ref.dtype)

    def pair_step(cur, nxt):
        gather_wait(cur)
        gather(tokn_ref, 0, nxt)
        ffn(0, cur)
        gather_wait(cur + 1)
        gather(tokn_ref, 1, nxt + 1)
        ffn(1, cur + 1)

        @pl.when(j == n_used - 1)
        def _():
            gather_wait(nxt)
            gather_wait(nxt + 1)

    @pl.when(j < n_used)
    def _():
        @pl.when(j == 0)
        def _():
            gather(tok_ref, 0, 0)
            gather(tok_ref, 1, 1)

        @pl.when((j == 0) | (pe_ref[j] != pe_ref[jnp.maximum(j - 1, 0)]))
        def _():
            wgu_sc[...] = wgu_ref[0, 0].astype(BF16)
            wd_sc[...] = wd_ref[0, 0].astype(BF16)

        @pl.when(j % 2 == 0)
        def _():
            pair_step(0, 2)

        @pl.when(j % 2 == 1)
        def _():
            pair_step(2, 0)

    @pl.when(j >= n_used)
    def _():
        y_ref[...] = jnp.zeros(y_ref.shape, y_ref.dtype)


def _moe_experts(layer, pair_expert, n_used, tok_tab, u, w_gu, b_gu, w_down, b_down):
    n_pairs = tok_tab.shape[0]
    D = u.shape[1]
    pair = 2 * MOE_BLOCK
    tab = lambda f: pl.BlockSpec((1, 1, pair), f, memory_space=pltpu.SMEM)
    grid_spec = pltpu.PrefetchScalarGridSpec(
        num_scalar_prefetch=2,
        grid=(n_pairs,),
        in_specs=[tab(lambda j, pe, nu: (j, 0, 0)),
                  tab(lambda j, pe, nu: (jnp.minimum(j + 1, n_pairs - 1), 0, 0)),
                  pl.BlockSpec(memory_space=pl.ANY),
                  pl.BlockSpec((1, 1, D, 2 * D), lambda j, pe, nu: (layer, pe[j], 0, 0)),
                  pl.BlockSpec((1, 1, 1, 2 * D), lambda j, pe, nu: (layer, pe[j], 0, 0)),
                  pl.BlockSpec((1, 1, D, D), lambda j, pe, nu: (layer, pe[j], 0, 0)),
                  pl.BlockSpec((1, 1, 1, D), lambda j, pe, nu: (layer, pe[j], 0, 0))],
        out_specs=pl.BlockSpec((pair, D), lambda j, pe, nu: (j, 0)),
        scratch_shapes=[pltpu.VMEM((D, 2 * D), BF16), pltpu.VMEM((D, D), BF16),
                        *([pltpu.VMEM((MOE_BLOCK, D), F32)] * 4),
                        pltpu.SemaphoreType.DMA((4,))],
    )
    return pl.pallas_call(
        _moe_kernel,
        grid_spec=grid_spec,
        out_shape=jax.ShapeDtypeStruct((n_pairs * pair, D), BF16),
        compiler_params=pltpu.CompilerParams(dimension_semantics=("arbitrary",),
                                             vmem_limit_bytes=MOE_VMEM_LIMIT),
        name="moe_experts",
    )(pair_expert, n_used, tok_tab, tok_tab, u, w_gu, b_gu.reshape(DEPTH, N_EXPERTS, 1, 2 * D),
      w_down, b_down.reshape(DEPTH, N_EXPERTS, 1, D))


def _moe_ffn(layer, u, route, counts, w_gu, b_gu, w_down, b_down):
    T, D = u.shape
    pair = 2 * MOE_BLOCK
    n_assign = T * TOP_K
    expert = route[:, :TOP_K].astype(jnp.int32)
    rank = route[:, TOP_K:2 * TOP_K].astype(jnp.int32)
    cnt = counts[0, :N_EXPERTS].astype(jnp.int32)
    padded = (cnt + pair - 1) // pair * pair
    padded_end = jnp.cumsum(padded)
    padded_start = padded_end - padded
    lanes_e = jnp.arange(N_EXPERTS, dtype=jnp.int32)
    start_of = jnp.sum(jnp.where(expert[..., None] == lanes_e, padded_start, 0), axis=-1)
    pos = (start_of + rank).T
    n_pairs = -(-n_assign // pair) + N_EXPERTS
    pair_start = jnp.arange(n_pairs, dtype=jnp.int32) * pair
    pair_expert = jnp.minimum(jnp.sum(padded_end[None, :] <= pair_start[:, None], axis=1),
                              N_EXPERTS - 1).astype(jnp.int32)
    n_used = (padded_end[-1:] // pair).astype(jnp.int32)
    tok = jnp.tile(jnp.arange(T, dtype=jnp.int32), TOP_K)
    _, sorted_tok = lax.sort((pos.reshape(-1), tok), num_keys=1)
    cnt_p = cnt[pair_expert][:, None]
    rank_p = (pair_start - padded_start[pair_expert])[:, None] + jnp.arange(pair, dtype=jnp.int32)
    compact = (jnp.cumsum(cnt) - cnt)[pair_expert][:, None] + rank_p
    tok_tab = jnp.where(rank_p < cnt_p, sorted_tok[jnp.minimum(compact, n_assign - 1)], 0)
    yb = _moe_experts(layer, pair_expert, n_used, tok_tab.reshape(n_pairs, 1, pair), u, w_gu, b_gu,
                      w_down, b_down)
    return yb[pos.reshape(-1)].reshape(TOP_K, T, D)


def _final_ln_kernel(x_ref, y_ref, rt_ref, mod_ref, ln_ref, o_ref):
    m = mod_ref[0]
    rt = rt_ref[0]
    f = rt[:, 2 * TOP_K:2 * TOP_K + 1] * y_ref[0].astype(F32)
    for k in range(1, TOP_K):
        f = f + rt[:, 2 * TOP_K + k:2 * TOP_K + k + 1] * y_ref[k].astype(F32)
    z = DEEPNORM_ALPHA * x_ref[0] + m[MOD_G2:MOD_G2 + 1] * f
    mu = jnp.mean(z, axis=-1, keepdims=True)
    zc = z - mu
    y = zc * lax.rsqrt(jnp.mean(zc * zc, axis=-1, keepdims=True) + NORM_EPS)
    o_ref[0] = y * ln_ref[0:1] + ln_ref[1:2]


def _final_ln(x1, yg, route, mods, lnp, nbl):
    B, L, D = x1.shape
    nblk = L // TM
    blk = pl.BlockSpec((1, TM, D), lambda b, i: (b, i, 0))
    return pl.pallas_call(
        _final_ln_kernel,
        grid=(B, nblk),
        in_specs=[blk,
                  pl.BlockSpec((TOP_K, TM, D), lambda b, i: (0, b * nblk + i, 0)),
                  pl.BlockSpec((1, TM, LANES), lambda b, i: (b, i, 0)),
                  pl.BlockSpec((1, 8, D), lambda b, i: (2 * b + i // nbl, 0, 0)),
                  pl.BlockSpec((2, D), lambda b, i: (0, 0))],
        out_specs=blk,
        out_shape=jax.ShapeDtypeStruct((B, L, D), F32),
        compiler_params=_cparams(2),
        name="final_ln",
    )(x1, yg, route, mods, lnp)


def _rope_tables(S, n_ctx):
    rows = S // GRID_W
    row = jnp.repeat(jnp.arange(rows), GRID_W).astype(F32)
    col = jnp.tile(jnp.arange(GRID_W), rows).astype(F32)
    inv = ROPE_BASE ** (-jnp.arange(ROPE_PAIRS_AXIS, dtype=F32) / ROPE_PAIRS_AXIS)
    ang = jnp.concatenate([row[:, None] * inv, col[:, None] * inv], -1)
    cos, sin = jnp.cos(ang), jnp.sin(ang)
    cos = jnp.concatenate([cos, jnp.ones((n_ctx, cos.shape[1]), F32)], axis=0)
    sin = jnp.concatenate([sin, jnp.zeros((n_ctx, sin.shape[1]), F32)], axis=0)
    return (jnp.concatenate([cos, cos, cos, cos], axis=1),
            jnp.concatenate([-sin, -sin, sin, sin], axis=1))


def _qk_column_perm():
    lane = np.arange(DA_HEAD_W)
    half, mp, jj = lane // 64, (lane % 64) // 32, lane % 32
    src = mp * DA_HEAD_DIM + half * 32 + jj
    head = np.arange(DA_HEADS)[:, None] * DA_HEAD_W
    perm = (head + src[None, :]).reshape(-1)
    return np.concatenate([perm, D_MODEL + perm])


def _split_router_w(w):
    w = jnp.pad(w, ((0, 0), (0, LANES - N_EXPERTS)))
    hi = w.astype(BF16)
    lo = (w - hi.astype(F32)).astype(BF16)
    return jnp.concatenate([hi, lo], axis=1)


def _layer_mods(mod_rows, B):
    D = D_MODEL
    parts = mod_rows.reshape(8, 6, D)
    sh1, sc1, g1, sh2, sc2, g2 = (parts[:, n] for n in range(6))
    tab = jnp.stack([1.0 + sc1, sh1, g1, 1.0 + sc2, sh2, g2, jnp.zeros_like(g1),
                     jnp.zeros_like(g1)], axis=1)
    lat = tab[:B]
    ctx = jnp.broadcast_to(tab[B:B + 1], (B, 8, D))
    return jnp.stack([lat, ctx], axis=1).reshape(2 * B, 8, D)


def kernel(x, c, ctx, c_ctx, ada_w, ada_b, ln_g, ln_b, da_w_in, da_w_out, da_lambda, da_subln_w,
           gla_w_in, gla_w_gate, gla_b_gate, gla_norm_w, gla_w_out, router_w, router_b,
           moe_w_gu, moe_b_gu, moe_w_down, moe_b_down):
    B, S, D = x.shape
    n_ctx = ctx.shape[1]
    assert D == D_MODEL and n_ctx == TM and S % TM == 0 and S % GRID_W == 0 and B + 1 <= 8
    nbl = S // TM
    nb = nbl + 1

    cc = jnp.concatenate([c, c_ctx[None, :], jnp.zeros((8 - B - 1, D), F32)], axis=0)
    mod_all = _ada_mods(cc, ada_w, ada_b)
    xa = jnp.concatenate([x, ctx], axis=1)

    rw = [_split_router_w(router_w[i]) for i in range(DEPTH)]
    rb = [jnp.pad(router_b[i], (0, LANES - N_EXPERTS)).reshape(1, LANES) for i in range(DEPTH)]
    lnp = [[jnp.stack([ln_g[i, n], ln_b[i, n]]) for n in range(2)] for i in range(DEPTH)]

    mods = _layer_mods(mod_all[0], B)
    w_in = da_w_in[0]
    wqk = w_in[:, _qk_column_perm()].astype(BF16)
    wv = w_in[:, 2 * D:].astype(BF16)
    cos, sin = _rope_tables(S, n_ctx)
    q, k, v = _proj_da(xa, mods, cos, sin, wqk, wv)
    lam_init = _lambda_init(0)
    lv = da_lambda[0].astype(F32)
    lam = (jnp.exp(jnp.sum(lv[0] * lv[1])) - jnp.exp(jnp.sum(lv[2] * lv[3])) + lam_init).reshape(1)
    o = _diff_attention(lam, q, k, v, da_subln_w[0], lam_init)
    x1, u, route, counts = _mixer_out("da", [o], da_w_out[0].astype(BF16), xa, mods, lnp[0][0],
                                      rw[0], rb[0], nb)
    yg = _moe_ffn(0, u.reshape(B * nb * TM, D), route.reshape(B * nb * TM, LANES), counts,
                  moe_w_gu, moe_b_gu, moe_w_down, moe_b_down)
    xa = _final_ln(x1, yg, route, mods, lnp[0][1], nbl)

    mods = _layer_mods(mod_all[1], B)
    gw = gla_w_in[0]
    c3 = 2 * GLA_DK + 2 * GLA_DV
    w_main = gw[:, :c3].astype(BF16)
    wz = jnp.pad(gw[:, c3:], ((0, 0), (0, LANES - 2 * GLA_GATE_RANK))).astype(BF16)
    wg = jnp.zeros((LANES, 2 * GLA_DK), F32)
    wg = wg.at[:GLA_GATE_RANK, :GLA_DK].set(gla_w_gate[0, 0])
    wg = wg.at[GLA_GATE_RANK:2 * GLA_GATE_RANK, GLA_DK:].set(gla_w_gate[0, 1])
    bg = gla_b_gate[0].reshape(1, 2 * GLA_DK)
    gq, gk, gv, gr, gg = _proj_gla(xa, mods, w_main, wz, wg, bg)
    of = _gla_scan(gq, gk, gv, gg, reverse=False)
    ob = _gla_scan(gq, gk, gv, gg, reverse=True)
    x1, u, route, counts = _mixer_out("gla", [of, ob, gr], gla_w_out[0].astype(BF16), xa, mods,
                                      lnp[1][0], rw[1], rb[1], nbl, norm_w=gla_norm_w[0])
    yg = _moe_ffn(1, u.reshape(B * S, D), route.reshape(B * S, LANES), counts,
                  moe_w_gu, moe_b_gu, moe_w_down, moe_b_down)
    return _final_ln(x1, yg, route, mods, lnp[1][1], nbl)
```

```python
import functools
import math

import numpy as np
import jax
import jax.numpy as jnp
from jax import lax
from jax.experimental import pallas as pl
from jax.experimental.pallas import tpu as pltpu

F32 = jnp.float32
BF16 = jnp.bfloat16

D_MODEL = 1024
DEPTH = 2
GRID_W = 64

DA_HEADS = 8
DA_HEAD_DIM = 64
DA_HEAD_W = 2 * DA_HEAD_DIM
ROPE_BASE = 10000.0
ROPE_PAIRS_AXIS = DA_HEAD_DIM // 4

GLA_HEADS = 4
GLA_DK = D_MODEL // 2
GLA_DV = D_MODEL
GLA_DK_HEAD = GLA_DK // GLA_HEADS
GLA_DV_HEAD = GLA_DV // GLA_HEADS
GLA_GATE_RANK = 16
GLA_TAU = 16.0
GLA_CHUNK = 64

N_EXPERTS = 32
TOP_K = 4
SWIGLU_ALPHA = 1.702
SWIGLU_LIMIT = 7.0
MOE_BLOCK = 256

DEEPNORM_ALPHA = (2.0 * DEPTH) ** 0.25
NORM_EPS = 1e-5

LANES = 128
TM = 256
ATT_TK_MAX = 2816
VMEM_LIMIT = 48 * 1024 * 1024
MOE_VMEM_LIMIT = 56 * 1024 * 1024

MOD_SC1, MOD_SH1, MOD_G1, MOD_SC2, MOD_SH2, MOD_G2 = range(6)


def _cparams(n_axes):
    return pltpu.CompilerParams(dimension_semantics=("arbitrary",) * n_axes,
                                vmem_limit_bytes=VMEM_LIMIT)


def _lambda_init(layer_idx):
    return 0.8 - 0.6 * math.exp(-0.3 * layer_idx)


def _ada_kernel(c_ref, w_ref, b_ref, o_ref):
    c = c_ref[...]
    s = c * jax.nn.sigmoid(c)
    o_ref[0] = jnp.dot(s, w_ref[0], preferred_element_type=F32) + b_ref[0]


def _ada_mods(cc, ada_w, ada_b):
    nt = 1536
    n6 = 6 * D_MODEL
    return pl.pallas_call(
        _ada_kernel,
        grid=(DEPTH, n6 // nt),
        in_specs=[pl.BlockSpec((8, D_MODEL), lambda l, j: (0, 0)),
                  pl.BlockSpec((1, D_MODEL, nt), lambda l, j: (l, 0, j)),
                  pl.BlockSpec((1, 1, nt), lambda l, j: (l, 0, j))],
        out_specs=pl.BlockSpec((1, 8, nt), lambda l, j: (l, 0, j)),
        out_shape=jax.ShapeDtypeStruct((DEPTH, 8, n6), F32),
        compiler_params=_cparams(2),
        name="ada_mods",
    )(cc, ada_w, ada_b.reshape(DEPTH, 1, n6))


def _proj_da_kernel(x_ref, c_ref, mod_ref, cos_ref, sin_ref, wqk_ref, wv_ref, q_ref, k_ref, v_ref):
    m = mod_ref[0]
    xin = jnp.where(pl.program_id(1) < pl.num_programs(1) - 1, x_ref[0], c_ref[0])
    t = (xin * m[MOD_SC1:MOD_SC1 + 1] + m[MOD_SH1:MOD_SH1 + 1]).astype(BF16)
    cos = cos_ref[...]
    sin = sin_ref[...]
    q_scale = DA_HEAD_DIM ** -0.5 * math.log2(math.e)
    for j in range(DA_HEADS):
        y2 = jnp.dot(t, wqk_ref[:, j * 256:(j + 1) * 256], preferred_element_type=F32)
        for hh in range(2):
            y = y2[:, hh * LANES:(hh + 1) * LANES]
            y = y * cos + pltpu.roll(y, 64, 1) * sin
            col = (2 * j + hh) * LANES
            if col < D_MODEL:
                q_ref[0, :, col:col + LANES] = (y * q_scale).astype(BF16)
            else:
                k_ref[0, :, col - D_MODEL:col - D_MODEL + LANES] = y.astype(BF16)
    v_ref[0] = jnp.dot(t, wv_ref[...], preferred_element_type=F32).astype(BF16)


def _proj_da(x, ctx, mods, cos, sin, wqk, wv):
    B, S, D = x.shape
    nbl = S // TM
    nb = nbl + 1
    out = jax.ShapeDtypeStruct((B, nb * TM, D), BF16)
    blk = pl.BlockSpec((1, TM, D), lambda b, i: (b, i, 0))
    return pl.pallas_call(
        _proj_da_kernel,
        grid=(B, nb),
        in_specs=[pl.BlockSpec((1, TM, D), lambda b, i: (b, jnp.minimum(i, nbl - 1), 0)),
                  pl.BlockSpec((1, TM, D), lambda b, i: (b, 0, 0)),
                  pl.BlockSpec((1, 8, D), lambda b, i: (2 * b + i // nbl, 0, 0)),
                  pl.BlockSpec((TM, LANES), lambda b, i: (i, 0)),
                  pl.BlockSpec((TM, LANES), lambda b, i: (i, 0)),
                  pl.BlockSpec((D, 2 * D), lambda b, i: (0, 0)),
                  pl.BlockSpec((D, D), lambda b, i: (0, 0))],
        out_specs=[blk, blk, blk],
        out_shape=[out, out, out],
        compiler_params=_cparams(2),
        name="da_proj",
    )(x, ctx, mods, cos, sin, wqk, wv)


def _attn_kernel(lam_ref, q_ref, k_ref, v_ref, sw_ref, o_ref, vext_sc, m_sc, acc_sc, s_sc, *,
                 n_lat, n_ctx, lam_init):
    i = pl.program_id(2)
    tq = q_ref.shape[1]

    @pl.when(i == 0)
    def _():
        vext_sc[:, :DA_HEAD_W] = v_ref[0]
        vext_sc[:, DA_HEAD_W:] = jnp.ones((vext_sc.shape[0], DA_HEAD_W), BF16)

    q = q_ref[0]
    lane = lax.broadcasted_iota(jnp.int32, (1, DA_HEAD_W), 1)
    map0 = (lane % 64) < 32
    zero = jnp.zeros_like(q)
    qs = jnp.concatenate([jnp.where(map0, q, zero), jnp.where(map0, zero, q)], axis=0)
    m_sc[...] = jnp.full(m_sc.shape, -jnp.inf, F32)
    acc_sc[...] = jnp.zeros(acc_sc.shape, F32)

    def scores(off, tk):
        k = k_ref[0, pl.ds(off, tk), :]
        return lax.dot_general(qs, k, (((1,), (1,)), ((), ())), preferred_element_type=F32)

    def accumulate(s, off, tk):
        m_prev = m_sc[...]
        m_new = jnp.maximum(m_prev, jnp.max(s, axis=1, keepdims=True))
        alpha = jnp.exp2(m_prev - m_new)
        p = jnp.exp2(s - jnp.tile(m_new, (1, tk // LANES)))
        pv = jnp.dot(p.astype(BF16), vext_sc[pl.ds(off, tk), :], preferred_element_type=F32)
        acc_sc[...] = jnp.tile(alpha, (1, 2)) * acc_sc[...] + pv
        m_sc[...] = m_new

    n_q_lat = n_lat // tq
    tk = s_sc.shape[2]
    n_steps = (n_lat + n_ctx) // tk

    @pl.when(i < n_q_lat)
    def _():
        s_sc[0] = scores(0, tk)
        for t in range(n_steps):
            if t + 1 < n_steps:
                s_sc[(t + 1) % 2] = scores((t + 1) * tk, tk)
            accumulate(s_sc[t % 2], t * tk, tk)

    @pl.when(i >= n_q_lat)
    def _():
        accumulate(scores(n_lat, n_ctx), n_lat, n_ctx)

    acc = acc_sc[...]
    o0 = acc[:tq, :DA_HEAD_W] / acc[:tq, DA_HEAD_W:DA_HEAD_W + 1]
    o1 = acc[tq:, :DA_HEAD_W] / acc[tq:, DA_HEAD_W:DA_HEAD_W + 1]
    o = o0 - lam_ref[0] * o1
    o = o * lax.rsqrt(jnp.mean(o * o, axis=-1, keepdims=True) + NORM_EPS)
    o_ref[0] = (o * sw_ref[...] * (1.0 - lam_init)).astype(BF16)


def _diff_attention(lam, q, k, v, subln_w, lam_init):
    B, LT, D = q.shape
    nb = LT // TM
    tk = max(t for t in range(TM, ATT_TK_MAX + 1, TM) if LT % t == 0)
    kern = functools.partial(_attn_kernel, n_lat=LT - TM, n_ctx=TM, lam_init=lam_init)
    grid_spec = pltpu.PrefetchScalarGridSpec(
        num_scalar_prefetch=1,
        grid=(B, DA_HEADS, nb),
        in_specs=[pl.BlockSpec((1, TM, DA_HEAD_W), lambda b, h, i, lam: (b, i, h)),
                  pl.BlockSpec((1, LT, DA_HEAD_W), lambda b, h, i, lam: (b, 0, h)),
                  pl.BlockSpec((1, LT, DA_HEAD_W), lambda b, h, i, lam: (b, 0, h)),
                  pl.BlockSpec((1, DA_HEAD_W), lambda b, h, i, lam: (0, 0))],
        out_specs=pl.BlockSpec((1, TM, DA_HEAD_W), lambda b, h, i, lam: (b, i, h)),
        scratch_shapes=[pltpu.VMEM((LT, 2 * DA_HEAD_W), BF16),
                        pltpu.VMEM((2 * TM, LANES), F32),
                        pltpu.VMEM((2 * TM, 2 * DA_HEAD_W), F32),
                        pltpu.VMEM((2, 2 * TM, tk), F32)],
    )
    return pl.pallas_call(
        kern,
        grid_spec=grid_spec,
        out_shape=jax.ShapeDtypeStruct((B, LT, D), BF16),
        compiler_params=_cparams(3),
        name="diff_attn",
    )(lam, q, k, v, subln_w.reshape(1, DA_HEAD_W))


def _route_block(logits, cnt_sc):
    lane = lax.broadcasted_iota(jnp.int32, logits.shape, 1)
    lane_f = lane.astype(F32)
    work = jnp.where(lane < N_EXPERTS, logits, -jnp.inf)
    tops, idxs, hits = [], [], []
    for _ in range(TOP_K):
        mk = jnp.max(work, axis=1, keepdims=True)
        ik = jnp.min(jnp.where(work == mk, lane_f, float(LANES)), axis=1, keepdims=True)
        hit = lane_f == ik
        tops.append(mk)
        idxs.append(ik)
        hits.append(hit)
        work = jnp.where(hit, -jnp.inf, work)
    chosen = functools.reduce(jnp.logical_or, hits).astype(F32)
    n = logits.shape[0]
    row = lax.broadcasted_iota(jnp.int32, (n, n), 0)
    col = lax.broadcasted_iota(jnp.int32, (n, n), 1)
    before = jnp.dot((col < row).astype(BF16), chosen.astype(BF16), preferred_element_type=F32)
    rank_all = cnt_sc[...] + before
    cnt_sc[...] = cnt_sc[...] + jnp.sum(chosen, axis=0, keepdims=True)
    exps = [jnp.exp(t - tops[0]) for t in tops]
    denom = functools.reduce(jnp.add, exps)
    table = jnp.zeros(logits.shape, F32)
    for k in range(TOP_K):
        rk = jnp.sum(jnp.where(hits[k], rank_all, 0.0), axis=1, keepdims=True)
        table = jnp.where(lane == k, idxs[k], table)
        table = jnp.where(lane == TOP_K + k, rk, table)
        table = jnp.where(lane == 2 * TOP_K + k, exps[k] / denom, table)
    return table


def _post_mixer(pre, w_ref, x_ref, xt_ref, mod_ref, ln_ref, rw_ref, rb_ref, x1_ref, u_ref, rt_ref,
                cnt_ref, cnt_sc, n_main):
    @pl.when((pl.program_id(0) == 0) & (pl.program_id(1) == 0))
    def _():
        cnt_sc[...] = jnp.zeros(cnt_sc.shape, F32)

    m = mod_ref[0]
    y = jnp.dot(pre, w_ref[...], preferred_element_type=F32)
    x_res = jnp.where(pl.program_id(1) < n_main, x_ref[0], xt_ref[0])
    z = DEEPNORM_ALPHA * x_res + m[MOD_G1:MOD_G1 + 1] * y
    mu = jnp.mean(z, axis=-1, keepdims=True)
    zc = z - mu
    x1 = zc * lax.rsqrt(jnp.mean(zc * zc, axis=-1, keepdims=True) + NORM_EPS)
    x1 = x1 * ln_ref[0:1] + ln_ref[1:2]
    x1_ref[0] = x1
    u = x1 * m[MOD_SC2:MOD_SC2 + 1] + m[MOD_SH2:MOD_SH2 + 1]
    u_ref[0] = u
    u_hi = u.astype(BF16)
    u_lo = (u - u_hi.astype(F32)).astype(BF16)
    d_hi = jnp.dot(u_hi, rw_ref[...], preferred_element_type=F32)
    d_lo = jnp.dot(u_lo, rw_ref[:, :LANES], preferred_element_type=F32)
    logits = d_hi[:, :LANES] + d_hi[:, LANES:] + d_lo + rb_ref[...]
    rt_ref[0] = _route_block(logits, cnt_sc)
    cnt_ref[...] = jnp.broadcast_to(cnt_sc[...], cnt_ref.shape)


def _out_da_kernel(o_ref, w_ref, x_ref, xt_ref, mod_ref, ln_ref, rw_ref, rb_ref, x1_ref, u_ref, rt_ref,
                   cnt_ref, cnt_sc, *, n_main):
    _post_mixer(o_ref[0], w_ref, x_ref, xt_ref, mod_ref, ln_ref, rw_ref, rb_ref, x1_ref, u_ref, rt_ref,
                cnt_ref, cnt_sc, n_main)


def _out_gla_kernel(of_ref, ob_ref, r_ref, nw_ref, w_ref, x_ref, xt_ref, mod_ref, ln_ref, rw_ref, rb_ref,
                    x1_ref, u_ref, rt_ref, cnt_ref, cnt_sc, *, n_main):
    parts = []
    for h in range(GLA_HEADS):
        sl = slice(h * GLA_DV_HEAD, (h + 1) * GLA_DV_HEAD)
        o = of_ref[0, :, sl] + ob_ref[0, :, sl]
        o = o * lax.rsqrt(jnp.mean(o * o, axis=-1, keepdims=True) + NORM_EPS) * nw_ref[...]
        r = r_ref[0, :, sl]
        parts.append((o * (r * jax.nn.sigmoid(r))).astype(BF16))
    pre = jnp.concatenate(parts, axis=1)
    _post_mixer(pre, w_ref, x_ref, xt_ref, mod_ref, ln_ref, rw_ref, rb_ref, x1_ref, u_ref, rt_ref,
                cnt_ref, cnt_sc, n_main)


def _mixer_out(kind, acts, w_out, x_main, x_tail, nbl, tail_blk, mods, lnp, rw, rb, nb_out, norm_w=None):
    B, _, D = x_main.shape
    blk = pl.BlockSpec((1, TM, D), lambda b, i: (b, i, 0))
    common_specs = [pl.BlockSpec((D, D), lambda b, i: (0, 0)),
                    pl.BlockSpec((1, TM, D), lambda b, i: (b, jnp.minimum(i, nbl - 1), 0)),
                    pl.BlockSpec((1, TM, D), lambda b, i: (b, tail_blk, 0)),
                    pl.BlockSpec((1, 8, D), lambda b, i: (2 * b + i // nbl, 0, 0)),
                    pl.BlockSpec((2, D), lambda b, i: (0, 0)),
                    pl.BlockSpec((D, 2 * LANES), lambda b, i: (0, 0)),
                    pl.BlockSpec((1, LANES), lambda b, i: (0, 0))]
    lout = nb_out * TM
    out_shape = [jax.ShapeDtypeStruct((B, lout, D), F32),
                 jax.ShapeDtypeStruct((B, lout, D), F32),
                 jax.ShapeDtypeStruct((B, lout, LANES), F32),
                 jax.ShapeDtypeStruct((8, LANES), F32)]
    out_specs = [blk, blk, pl.BlockSpec((1, TM, LANES), lambda b, i: (b, i, 0)),
                 pl.BlockSpec((8, LANES), lambda b, i: (0, 0))]
    if kind == "da":
        kern = _out_da_kernel
        in_specs = [blk] + common_specs
        args = list(acts)
    else:
        kern = _out_gla_kernel
        in_specs = [blk, blk, blk, pl.BlockSpec((1, GLA_DV_HEAD), lambda b, i: (0, 0))] + common_specs
        args = list(acts) + [norm_w.reshape(1, GLA_DV_HEAD)]
    return pl.pallas_call(
        functools.partial(kern, n_main=nbl),
        grid=(B, nb_out),
        in_specs=in_specs,
        out_specs=out_specs,
        out_shape=out_shape,
        scratch_shapes=[pltpu.VMEM((1, LANES), F32)],
        compiler_params=_cparams(2),
        name="mixer_out_" + kind,
    )(*args, w_out, x_main, x_tail, mods, lnp, rw, rb)


def _proj_gla_kernel(x_ref, mod_ref, w_ref, wz_ref, wg_ref, bg_ref,
                     q_ref, k_ref, v_ref, r_ref, g_ref):
    m = mod_ref[0]
    t = (x_ref[0] * m[MOD_SC1:MOD_SC1 + 1] + m[MOD_SH1:MOD_SH1 + 1]).astype(BF16)
    c0, c1, c2, c3 = GLA_DK, 2 * GLA_DK, 2 * GLA_DK + GLA_DV, 2 * GLA_DK + 2 * GLA_DV
    q_ref[0] = jnp.dot(t, w_ref[:, :c0], preferred_element_type=F32) * (GLA_DK_HEAD ** -0.5)
    k_ref[0] = jnp.dot(t, w_ref[:, c0:c1], preferred_element_type=F32)
    v_ref[0] = jnp.dot(t, w_ref[:, c1:c2], preferred_element_type=F32).astype(BF16)
    r_ref[0] = jnp.dot(t, w_ref[:, c2:c3], preferred_element_type=F32)
    z = jnp.dot(t, wz_ref[...], preferred_element_type=F32)
    gl = jnp.dot(z, wg_ref[...], preferred_element_type=F32) + bg_ref[...]
    log_sig = jnp.minimum(gl, 0.0) - jnp.log1p(jnp.exp(-jnp.abs(gl)))
    g_ref[0] = log_sig * (1.0 / GLA_TAU)


def _proj_gla(xa, mods, w_main, wz, wg, bg):
    B, LT, D = xa.shape
    nb = LT // TM
    nbl = nb - 1
    blk = lambda w: pl.BlockSpec((1, TM, w), lambda b, i: (b, i, 0))
    return pl.pallas_call(
        _proj_gla_kernel,
        grid=(B, nb),
        in_specs=[blk(D),
                  pl.BlockSpec((1, 8, D), lambda b, i: (2 * b + i // nbl, 0, 0)),
                  pl.BlockSpec(w_main.shape, lambda b, i: (0, 0)),
                  pl.BlockSpec(wz.shape, lambda b, i: (0, 0)),
                  pl.BlockSpec(wg.shape, lambda b, i: (0, 0)),
                  pl.BlockSpec(bg.shape, lambda b, i: (0, 0))],
        out_specs=[blk(GLA_DK), blk(GLA_DK), blk(GLA_DV), blk(GLA_DV), blk(2 * GLA_DK)],
        out_shape=[jax.ShapeDtypeStruct((B, LT, GLA_DK), F32),
                   jax.ShapeDtypeStruct((B, LT, GLA_DK), F32),
                   jax.ShapeDtypeStruct((B, LT, GLA_DV), BF16),
                   jax.ShapeDtypeStruct((B, LT, GLA_DV), F32),
                   jax.ShapeDtypeStruct((B, LT, 2 * GLA_DK), F32)],
        compiler_params=_cparams(2),
        name="gla_proj",
    )(xa, mods, w_main, wz, wg, bg)


def _gla_scan_kernel(q_ref, k_ref, v_ref, g_ref, o_ref, st_sc, *, reverse):
    j = pl.program_id(1)

    @pl.when(j == 0)
    def _():
        st_sc[...] = jnp.zeros(st_sc.shape, F32)

    C = GLA_CHUNK
    n_chunks = TM // C

    def causal(n):
        row = lax.broadcasted_iota(jnp.int32, (n, n), 0)
        col = lax.broadcasted_iota(jnp.int32, (n, n), 1)
        return (row // C == col // C) & ((col >= row) if reverse else (col <= row))

    keep = causal(C)
    tri = causal(TM).astype(BF16)
    g = g_ref[0]
    g_hi = g.astype(BF16)
    rem = g - g_hi.astype(F32)
    g_mid = rem.astype(BF16)
    g_lo = (rem - g_mid.astype(F32)).astype(BF16)
    b_all = (jnp.dot(tri, g_hi, preferred_element_type=F32)
             + jnp.dot(tri, g_mid, preferred_element_type=F32)
             + jnp.dot(tri, g_lo, preferred_element_type=F32))

    states = [st_sc[h] for h in range(GLA_HEADS)]
    order = range(n_chunks - 1, -1, -1) if reverse else range(n_chunks)
    for c in order:
        rows = slice(c * C, (c + 1) * C)
        for h in range(GLA_HEADS):
            ks = slice(h * GLA_DK_HEAD, (h + 1) * GLA_DK_HEAD)
            vs = slice(h * GLA_DV_HEAD, (h + 1) * GLA_DV_HEAD)
            b = b_all[rows, ks]
            tot = b[0:1] if reverse else b[C - 1:C]
            q = q_ref[0, rows, ks]
            k = k_ref[0, rows, ks]
            q_in = (q * jnp.exp(b)).astype(BF16)
            k_in = (k * jnp.exp(-b)).astype(BF16)
            k_st = (k * jnp.exp(tot - b)).astype(BF16)
            att = lax.dot_general(q_in, k_in, (((1,), (1,)), ((), ())), preferred_element_type=F32)
            att = jnp.where(keep, att, 0.0).astype(BF16)
            v = v_ref[0, rows, vs]
            st = states[h]
            o = jnp.dot(att, v, preferred_element_type=F32)
            o = o + lax.dot_general(q_in, st.astype(BF16), (((1,), (1,)), ((), ())),
                                    preferred_element_type=F32)
            o_ref[0, rows, vs] = o
            ds = lax.dot_general(v, k_st, (((0,), (0,)), ((), ())), preferred_element_type=F32)
            states[h] = st * jnp.exp(tot) + ds
    for h in range(GLA_HEADS):
        st_sc[h] = states[h]


def _gla_scan(q, k, v, g, reverse):
    B, LT, _ = q.shape
    nb = LT // TM
    ctx_blk = nb - 1
    if reverse:
        order = lambda j: jnp.where(j == 0, ctx_blk, ctx_blk - j)
    else:
        order = lambda j: jnp.where(j == 0, ctx_blk, j - 1)
    gcol = 1 if reverse else 0
    return pl.pallas_call(
        functools.partial(_gla_scan_kernel, reverse=reverse),
        grid=(B, nb),
        in_specs=[pl.BlockSpec((1, TM, GLA_DK), lambda b, j: (b, order(j), 0)),
                  pl.BlockSpec((1, TM, GLA_DK), lambda b, j: (b, order(j), 0)),
                  pl.BlockSpec((1, TM, GLA_DV), lambda b, j: (b, order(j), 0)),
                  pl.BlockSpec((1, TM, GLA_DK), lambda b, j: (b, order(j), gcol))],
        out_specs=pl.BlockSpec((1, TM, GLA_DV), lambda b, j: (b, order(j), 0)),
        out_shape=jax.ShapeDtypeStruct((B, LT, GLA_DV), F32),
        scratch_shapes=[pltpu.VMEM((GLA_HEADS, GLA_DV_HEAD, GLA_DK_HEAD), F32)],
        compiler_params=_cparams(2),
        name="gla_scan_bwd" if reverse else "gla_scan_fwd",
    )(q, k, v, g)


def _moe_kernel(pe_ref, nu_ref, tok_ref, tokn_ref, u_hbm, wgu_ref, bgu_ref, wd_ref, bd_ref, y_ref,
                wgu_sc, wd_sc, x0, x1, x2, x3, gsem):
    j = pl.program_id(0)
    n_used = nu_ref[0]
    H = MOE_BLOCK
    xbuf = (x0, x1, x2, x3)

    def gather(tab_ref, half, buf):
        for r in range(H):
            t = tab_ref[0, 0, half * H + r]
            pltpu.make_async_copy(u_hbm.at[pl.ds(t, 1), :], xbuf[buf].at[pl.ds(r, 1), :],
                                  gsem.at[buf]).start(priority=r % 2)

    def gather_wait(buf):
        pltpu.make_async_copy(u_hbm.at[pl.ds(0, H), :], xbuf[buf], gsem.at[buf]).wait()

    def ffn(half, buf):
        gu = jnp.dot(xbuf[buf][...].astype(BF16), wgu_sc[...], preferred_element_type=F32)
        gu = gu + bgu_ref[0, 0]
        mid = gu.shape[1] // 2
        glu = jnp.minimum(gu[:, :mid], SWIGLU_LIMIT)
        lin = jnp.clip(gu[:, mid:], -SWIGLU_LIMIT, SWIGLU_LIMIT)
        act = glu * jax.nn.sigmoid(SWIGLU_ALPHA * glu) * (lin + 1.0)
        y = jnp.dot(act.astype(BF16), wd_sc[...], preferred_element_type=F32) + bd_ref[0, 0]
        y_ref[half * H:(half + 1) * H, :] = y.astype(y_ref.dtype)

    def pair_step(cur, nxt):
        gather_wait(cur)
        gather(tokn_ref, 0, nxt)
        ffn(0, cur)
        gather_wait(cur + 1)
        gather(tokn_ref, 1, nxt + 1)
        ffn(1, cur + 1)

        @pl.when(j == n_used - 1)
        def _():
            gather_wait(nxt)
            gather_wait(nxt + 1)

    @pl.when(j < n_used)
    def _():
        @pl.when(j == 0)
        def _():
            gather(tok_ref, 0, 0)
            gather(tok_ref, 1, 1)

        @pl.when((j == 0) | (pe_ref[j] != pe_ref[jnp.maximum(j - 1, 0)]))
        def _():
            wgu_sc[...] = wgu_ref[0, 0].astype(BF16)
            wd_sc[...] = wd_ref[0, 0].astype(BF16)

        @pl.when(j % 2 == 0)
        def _():
            pair_step(0, 2)

        @pl.when(j % 2 == 1)
        def _():
            pair_step(2, 0)

    @pl.when(j >= n_used)
    def _():
        y_ref[...] = jnp.zeros(y_ref.shape, y_ref.dtype)


def _moe_experts(layer, pair_expert, n_used, tok_tab, u, w_gu, b_gu, w_down, b_down):
    n_pairs = tok_tab.shape[0]
    D = u.shape[1]
    pair = 2 * MOE_BLOCK
    tab = lambda f: pl.BlockSpec((1, 1, pair), f, memory_space=pltpu.SMEM)
    grid_spec = pltpu.PrefetchScalarGridSpec(
        num_scalar_prefetch=2,
        grid=(n_pairs,),
        in_specs=[tab(lambda j, pe, nu: (j, 0, 0)),
                  tab(lambda j, pe, nu: (jnp.minimum(j + 1, n_pairs - 1), 0, 0)),
                  pl.BlockSpec(memory_space=pl.ANY),
                  pl.BlockSpec((1, 1, D, 2 * D), lambda j, pe, nu: (layer, pe[j], 0, 0)),
                  pl.BlockSpec((1, 1, 1, 2 * D), lambda j, pe, nu: (layer, pe[j], 0, 0)),
                  pl.BlockSpec((1, 1, D, D), lambda j, pe, nu: (layer, pe[j], 0, 0)),
                  pl.BlockSpec((1, 1, 1, D), lambda j, pe, nu: (layer, pe[j], 0, 0))],
        out_specs=pl.BlockSpec((pair, D), lambda j, pe, nu: (j, 0)),
        scratch_shapes=[pltpu.VMEM((D, 2 * D), BF16), pltpu.VMEM((D, D), BF16),
                        *([pltpu.VMEM((MOE_BLOCK, D), F32)] * 4),
                        pltpu.SemaphoreType.DMA((4,))],
    )
    return pl.pallas_call(
        _moe_kernel,
        grid_spec=grid_spec,
        out_shape=jax.ShapeDtypeStruct((n_pairs * pair, D), BF16),
        compiler_params=pltpu.CompilerParams(dimension_semantics=("arbitrary",),
                                             vmem_limit_bytes=MOE_VMEM_LIMIT),
        name="moe_experts",
    )(pair_expert, n_used, tok_tab, tok_tab, u, w_gu, b_gu.reshape(DEPTH, N_EXPERTS, 1, 2 * D),
      w_down, b_down.reshape(DEPTH, N_EXPERTS, 1, D))


def _moe_ffn(layer, u, route, counts, w_gu, b_gu, w_down, b_down):
    T, D = u.shape
    pair = 2 * MOE_BLOCK
    n_assign = T * TOP_K
    expert = route[:, :TOP_K].astype(jnp.int32)
    rank = route[:, TOP_K:2 * TOP_K].astype(jnp.int32)
    cnt = counts[0, :N_EXPERTS].astype(jnp.int32)
    padded = (cnt + pair - 1) // pair * pair
    padded_end = jnp.cumsum(padded)
    padded_start = padded_end - padded
    lanes_e = jnp.arange(N_EXPERTS, dtype=jnp.int32)
    start_of = jnp.sum(jnp.where(expert[..., None] == lanes_e, padded_start, 0), axis=-1)
    pos = (start_of + rank).T
    n_pairs = -(-n_assign // pair) + N_EXPERTS
    pair_start = jnp.arange(n_pairs, dtype=jnp.int32) * pair
    pair_expert = jnp.minimum(jnp.sum(padded_end[None, :] <= pair_start[:, None], axis=1),
                              N_EXPERTS - 1).astype(jnp.int32)
    n_used = (padded_end[-1:] // pair).astype(jnp.int32)
    tok = jnp.tile(jnp.arange(T, dtype=jnp.int32), TOP_K)
    _, sorted_tok = lax.sort((pos.reshape(-1), tok), num_keys=1)
    cnt_p = cnt[pair_expert][:, None]
    rank_p = (pair_start - padded_start[pair_expert])[:, None] + jnp.arange(pair, dtype=jnp.int32)
    compact = (jnp.cumsum(cnt) - cnt)[pair_expert][:, None] + rank_p
    tok_tab = jnp.where(rank_p < cnt_p, sorted_tok[jnp.minimum(compact, n_assign - 1)], 0)
    yb = _moe_experts(layer, pair_expert, n_used, tok_tab.reshape(n_pairs, 1, pair), u, w_gu, b_gu,
                      w_down, b_down)
    return yb[pos.reshape(-1)].reshape(TOP_K, T, D)


def _final_ln_kernel(x_ref, y_ref, rt_ref, mod_ref, ln_ref, o_ref):
    m = mod_ref[0]
    rt = rt_ref[0]
    f = rt[:, 2 * TOP_K:2 * TOP_K + 1] * y_ref[0].astype(F32)
    for k in range(1, TOP_K):
        f = f + rt[:, 2 * TOP_K + k:2 * TOP_K + k + 1] * y_ref[k].astype(F32)
    z = DEEPNORM_ALPHA * x_ref[0] + m[MOD_G2:MOD_G2 + 1] * f
    mu = jnp.mean(z, axis=-1, keepdims=True)
    zc = z - mu
    y = zc * lax.rsqrt(jnp.mean(zc * zc, axis=-1, keepdims=True) + NORM_EPS)
    o_ref[0] = y * ln_ref[0:1] + ln_ref[1:2]


def _final_ln(x1, yg, route, mods, lnp, nbl):
    B, L, D = x1.shape
    nblk = L // TM
    blk = pl.BlockSpec((1, TM, D), lambda b, i: (b, i, 0))
    return pl.pallas_call(
        _final_ln_kernel,
        grid=(B, nblk),
        in_specs=[blk,
                  pl.BlockSpec((TOP_K, TM, D), lambda b, i: (0, b * nblk + i, 0)),
                  pl.BlockSpec((1, TM, LANES), lambda b, i: (b, i, 0)),
                  pl.BlockSpec((1, 8, D), lambda b, i: (2 * b + i // nbl, 0, 0)),
                  pl.BlockSpec((2, D), lambda b, i: (0, 0))],
        out_specs=blk,
        out_shape=jax.ShapeDtypeStruct((B, L, D), F32),
        compiler_params=_cparams(2),
        name="final_ln",
    )(x1, yg, route, mods, lnp)


def _rope_tables(S, n_ctx):
    rows = S // GRID_W
    row = jnp.repeat(jnp.arange(rows), GRID_W).astype(F32)
    col = jnp.tile(jnp.arange(GRID_W), rows).astype(F32)
    inv = ROPE_BASE ** (-jnp.arange(ROPE_PAIRS_AXIS, dtype=F32) / ROPE_PAIRS_AXIS)
    ang = jnp.concatenate([row[:, None] * inv, col[:, None] * inv], -1)
    cos, sin = jnp.cos(ang), jnp.sin(ang)
    cos = jnp.concatenate([cos, jnp.ones((n_ctx, cos.shape[1]), F32)], axis=0)
    sin = jnp.concatenate([sin, jnp.zeros((n_ctx, sin.shape[1]), F32)], axis=0)
    return (jnp.concatenate([cos, cos, cos, cos], axis=1),
            jnp.concatenate([-sin, -sin, sin, sin], axis=1))


def _qk_column_perm():
    lane = np.arange(DA_HEAD_W)
    half, mp, jj = lane // 64, (lane % 64) // 32, lane % 32
    src = mp * DA_HEAD_DIM + half * 32 + jj
    head = np.arange(DA_HEADS)[:, None] * DA_HEAD_W
    perm = (head + src[None, :]).reshape(-1)
    return np.concatenate([perm, D_MODEL + perm])


def _split_router_w(w):
    w = jnp.pad(w, ((0, 0), (0, LANES - N_EXPERTS)))
    hi = w.astype(BF16)
    lo = (w - hi.astype(F32)).astype(BF16)
    return jnp.concatenate([hi, lo], axis=1)


def _layer_mods(mod_rows, B):
    D = D_MODEL
    parts = mod_rows.reshape(8, 6, D)
    sh1, sc1, g1, sh2, sc2, g2 = (parts[:, n] for n in range(6))
    tab = jnp.stack([1.0 + sc1, sh1, g1, 1.0 + sc2, sh2, g2, jnp.zeros_like(g1),
                     jnp.zeros_like(g1)], axis=1)
    lat = tab[:B]
    ctx = jnp.broadcast_to(tab[B:B + 1], (B, 8, D))
    return jnp.stack([lat, ctx], axis=1).reshape(2 * B, 8, D)


def kernel(x, c, ctx, c_ctx, ada_w, ada_b, ln_g, ln_b, da_w_in, da_w_out, da_lambda, da_subln_w,
           gla_w_in, gla_w_gate, gla_b_gate, gla_norm_w, gla_w_out, router_w, router_b,
           moe_w_gu, moe_b_gu, moe_w_down, moe_b_down):
    B, S, D = x.shape
    n_ctx = ctx.shape[1]
    assert D == D_MODEL and n_ctx == TM and S % TM == 0 and S % GRID_W == 0 and B + 1 <= 8
    nbl = S // TM
    nb = nbl + 1

    cc = jnp.concatenate([c, c_ctx[None, :], jnp.zeros((8 - B - 1, D), F32)], axis=0)
    mod_all = _ada_mods(cc, ada_w, ada_b)

    rw = [_split_router_w(router_w[i]) for i in range(DEPTH)]
    rb = [jnp.pad(router_b[i], (0, LANES - N_EXPERTS)).reshape(1, LANES) for i in range(DEPTH)]
    lnp = [[jnp.stack([ln_g[i, n], ln_b[i, n]]) for n in range(2)] for i in range(DEPTH)]

    mods = _layer_mods(mod_all[0], B)
    w_in = da_w_in[0]
    wqk = w_in[:, _qk_column_perm()].astype(BF16)
    wv = w_in[:, 2 * D:].astype(BF16)
    cos, sin = _rope_tables(S, n_ctx)
    q, k, v = _proj_da(x, ctx, mods, cos, sin, wqk, wv)
    lam_init = _lambda_init(0)
    lv = da_lambda[0].astype(F32)
    lam = (jnp.exp(jnp.sum(lv[0] * lv[1])) - jnp.exp(jnp.sum(lv[2] * lv[3])) + lam_init).reshape(1)
    o = _diff_attention(lam, q, k, v, da_subln_w[0], lam_init)
    x1, u, route, counts = _mixer_out("da", [o], da_w_out[0].astype(BF16), x, ctx, nbl, 0, mods,
                                      lnp[0][0], rw[0], rb[0], nb)
    yg = _moe_ffn(0, u.reshape(B * nb * TM, D), route.reshape(B * nb * TM, LANES), counts,
                  moe_w_gu, moe_b_gu, moe_w_down, moe_b_down)
    xa = _final_ln(x1, yg, route, mods, lnp[0][1], nbl)

    mods = _layer_mods(mod_all[1], B)
    gw = gla_w_in[0]
    c3 = 2 * GLA_DK + 2 * GLA_DV
    w_main = gw[:, :c3].astype(BF16)
    wz = jnp.pad(gw[:, c3:], ((0, 0), (0, LANES - 2 * GLA_GATE_RANK))).astype(BF16)
    wg = jnp.zeros((LANES, 2 * GLA_DK), F32)
    wg = wg.at[:GLA_GATE_RANK, :GLA_DK].set(gla_w_gate[0, 0])
    wg = wg.at[GLA_GATE_RANK:2 * GLA_GATE_RANK, GLA_DK:].set(gla_w_gate[0, 1])
    bg = gla_b_gate[0].reshape(1, 2 * GLA_DK)
    gq, gk, gv, gr, gg = _proj_gla(xa, mods, w_main, wz, wg, bg)
    of = _gla_scan(gq, gk, gv, gg, reverse=False)
    ob = _gla_scan(gq, gk, gv, gg, reverse=True)
    x1, u, route, counts = _mixer_out("gla", [of, ob, gr], gla_w_out[0].astype(BF16), xa, xa, nbl, nbl,
                                      mods, lnp[1][0], rw[1], rb[1], nbl, norm_w=gla_norm_w[0])
    yg = _moe_ffn(1, u.reshape(B * S, D), route.reshape(B * S, LANES), counts,
                  moe_w_gu, moe_b_gu, moe_w_down, moe_b_down)
    return _final_ln(x1, yg, route, mods, lnp[1][1], nbl)
```

```python
import functools
import math

import numpy as np
import jax
import jax.numpy as jnp
from jax import lax
from jax.experimental import pallas as pl
from jax.experimental.pallas import tpu as pltpu

F32 = jnp.float32
BF16 = jnp.bfloat16

D_MODEL = 1024
DEPTH = 2
GRID_W = 64

DA_HEADS = 8
DA_HEAD_DIM = 64
DA_HEAD_W = 2 * DA_HEAD_DIM
ROPE_BASE = 10000.0
ROPE_PAIRS_AXIS = DA_HEAD_DIM // 4

GLA_HEADS = 4
GLA_DK = D_MODEL // 2
GLA_DV = D_MODEL
GLA_DK_HEAD = GLA_DK // GLA_HEADS
GLA_DV_HEAD = GLA_DV // GLA_HEADS
GLA_GATE_RANK = 16
GLA_TAU = 16.0
GLA_CHUNK = 64

N_EXPERTS = 32
TOP_K = 4
SWIGLU_ALPHA = 1.702
SWIGLU_LIMIT = 7.0
MOE_BLOCK = 256

DEEPNORM_ALPHA = (2.0 * DEPTH) ** 0.25
NORM_EPS = 1e-5

LANES = 128
TM = 256
ATT_TK_MAX = 2816
VMEM_LIMIT = 48 * 1024 * 1024
MOE_VMEM_LIMIT = 56 * 1024 * 1024

MOD_SC1, MOD_SH1, MOD_G1, MOD_SC2, MOD_SH2, MOD_G2 = range(6)


def _cparams(n_axes):
    return pltpu.CompilerParams(dimension_semantics=("arbitrary",) * n_axes,
                                vmem_limit_bytes=VMEM_LIMIT)


def _lambda_init(layer_idx):
    return 0.8 - 0.6 * math.exp(-0.3 * layer_idx)


def _ada_kernel(c_ref, w_ref, b_ref, o_ref):
    c = c_ref[...]
    s = c * jax.nn.sigmoid(c)
    o_ref[0] = jnp.dot(s, w_ref[0], preferred_element_type=F32) + b_ref[0]


def _ada_mods(cc, ada_w, ada_b):
    nt = 1536
    n6 = 6 * D_MODEL
    return pl.pallas_call(
        _ada_kernel,
        grid=(DEPTH, n6 // nt),
        in_specs=[pl.BlockSpec((8, D_MODEL), lambda l, j: (0, 0)),
                  pl.BlockSpec((1, D_MODEL, nt), lambda l, j: (l, 0, j)),
                  pl.BlockSpec((1, 1, nt), lambda l, j: (l, 0, j))],
        out_specs=pl.BlockSpec((1, 8, nt), lambda l, j: (l, 0, j)),
        out_shape=jax.ShapeDtypeStruct((DEPTH, 8, n6), F32),
        compiler_params=_cparams(2),
        name="ada_mods",
    )(cc, ada_w, ada_b.reshape(DEPTH, 1, n6))


def _proj_da_kernel(x_ref, c_ref, mod_ref, cos_ref, sin_ref, wqk_ref, wv_ref, q_ref, k_ref, v_ref):
    m = mod_ref[0]
    xin = jnp.where(pl.program_id(1) < pl.num_programs(1) - 1, x_ref[0], c_ref[0])
    t = (xin * m[MOD_SC1:MOD_SC1 + 1] + m[MOD_SH1:MOD_SH1 + 1]).astype(BF16)
    cos = cos_ref[...]
    sin = sin_ref[...]
    q_scale = DA_HEAD_DIM ** -0.5 * math.log2(math.e)
    for j in range(DA_HEADS):
        y2 = jnp.dot(t, wqk_ref[:, j * 256:(j + 1) * 256], preferred_element_type=F32)
        for hh in range(2):
            y = y2[:, hh * LANES:(hh + 1) * LANES]
            y = y * cos + pltpu.roll(y, 64, 1) * sin
            col = (2 * j + hh) * LANES
            if col < D_MODEL:
                q_ref[0, :, col:col + LANES] = (y * q_scale).astype(BF16)
            else:
                k_ref[0, :, col - D_MODEL:col - D_MODEL + LANES] = y.astype(BF16)
    v_ref[0] = jnp.dot(t, wv_ref[...], preferred_element_type=F32).astype(BF16)


def _proj_da(x, ctx, mods, cos, sin, wqk, wv):
    B, S, D = x.shape
    nbl = S // TM
    nb = nbl + 1
    out = jax.ShapeDtypeStruct((B, nb * TM, D), BF16)
    blk = pl.BlockSpec((1, TM, D), lambda b, i: (b, i, 0))
    return pl.pallas_call(
        _proj_da_kernel,
        grid=(B, nb),
        in_specs=[pl.BlockSpec((1, TM, D), lambda b, i: (b, jnp.minimum(i, nbl - 1), 0)),
                  pl.BlockSpec((1, TM, D), lambda b, i: (b, 0, 0)),
                  pl.BlockSpec((1, 8, D), lambda b, i: (2 * b + i // nbl, 0, 0)),
                  pl.BlockSpec((TM, LANES), lambda b, i: (i, 0)),
                  pl.BlockSpec((TM, LANES), lambda b, i: (i, 0)),
                  pl.BlockSpec((D, 2 * D), lambda b, i: (0, 0)),
                  pl.BlockSpec((D, D), lambda b, i: (0, 0))],
        out_specs=[blk, blk, blk],
        out_shape=[out, out, out],
        compiler_params=_cparams(2),
        name="da_proj",
    )(x, ctx, mods, cos, sin, wqk, wv)


def _attn_kernel(lam_ref, q_ref, k_ref, v_ref, sw_ref, o_ref, vext_sc, m_sc, acc_sc, s_sc, *,
                 n_lat, n_ctx, lam_init):
    i = pl.program_id(2)
    tq = q_ref.shape[1]

    @pl.when(i == 0)
    def _():
        vext_sc[:, :DA_HEAD_W] = v_ref[0]
        vext_sc[:, DA_HEAD_W:] = jnp.ones((vext_sc.shape[0], DA_HEAD_W), BF16)

    q = q_ref[0]
    lane = lax.broadcasted_iota(jnp.int32, (1, DA_HEAD_W), 1)
    map0 = (lane % 64) < 32
    zero = jnp.zeros_like(q)
    qs = jnp.concatenate([jnp.where(map0, q, zero), jnp.where(map0, zero, q)], axis=0)
    m_sc[...] = jnp.full(m_sc.shape, -jnp.inf, F32)
    acc_sc[...] = jnp.zeros(acc_sc.shape, F32)

    def scores(off, tk):
        k = k_ref[0, pl.ds(off, tk), :]
        return lax.dot_general(qs, k, (((1,), (1,)), ((), ())), preferred_element_type=F32)

    def accumulate(s, off, tk):
        m_prev = m_sc[...]
        m_new = jnp.maximum(m_prev, jnp.max(s, axis=1, keepdims=True))
        alpha = jnp.exp2(m_prev - m_new)
        p = jnp.exp2(s - jnp.tile(m_new, (1, tk // LANES)))
        pv = jnp.dot(p.astype(BF16), vext_sc[pl.ds(off, tk), :], preferred_element_type=F32)
        acc_sc[...] = jnp.tile(alpha, (1, 2)) * acc_sc[...] + pv
        m_sc[...] = m_new

    n_q_lat = n_lat // tq
    tk = s_sc.shape[2]
    n_steps = (n_lat + n_ctx) // tk

    @pl.when(i < n_q_lat)
    def _():
        s_sc[0] = scores(0, tk)
        for t in range(n_steps):
            if t + 1 < n_steps:
                s_sc[(t + 1) % 2] = scores((t + 1) * tk, tk)
            accumulate(s_sc[t % 2], t * tk, tk)

    @pl.when(i >= n_q_lat)
    def _():
        accumulate(scores(n_lat, n_ctx), n_lat, n_ctx)

    acc = acc_sc[...]
    o0 = acc[:tq, :DA_HEAD_W] / acc[:tq, DA_HEAD_W:DA_HEAD_W + 1]
    o1 = acc[tq:, :DA_HEAD_W] / acc[tq:, DA_HEAD_W:DA_HEAD_W + 1]
    o = o0 - lam_ref[0] * o1
    o = o * lax.rsqrt(jnp.mean(o * o, axis=-1, keepdims=True) + NORM_EPS)
    o_ref[0] = (o * sw_ref[...] * (1.0 - lam_init)).astype(BF16)


def _diff_attention(lam, q, k, v, subln_w, lam_init):
    B, LT, D = q.shape
    nb = LT // TM
    tk = max(t for t in range(TM, ATT_TK_MAX + 1, TM) if LT % t == 0)
    kern = functools.partial(_attn_kernel, n_lat=LT - TM, n_ctx=TM, lam_init=lam_init)
    grid_spec = pltpu.PrefetchScalarGridSpec(
        num_scalar_prefetch=1,
        grid=(B, DA_HEADS, nb),
        in_specs=[pl.BlockSpec((1, TM, DA_HEAD_W), lambda b, h, i, lam: (b, i, h)),
                  pl.BlockSpec((1, LT, DA_HEAD_W), lambda b, h, i, lam: (b, 0, h)),
                  pl.BlockSpec((1, LT, DA_HEAD_W), lambda b, h, i, lam: (b, 0, h)),
                  pl.BlockSpec((1, DA_HEAD_W), lambda b, h, i, lam: (0, 0))],
        out_specs=pl.BlockSpec((1, TM, DA_HEAD_W), lambda b, h, i, lam: (b, i, h)),
        scratch_shapes=[pltpu.VMEM((LT, 2 * DA_HEAD_W), BF16),
                        pltpu.VMEM((2 * TM, LANES), F32),
                        pltpu.VMEM((2 * TM, 2 * DA_HEAD_W), F32),
                        pltpu.VMEM((2, 2 * TM, tk), F32)],
    )
    return pl.pallas_call(
        kern,
        grid_spec=grid_spec,
        out_shape=jax.ShapeDtypeStruct((B, LT, D), BF16),
        compiler_params=_cparams(3),
        name="diff_attn",
    )(lam, q, k, v, subln_w.reshape(1, DA_HEAD_W))


def _route_block(logits, cnt_sc):
    lane = lax.broadcasted_iota(jnp.int32, logits.shape, 1)
    lane_f = lane.astype(F32)
    work = jnp.where(lane < N_EXPERTS, logits, -jnp.inf)
    tops, idxs, hits = [], [], []
    for _ in range(TOP_K):
        mk = jnp.max(work, axis=1, keepdims=True)
        ik = jnp.min(jnp.where(work == mk, lane_f, float(LANES)), axis=1, keepdims=True)
        hit = lane_f == ik
        tops.append(mk)
        idxs.append(ik)
        hits.append(hit)
        work = jnp.where(hit, -jnp.inf, work)
    chosen = functools.reduce(jnp.logical_or, hits).astype(F32)
    n = logits.shape[0]
    row = lax.broadcasted_iota(jnp.int32, (n, n), 0)
    col = lax.broadcasted_iota(jnp.int32, (n, n), 1)
    before = jnp.dot((col < row).astype(BF16), chosen.astype(BF16), preferred_element_type=F32)
    rank_all = cnt_sc[...] + before
    cnt_sc[...] = cnt_sc[...] + jnp.sum(chosen, axis=0, keepdims=True)
    exps = [jnp.exp(t - tops[0]) for t in tops]
    denom = functools.reduce(jnp.add, exps)
    table = jnp.zeros(logits.shape, F32)
    for k in range(TOP_K):
        rk = jnp.sum(jnp.where(hits[k], rank_all, 0.0), axis=1, keepdims=True)
        table = jnp.where(lane == k, idxs[k], table)
        table = jnp.where(lane == TOP_K + k, rk, table)
        table = jnp.where(lane == 2 * TOP_K + k, exps[k] / denom, table)
    return table


def _post_mixer(pre, w_ref, x_ref, xt_ref, mod_ref, ln_ref, rw_ref, rb_ref, x1_ref, u_ref, rt_ref,
                cnt_ref, cnt_sc, n_main):
    @pl.when((pl.program_id(0) == 0) & (pl.program_id(1) == 0))
    def _():
        cnt_sc[...] = jnp.zeros(cnt_sc.shape, F32)

    m = mod_ref[0]
    y = jnp.dot(pre, w_ref[...], preferred_element_type=F32)
    x_res = jnp.where(pl.program_id(1) < n_main, x_ref[0], xt_ref[0])
    z = DEEPNORM_ALPHA * x_res + m[MOD_G1:MOD_G1 + 1] * y
    mu = jnp.mean(z, axis=-1, keepdims=True)
    zc = z - mu
    x1 = zc * lax.rsqrt(jnp.mean(zc * zc, axis=-1, keepdims=True) + NORM_EPS)
    x1 = x1 * ln_ref[0:1] + ln_ref[1:2]
    x1_ref[0] = x1
    u = x1 * m[MOD_SC2:MOD_SC2 + 1] + m[MOD_SH2:MOD_SH2 + 1]
    u_hi = u.astype(BF16)
    half = u.shape[1] // 2
    bits = lax.bitcast_convert_type(u_hi.astype(F32), jnp.uint32)
    u_ref[0] = (bits[:, :half] >> 16) | (bits[:, half:] & jnp.uint32(0xFFFF0000))
    u_lo = (u - u_hi.astype(F32)).astype(BF16)
    d_hi = jnp.dot(u_hi, rw_ref[...], preferred_element_type=F32)
    d_lo = jnp.dot(u_lo, rw_ref[:, :LANES], preferred_element_type=F32)
    logits = d_hi[:, :LANES] + d_hi[:, LANES:] + d_lo + rb_ref[...]
    rt_ref[0] = _route_block(logits, cnt_sc)
    cnt_ref[...] = jnp.broadcast_to(cnt_sc[...], cnt_ref.shape)


def _out_da_kernel(o_ref, w_ref, x_ref, xt_ref, mod_ref, ln_ref, rw_ref, rb_ref, x1_ref, u_ref, rt_ref,
                   cnt_ref, cnt_sc, *, n_main):
    _post_mixer(o_ref[0], w_ref, x_ref, xt_ref, mod_ref, ln_ref, rw_ref, rb_ref, x1_ref, u_ref, rt_ref,
                cnt_ref, cnt_sc, n_main)


def _out_gla_kernel(of_ref, ob_ref, r_ref, nw_ref, w_ref, x_ref, xt_ref, mod_ref, ln_ref, rw_ref, rb_ref,
                    x1_ref, u_ref, rt_ref, cnt_ref, cnt_sc, *, n_main):
    parts = []
    for h in range(GLA_HEADS):
        sl = slice(h * GLA_DV_HEAD, (h + 1) * GLA_DV_HEAD)
        o = of_ref[0, :, sl] + ob_ref[0, :, sl]
        o = o * lax.rsqrt(jnp.mean(o * o, axis=-1, keepdims=True) + NORM_EPS) * nw_ref[...]
        r = r_ref[0, :, sl]
        parts.append((o * (r * jax.nn.sigmoid(r))).astype(BF16))
    pre = jnp.concatenate(parts, axis=1)
    _post_mixer(pre, w_ref, x_ref, xt_ref, mod_ref, ln_ref, rw_ref, rb_ref, x1_ref, u_ref, rt_ref,
                cnt_ref, cnt_sc, n_main)


def _mixer_out(kind, acts, w_out, x_main, x_tail, nbl, tail_blk, mods, lnp, rw, rb, nb_out, norm_w=None):
    B, _, D = x_main.shape
    blk = pl.BlockSpec((1, TM, D), lambda b, i: (b, i, 0))
    common_specs = [pl.BlockSpec((D, D), lambda b, i: (0, 0)),
                    pl.BlockSpec((1, TM, D), lambda b, i: (b, jnp.minimum(i, nbl - 1), 0)),
                    pl.BlockSpec((1, TM, D), lambda b, i: (b, tail_blk, 0)),
                    pl.BlockSpec((1, 8, D), lambda b, i: (2 * b + i // nbl, 0, 0)),
                    pl.BlockSpec((2, D), lambda b, i: (0, 0)),
                    pl.BlockSpec((D, 2 * LANES), lambda b, i: (0, 0)),
                    pl.BlockSpec((1, LANES), lambda b, i: (0, 0))]
    lout = nb_out * TM
    out_shape = [jax.ShapeDtypeStruct((B, lout, D), F32),
                 jax.ShapeDtypeStruct((B, lout, D // 2), jnp.uint32),
                 jax.ShapeDtypeStruct((B, lout, LANES), F32),
                 jax.ShapeDtypeStruct((8, LANES), F32)]
    out_specs = [blk, pl.BlockSpec((1, TM, D // 2), lambda b, i: (b, i, 0)),
                 pl.BlockSpec((1, TM, LANES), lambda b, i: (b, i, 0)),
                 pl.BlockSpec((8, LANES), lambda b, i: (0, 0))]
    if kind == "da":
        kern = _out_da_kernel
        in_specs = [blk] + common_specs
        args = list(acts)
    else:
        kern = _out_gla_kernel
        in_specs = [blk, blk, blk, pl.BlockSpec((1, GLA_DV_HEAD), lambda b, i: (0, 0))] + common_specs
        args = list(acts) + [norm_w.reshape(1, GLA_DV_HEAD)]
    return pl.pallas_call(
        functools.partial(kern, n_main=nbl),
        grid=(B, nb_out),
        in_specs=in_specs,
        out_specs=out_specs,
        out_shape=out_shape,
        scratch_shapes=[pltpu.VMEM((1, LANES), F32)],
        compiler_params=_cparams(2),
        name="mixer_out_" + kind,
    )(*args, w_out, x_main, x_tail, mods, lnp, rw, rb)


def _proj_gla_kernel(x_ref, mod_ref, w_ref, wz_ref, wg_ref, bg_ref,
                     q_ref, k_ref, v_ref, r_ref, g_ref):
    m = mod_ref[0]
    t = (x_ref[0] * m[MOD_SC1:MOD_SC1 + 1] + m[MOD_SH1:MOD_SH1 + 1]).astype(BF16)
    c0, c1, c2, c3 = GLA_DK, 2 * GLA_DK, 2 * GLA_DK + GLA_DV, 2 * GLA_DK + 2 * GLA_DV
    q_ref[0] = jnp.dot(t, w_ref[:, :c0], preferred_element_type=F32) * (GLA_DK_HEAD ** -0.5)
    k_ref[0] = jnp.dot(t, w_ref[:, c0:c1], preferred_element_type=F32)
    v_ref[0] = jnp.dot(t, w_ref[:, c1:c2], preferred_element_type=F32).astype(BF16)
    r_ref[0] = jnp.dot(t, w_ref[:, c2:c3], preferred_element_type=F32)
    z = jnp.dot(t, wz_ref[...], preferred_element_type=F32)
    gl = jnp.dot(z, wg_ref[...], preferred_element_type=F32) + bg_ref[...]
    log_sig = jnp.minimum(gl, 0.0) - jnp.log1p(jnp.exp(-jnp.abs(gl)))
    g_ref[0] = log_sig * (1.0 / GLA_TAU)


def _proj_gla(xa, mods, w_main, wz, wg, bg):
    B, LT, D = xa.shape
    nb = LT // TM
    nbl = nb - 1
    blk = lambda w: pl.BlockSpec((1, TM, w), lambda b, i: (b, i, 0))
    return pl.pallas_call(
        _proj_gla_kernel,
        grid=(B, nb),
        in_specs=[blk(D),
                  pl.BlockSpec((1, 8, D), lambda b, i: (2 * b + i // nbl, 0, 0)),
                  pl.BlockSpec(w_main.shape, lambda b, i: (0, 0)),
                  pl.BlockSpec(wz.shape, lambda b, i: (0, 0)),
                  pl.BlockSpec(wg.shape, lambda b, i: (0, 0)),
                  pl.BlockSpec(bg.shape, lambda b, i: (0, 0))],
        out_specs=[blk(GLA_DK), blk(GLA_DK), blk(GLA_DV), blk(GLA_DV), blk(2 * GLA_DK)],
        out_shape=[jax.ShapeDtypeStruct((B, LT, GLA_DK), F32),
                   jax.ShapeDtypeStruct((B, LT, GLA_DK), F32),
                   jax.ShapeDtypeStruct((B, LT, GLA_DV), BF16),
                   jax.ShapeDtypeStruct((B, LT, GLA_DV), F32),
                   jax.ShapeDtypeStruct((B, LT, 2 * GLA_DK), F32)],
        compiler_params=_cparams(2),
        name="gla_proj",
    )(xa, mods, w_main, wz, wg, bg)


def _gla_scan_kernel(q_ref, k_ref, v_ref, g_ref, o_ref, st_sc, *, reverse):
    j = pl.program_id(1)

    @pl.when(j == 0)
    def _():
        st_sc[...] = jnp.zeros(st_sc.shape, F32)

    C = GLA_CHUNK
    n_chunks = TM // C

    def causal(n):
        row = lax.broadcasted_iota(jnp.int32, (n, n), 0)
        col = lax.broadcasted_iota(jnp.int32, (n, n), 1)
        return (row // C == col // C) & ((col >= row) if reverse else (col <= row))

    keep = causal(C)
    tri = causal(TM).astype(BF16)
    g = g_ref[0]
    g_hi = g.astype(BF16)
    rem = g - g_hi.astype(F32)
    g_mid = rem.astype(BF16)
    g_lo = (rem - g_mid.astype(F32)).astype(BF16)
    b_all = (jnp.dot(tri, g_hi, preferred_element_type=F32)
             + jnp.dot(tri, g_mid, preferred_element_type=F32)
             + jnp.dot(tri, g_lo, preferred_element_type=F32))

    states = [st_sc[h] for h in range(GLA_HEADS)]
    order = range(n_chunks - 1, -1, -1) if reverse else range(n_chunks)
    for c in order:
        rows = slice(c * C, (c + 1) * C)
        for h in range(GLA_HEADS):
            ks = slice(h * GLA_DK_HEAD, (h + 1) * GLA_DK_HEAD)
            vs = slice(h * GLA_DV_HEAD, (h + 1) * GLA_DV_HEAD)
            b = b_all[rows, ks]
            tot = b[0:1] if reverse else b[C - 1:C]
            q = q_ref[0, rows, ks]
            k = k_ref[0, rows, ks]
            q_in = (q * jnp.exp(b)).astype(BF16)
            k_in = (k * jnp.exp(-b)).astype(BF16)
            k_st = (k * jnp.exp(tot - b)).astype(BF16)
            att = lax.dot_general(q_in, k_in, (((1,), (1,)), ((), ())), preferred_element_type=F32)
            att = jnp.where(keep, att, 0.0).astype(BF16)
            v = v_ref[0, rows, vs]
            st = states[h]
            o = jnp.dot(att, v, preferred_element_type=F32)
            o = o + lax.dot_general(q_in, st.astype(BF16), (((1,), (1,)), ((), ())),
                                    preferred_element_type=F32)
            o_ref[0, rows, vs] = o
            ds = lax.dot_general(v, k_st, (((0,), (0,)), ((), ())), preferred_element_type=F32)
            states[h] = st * jnp.exp(tot) + ds
    for h in range(GLA_HEADS):
        st_sc[h] = states[h]


def _gla_scan(q, k, v, g, reverse):
    B, LT, _ = q.shape
    nb = LT // TM
    ctx_blk = nb - 1
    if reverse:
        order = lambda j: jnp.where(j == 0, ctx_blk, ctx_blk - j)
    else:
        order = lambda j: jnp.where(j == 0, ctx_blk, j - 1)
    gcol = 1 if reverse else 0
    return pl.pallas_call(
        functools.partial(_gla_scan_kernel, reverse=reverse),
        grid=(B, nb),
        in_specs=[pl.BlockSpec((1, TM, GLA_DK), lambda b, j: (b, order(j), 0)),
                  pl.BlockSpec((1, TM, GLA_DK), lambda b, j: (b, order(j), 0)),
                  pl.BlockSpec((1, TM, GLA_DV), lambda b, j: (b, order(j), 0)),
                  pl.BlockSpec((1, TM, GLA_DK), lambda b, j: (b, order(j), gcol))],
        out_specs=pl.BlockSpec((1, TM, GLA_DV), lambda b, j: (b, order(j), 0)),
        out_shape=jax.ShapeDtypeStruct((B, LT, GLA_DV), F32),
        scratch_shapes=[pltpu.VMEM((GLA_HEADS, GLA_DV_HEAD, GLA_DK_HEAD), F32)],
        compiler_params=_cparams(2),
        name="gla_scan_bwd" if reverse else "gla_scan_fwd",
    )(q, k, v, g)


def _moe_kernel(pe_ref, nu_ref, tok_ref, tokn_ref, u_hbm, wgu_ref, bgu_ref, wd_ref, bd_ref, y_ref,
                wgu_sc, wd_sc, x0, x1, x2, x3, gsem):
    j = pl.program_id(0)
    n_used = nu_ref[0]
    H = MOE_BLOCK
    xbuf = (x0, x1, x2, x3)

    def gather(tab_ref, half, buf):
        for r in range(H):
            t = tab_ref[0, 0, half * H + r]
            pltpu.make_async_copy(u_hbm.at[pl.ds(t, 1), :], xbuf[buf].at[pl.ds(r, 1), :],
                                  gsem.at[buf]).start(priority=r % 2)

    def gather_wait(buf):
        pltpu.make_async_copy(u_hbm.at[pl.ds(0, H), :], xbuf[buf], gsem.at[buf]).wait()

    def ffn(half, buf):
        words = xbuf[buf][...]
        x = jnp.concatenate(
            [lax.bitcast_convert_type(words << 16, F32),
             lax.bitcast_convert_type(words & jnp.uint32(0xFFFF0000), F32)], axis=1).astype(BF16)
        gu = jnp.dot(x, wgu_sc[...], preferred_element_type=F32)
        gu = gu + bgu_ref[0, 0]
        mid = gu.shape[1] // 2
        glu = jnp.minimum(gu[:, :mid], SWIGLU_LIMIT)
        lin = jnp.clip(gu[:, mid:], -SWIGLU_LIMIT, SWIGLU_LIMIT)
        act = glu * jax.nn.sigmoid(SWIGLU_ALPHA * glu) * (lin + 1.0)
        y = jnp.dot(act.astype(BF16), wd_sc[...], preferred_element_type=F32) + bd_ref[0, 0]
        y_ref[half * H:(half + 1) * H, :] = y.astype(y_ref.dtype)

    def pair_step(cur, nxt):
        gather_wait(cur)
        gather(tokn_ref, 0, nxt)
        ffn(0, cur)
        gather_wait(cur + 1)
        gather(tokn_ref, 1, nxt + 1)
        ffn(1, cur + 1)

        @pl.when(j == n_used - 1)
        def _():
            gather_wait(nxt)
            gather_wait(nxt + 1)

    @pl.when(j < n_used)
    def _():
        @pl.when(j == 0)
        def _():
            gather(tok_ref, 0, 0)
            gather(tok_ref, 1, 1)

        @pl.when((j == 0) | (pe_ref[j] != pe_ref[jnp.maximum(j - 1, 0)]))
        def _():
            wgu_sc[...] = wgu_ref[0, 0].astype(BF16)
            wd_sc[...] = wd_ref[0, 0].astype(BF16)

        @pl.when(j % 2 == 0)
        def _():
            pair_step(0, 2)

        @pl.when(j % 2 == 1)
        def _():
            pair_step(2, 0)

    @pl.when(j >= n_used)
    def _():
        y_ref[...] = jnp.zeros(y_ref.shape, y_ref.dtype)


def _moe_experts(layer, pair_expert, n_used, tok_tab, u, w_gu, b_gu, w_down, b_down):
    n_pairs = tok_tab.shape[0]
    D = D_MODEL
    pair = 2 * MOE_BLOCK
    tab = lambda f: pl.BlockSpec((1, 1, pair), f, memory_space=pltpu.SMEM)
    grid_spec = pltpu.PrefetchScalarGridSpec(
        num_scalar_prefetch=2,
        grid=(n_pairs,),
        in_specs=[tab(lambda j, pe, nu: (j, 0, 0)),
                  tab(lambda j, pe, nu: (jnp.minimum(j + 1, n_pairs - 1), 0, 0)),
                  pl.BlockSpec(memory_space=pl.ANY),
                  pl.BlockSpec((1, 1, D, 2 * D), lambda j, pe, nu: (layer, pe[j], 0, 0)),
                  pl.BlockSpec((1, 1, 1, 2 * D), lambda j, pe, nu: (layer, pe[j], 0, 0)),
                  pl.BlockSpec((1, 1, D, D), lambda j, pe, nu: (layer, pe[j], 0, 0)),
                  pl.BlockSpec((1, 1, 1, D), lambda j, pe, nu: (layer, pe[j], 0, 0))],
        out_specs=pl.BlockSpec((pair, D), lambda j, pe, nu: (j, 0)),
        scratch_shapes=[pltpu.VMEM((D, 2 * D), BF16), pltpu.VMEM((D, D), BF16),
                        *([pltpu.VMEM((MOE_BLOCK, D // 2), jnp.uint32)] * 4),
                        pltpu.SemaphoreType.DMA((4,))],
    )
    return pl.pallas_call(
        _moe_kernel,
        grid_spec=grid_spec,
        out_shape=jax.ShapeDtypeStruct((n_pairs * pair, D), BF16),
        compiler_params=pltpu.CompilerParams(dimension_semantics=("arbitrary",),
                                             vmem_limit_bytes=MOE_VMEM_LIMIT),
        name="moe_experts",
    )(pair_expert, n_used, tok_tab, tok_tab, u, w_gu, b_gu.reshape(DEPTH, N_EXPERTS, 1, 2 * D),
      w_down, b_down.reshape(DEPTH, N_EXPERTS, 1, D))


def _moe_ffn(layer, u, route, counts, w_gu, b_gu, w_down, b_down):
    T, D = u.shape[0], D_MODEL
    pair = 2 * MOE_BLOCK
    n_assign = T * TOP_K
    expert = route[:, :TOP_K].astype(jnp.int32)
    rank = route[:, TOP_K:2 * TOP_K].astype(jnp.int32)
    cnt = counts[0, :N_EXPERTS].astype(jnp.int32)
    padded = (cnt + pair - 1) // pair * pair
    padded_end = jnp.cumsum(padded)
    padded_start = padded_end - padded
    lanes_e = jnp.arange(N_EXPERTS, dtype=jnp.int32)
    start_of = jnp.sum(jnp.where(expert[..., None] == lanes_e, padded_start, 0), axis=-1)
    pos = (start_of + rank).T
    n_pairs = -(-n_assign // pair) + N_EXPERTS
    pair_start = jnp.arange(n_pairs, dtype=jnp.int32) * pair
    pair_expert = jnp.minimum(jnp.sum(padded_end[None, :] <= pair_start[:, None], axis=1),
                              N_EXPERTS - 1).astype(jnp.int32)
    n_used = (padded_end[-1:] // pair).astype(jnp.int32)
    tok = jnp.tile(jnp.arange(T, dtype=jnp.int32), TOP_K)
    _, sorted_tok = lax.sort((pos.reshape(-1), tok), num_keys=1)
    cnt_p = cnt[pair_expert][:, None]
    rank_p = (pair_start - padded_start[pair_expert])[:, None] + jnp.arange(pair, dtype=jnp.int32)
    compact = (jnp.cumsum(cnt) - cnt)[pair_expert][:, None] + rank_p
    tok_tab = jnp.where(rank_p < cnt_p, sorted_tok[jnp.minimum(compact, n_assign - 1)], 0)
    yb = _moe_experts(layer, pair_expert, n_used, tok_tab.reshape(n_pairs, 1, pair), u, w_gu, b_gu,
                      w_down, b_down)
    return yb[pos.reshape(-1)].reshape(TOP_K, T, D)


def _final_ln_kernel(x_ref, y_ref, rt_ref, mod_ref, ln_ref, o_ref):
    m = mod_ref[0]
    rt = rt_ref[0]
    f = rt[:, 2 * TOP_K:2 * TOP_K + 1] * y_ref[0].astype(F32)
    for k in range(1, TOP_K):
        f = f + rt[:, 2 * TOP_K + k:2 * TOP_K + k + 1] * y_ref[k].astype(F32)
    z = DEEPNORM_ALPHA * x_ref[0] + m[MOD_G2:MOD_G2 + 1] * f
    mu = jnp.mean(z, axis=-1, keepdims=True)
    zc = z - mu
    y = zc * lax.rsqrt(jnp.mean(zc * zc, axis=-1, keepdims=True) + NORM_EPS)
    o_ref[0] = y * ln_ref[0:1] + ln_ref[1:2]


def _final_ln(x1, yg, route, mods, lnp, nbl):
    B, L, D = x1.shape
    nblk = L // TM
    blk = pl.BlockSpec((1, TM, D), lambda b, i: (b, i, 0))
    return pl.pallas_call(
        _final_ln_kernel,
        grid=(B, nblk),
        in_specs=[blk,
                  pl.BlockSpec((TOP_K, TM, D), lambda b, i: (0, b * nblk + i, 0)),
                  pl.BlockSpec((1, TM, LANES), lambda b, i: (b, i, 0)),
                  pl.BlockSpec((1, 8, D), lambda b, i: (2 * b + i // nbl, 0, 0)),
                  pl.BlockSpec((2, D), lambda b, i: (0, 0))],
        out_specs=blk,
        out_shape=jax.ShapeDtypeStruct((B, L, D), F32),
        compiler_params=_cparams(2),
        name="final_ln",
    )(x1, yg, route, mods, lnp)


def _rope_tables(S, n_ctx):
    rows = S // GRID_W
    row = jnp.repeat(jnp.arange(rows), GRID_W).astype(F32)
    col = jnp.tile(jnp.arange(GRID_W), rows).astype(F32)
    inv = ROPE_BASE ** (-jnp.arange(ROPE_PAIRS_AXIS, dtype=F32) / ROPE_PAIRS_AXIS)
    ang = jnp.concatenate([row[:, None] * inv, col[:, None] * inv], -1)
    cos, sin = jnp.cos(ang), jnp.sin(ang)
    cos = jnp.concatenate([cos, jnp.ones((n_ctx, cos.shape[1]), F32)], axis=0)
    sin = jnp.concatenate([sin, jnp.zeros((n_ctx, sin.shape[1]), F32)], axis=0)
    return (jnp.concatenate([cos, cos, cos, cos], axis=1),
            jnp.concatenate([-sin, -sin, sin, sin], axis=1))


def _qk_column_perm():
    lane = np.arange(DA_HEAD_W)
    half, mp, jj = lane // 64, (lane % 64) // 32, lane % 32
    src = mp * DA_HEAD_DIM + half * 32 + jj
    head = np.arange(DA_HEADS)[:, None] * DA_HEAD_W
    perm = (head + src[None, :]).reshape(-1)
    return np.concatenate([perm, D_MODEL + perm])


def _split_router_w(w):
    w = jnp.pad(w, ((0, 0), (0, LANES - N_EXPERTS)))
    hi = w.astype(BF16)
    lo = (w - hi.astype(F32)).astype(BF16)
    return jnp.concatenate([hi, lo], axis=1)


def _layer_mods(mod_rows, B):
    D = D_MODEL
    parts = mod_rows.reshape(8, 6, D)
    sh1, sc1, g1, sh2, sc2, g2 = (parts[:, n] for n in range(6))
    tab = jnp.stack([1.0 + sc1, sh1, g1, 1.0 + sc2, sh2, g2, jnp.zeros_like(g1),
                     jnp.zeros_like(g1)], axis=1)
    lat = tab[:B]
    ctx = jnp.broadcast_to(tab[B:B + 1], (B, 8, D))
    return jnp.stack([lat, ctx], axis=1).reshape(2 * B, 8, D)


def kernel(x, c, ctx, c_ctx, ada_w, ada_b, ln_g, ln_b, da_w_in, da_w_out, da_lambda, da_subln_w,
           gla_w_in, gla_w_gate, gla_b_gate, gla_norm_w, gla_w_out, router_w, router_b,
           moe_w_gu, moe_b_gu, moe_w_down, moe_b_down):
    B, S, D = x.shape
    n_ctx = ctx.shape[1]
    assert D == D_MODEL and n_ctx == TM and S % TM == 0 and S % GRID_W == 0 and B + 1 <= 8
    nbl = S // TM
    nb = nbl + 1

    cc = jnp.concatenate([c, c_ctx[None, :], jnp.zeros((8 - B - 1, D), F32)], axis=0)
    mod_all = _ada_mods(cc, ada_w, ada_b)

    rw = [_split_router_w(router_w[i]) for i in range(DEPTH)]
    rb = [jnp.pad(router_b[i], (0, LANES - N_EXPERTS)).reshape(1, LANES) for i in range(DEPTH)]
    lnp = [[jnp.stack([ln_g[i, n], ln_b[i, n]]) for n in range(2)] for i in range(DEPTH)]

    mods = _layer_mods(mod_all[0], B)
    w_in = da_w_in[0]
    wqk = w_in[:, _qk_column_perm()].astype(BF16)
    wv = w_in[:, 2 * D:].astype(BF16)
    cos, sin = _rope_tables(S, n_ctx)
    q, k, v = _proj_da(x, ctx, mods, cos, sin, wqk, wv)
    lam_init = _lambda_init(0)
    lv = da_lambda[0].astype(F32)
    lam = (jnp.exp(jnp.sum(lv[0] * lv[1])) - jnp.exp(jnp.sum(lv[2] * lv[3])) + lam_init).reshape(1)
    o = _diff_attention(lam, q, k, v, da_subln_w[0], lam_init)
    x1, u, route, counts = _mixer_out("da", [o], da_w_out[0].astype(BF16), x, ctx, nbl, 0, mods,
                                      lnp[0][0], rw[0], rb[0], nb)
    yg = _moe_ffn(0, u.reshape(B * nb * TM, D // 2), route.reshape(B * nb * TM, LANES), counts,
                  moe_w_gu, moe_b_gu, moe_w_down, moe_b_down)
    xa = _final_ln(x1, yg, route, mods, lnp[0][1], nbl)

    mods = _layer_mods(mod_all[1], B)
    gw = gla_w_in[0]
    c3 = 2 * GLA_DK + 2 * GLA_DV
    w_main = gw[:, :c3].astype(BF16)
    wz = jnp.pad(gw[:, c3:], ((0, 0), (0, LANES - 2 * GLA_GATE_RANK))).astype(BF16)
    wg = jnp.zeros((LANES, 2 * GLA_DK), F32)
    wg = wg.at[:GLA_GATE_RANK, :GLA_DK].set(gla_w_gate[0, 0])
    wg = wg.at[GLA_GATE_RANK:2 * GLA_GATE_RANK, GLA_DK:].set(gla_w_gate[0, 1])
    bg = gla_b_gate[0].reshape(1, 2 * GLA_DK)
    gq, gk, gv, gr, gg = _proj_gla(xa, mods, w_main, wz, wg, bg)
    of = _gla_scan(gq, gk, gv, gg, reverse=False)
    ob = _gla_scan(gq, gk, gv, gg, reverse=True)
    x1, u, route, counts = _mixer_out("gla", [of, ob, gr], gla_w_out[0].astype(BF16), xa, xa, nbl, nbl,
                                      mods, lnp[1][0], rw[1], rb[1], nbl, norm_w=gla_norm_w[0])
    yg = _moe_ffn(1, u.reshape(B * S, D // 2), route.reshape(B * S, LANES), counts,
                  moe_w_gu, moe_b_gu, moe_w_down, moe_b_down)
    return _final_ln(x1, yg, route, mods, lnp[1][1], nbl)
```

```python
import functools
import math

import numpy as np
import jax
import jax.numpy as jnp
from jax import lax
from jax.experimental import pallas as pl
from jax.experimental.pallas import tpu as pltpu

F32 = jnp.float32
BF16 = jnp.bfloat16

D_MODEL = 1024
DEPTH = 2
GRID_W = 64

DA_HEADS = 8
DA_HEAD_DIM = 64
DA_HEAD_W = 2 * DA_HEAD_DIM
ROPE_BASE = 10000.0
ROPE_PAIRS_AXIS = DA_HEAD_DIM // 4
ROPE_HALF = DA_HEAD_DIM // 2

GLA_HEADS = 4
GLA_DK = D_MODEL // 2
GLA_DV = D_MODEL
GLA_DK_HEAD = GLA_DK // GLA_HEADS
GLA_DV_HEAD = GLA_DV // GLA_HEADS
GLA_GATE_RANK = 16
GLA_TAU = 16.0
GLA_CHUNK = 64

N_EXPERTS = 32
TOP_K = 4
SWIGLU_ALPHA = 1.702
SWIGLU_LIMIT = 7.0
MOE_BLOCK = 256

DEEPNORM_ALPHA = (2.0 * DEPTH) ** 0.25
NORM_EPS = 1e-5

LANES = 128
TM = 256
ATT_TK_MAX = 2816
VMEM_LIMIT = 48 * 1024 * 1024
MOE_VMEM_LIMIT = 56 * 1024 * 1024

MOD_SC1, MOD_SH1, MOD_G1, MOD_SC2, MOD_SH2, MOD_G2 = range(6)


def _cparams(n_axes):
    return pltpu.CompilerParams(dimension_semantics=("arbitrary",) * n_axes,
                                vmem_limit_bytes=VMEM_LIMIT)


def _lambda_init(layer_idx):
    return 0.8 - 0.6 * math.exp(-0.3 * layer_idx)


def _ada_kernel(c_ref, w_ref, b_ref, o_ref):
    c = c_ref[...]
    s = c * jax.nn.sigmoid(c)
    o_ref[0] = jnp.dot(s, w_ref[0], preferred_element_type=F32) + b_ref[0]


def _ada_mods(cc, ada_w, ada_b):
    n6 = 6 * D_MODEL
    nt = n6 // 4
    return pl.pallas_call(
        _ada_kernel,
        grid=(DEPTH, n6 // nt),
        in_specs=[pl.BlockSpec((8, D_MODEL), lambda l, j: (0, 0)),
                  pl.BlockSpec((1, D_MODEL, nt), lambda l, j: (l, 0, j)),
                  pl.BlockSpec((1, 1, nt), lambda l, j: (l, 0, j))],
        out_specs=pl.BlockSpec((1, 8, nt), lambda l, j: (l, 0, j)),
        out_shape=jax.ShapeDtypeStruct((DEPTH, 8, n6), F32),
        compiler_params=_cparams(2),
        name="ada_mods",
    )(cc, ada_w, ada_b.reshape(DEPTH, 1, n6))


def _proj_da_kernel(x_ref, c_ref, mod_ref, cos_ref, sin_ref, wqk_ref, wv_ref, q_ref, k_ref, v_ref):
    m = mod_ref[0]
    xin = jnp.where(pl.program_id(1) < pl.num_programs(1) - 1, x_ref[0], c_ref[0])
    t = (xin * m[MOD_SC1:MOD_SC1 + 1] + m[MOD_SH1:MOD_SH1 + 1]).astype(BF16)
    cos = cos_ref[...]
    sin = sin_ref[...]
    q_scale = DA_HEAD_DIM ** -0.5 * math.log2(math.e)
    for j in range(DA_HEADS):
        y2 = jnp.dot(t, wqk_ref[:, j * 2 * DA_HEAD_W:(j + 1) * 2 * DA_HEAD_W],
                     preferred_element_type=F32)
        for hh in range(2):
            y = y2[:, hh * LANES:(hh + 1) * LANES]
            y = y * cos + pltpu.roll(y, DA_HEAD_W // 2, 1) * sin
            col = (2 * j + hh) * LANES
            if col < D_MODEL:
                q_ref[0, :, col:col + LANES] = (y * q_scale).astype(BF16)
            else:
                k_ref[0, :, col - D_MODEL:col - D_MODEL + LANES] = y.astype(BF16)
    v_ref[0] = jnp.dot(t, wv_ref[...], preferred_element_type=F32).astype(BF16)


def _proj_da(x, ctx, mods, cos, sin, wqk, wv):
    B, S, D = x.shape
    nbl = S // TM
    nb = nbl + 1
    out = jax.ShapeDtypeStruct((B, nb * TM, D), BF16)
    blk = pl.BlockSpec((1, TM, D), lambda b, i: (b, i, 0))
    return pl.pallas_call(
        _proj_da_kernel,
        grid=(B, nb),
        in_specs=[pl.BlockSpec((1, TM, D), lambda b, i: (b, jnp.minimum(i, nbl - 1), 0)),
                  pl.BlockSpec((1, TM, D), lambda b, i: (b, 0, 0)),
                  pl.BlockSpec((1, 8, D), lambda b, i: (2 * b + i // nbl, 0, 0)),
                  pl.BlockSpec((TM, LANES), lambda b, i: (i, 0)),
                  pl.BlockSpec((TM, LANES), lambda b, i: (i, 0)),
                  pl.BlockSpec((D, 2 * D), lambda b, i: (0, 0)),
                  pl.BlockSpec((D, D), lambda b, i: (0, 0))],
        out_specs=[blk, blk, blk],
        out_shape=[out, out, out],
        compiler_params=_cparams(2),
        name="da_proj",
    )(x, ctx, mods, cos, sin, wqk, wv)


def _attn_kernel(lam_ref, q_ref, k_ref, v_ref, sw_ref, o_ref, vext_sc, m_sc, acc_sc, s_sc, *,
                 n_lat, n_ctx, lam_init):
    i = pl.program_id(2)
    tq = q_ref.shape[1]

    @pl.when(i == 0)
    def _():
        vext_sc[:, :DA_HEAD_W] = v_ref[0]
        vext_sc[:, DA_HEAD_W:] = jnp.ones((vext_sc.shape[0], DA_HEAD_W), BF16)

    q = q_ref[0]
    lane = lax.broadcasted_iota(jnp.int32, (1, DA_HEAD_W), 1)
    map0 = (lane % (DA_HEAD_W // 2)) < ROPE_HALF
    zero = jnp.zeros_like(q)
    qs = jnp.concatenate([jnp.where(map0, q, zero), jnp.where(map0, zero, q)], axis=0)

    def scores(off, tk):
        k = k_ref[0, pl.ds(off, tk), :]
        return lax.dot_general(qs, k, (((1,), (1,)), ((), ())), preferred_element_type=F32)

    def accumulate(s, off, tk, first):
        m_cur = jnp.max(s, axis=1, keepdims=True)
        if first:
            m_new = jnp.broadcast_to(m_cur, m_sc.shape)
        else:
            m_prev = m_sc[...]
            m_new = jnp.maximum(m_prev, m_cur)
        p = jnp.exp2(s - jnp.tile(m_new, (1, tk // LANES)))
        pv = jnp.dot(p.astype(BF16), vext_sc[pl.ds(off, tk), :], preferred_element_type=F32)
        if first:
            acc_sc[...] = pv
        else:
            acc_sc[...] = jnp.tile(jnp.exp2(m_prev - m_new), (1, 2)) * acc_sc[...] + pv
        m_sc[...] = m_new

    n_q_lat = n_lat // tq
    tk = s_sc.shape[2]
    n_steps = (n_lat + n_ctx) // tk

    @pl.when(i < n_q_lat)
    def _():
        s_sc[0] = scores(0, tk)
        for t in range(n_steps):
            if t + 1 < n_steps:
                s_sc[(t + 1) % 2] = scores((t + 1) * tk, tk)
            accumulate(s_sc[t % 2], t * tk, tk, t == 0)

    @pl.when(i >= n_q_lat)
    def _():
        accumulate(scores(n_lat, n_ctx), n_lat, n_ctx, True)

    acc = acc_sc[...]
    o0 = acc[:tq, :DA_HEAD_W] / acc[:tq, DA_HEAD_W:DA_HEAD_W + 1]
    o1 = acc[tq:, :DA_HEAD_W] / acc[tq:, DA_HEAD_W:DA_HEAD_W + 1]
    o = o0 - lam_ref[0] * o1
    o = o * lax.rsqrt(jnp.mean(o * o, axis=-1, keepdims=True) + NORM_EPS)
    o_ref[0] = (o * sw_ref[...] * (1.0 - lam_init)).astype(BF16)


def _diff_attention(lam, q, k, v, subln_w, lam_init):
    B, LT, D = q.shape
    nb = LT // TM
    tk = max(t for t in range(TM, ATT_TK_MAX + 1, TM) if LT % t == 0)
    kern = functools.partial(_attn_kernel, n_lat=LT - TM, n_ctx=TM, lam_init=lam_init)
    grid_spec = pltpu.PrefetchScalarGridSpec(
        num_scalar_prefetch=1,
        grid=(B, DA_HEADS, nb),
        in_specs=[pl.BlockSpec((1, TM, DA_HEAD_W), lambda b, h, i, lam: (b, i, h)),
                  pl.BlockSpec((1, LT, DA_HEAD_W), lambda b, h, i, lam: (b, 0, h)),
                  pl.BlockSpec((1, LT, DA_HEAD_W), lambda b, h, i, lam: (b, 0, h)),
                  pl.BlockSpec((1, DA_HEAD_W), lambda b, h, i, lam: (0, 0))],
        out_specs=pl.BlockSpec((1, TM, DA_HEAD_W), lambda b, h, i, lam: (b, i, h)),
        scratch_shapes=[pltpu.VMEM((LT, 2 * DA_HEAD_W), BF16),
                        pltpu.VMEM((2 * TM, LANES), F32),
                        pltpu.VMEM((2 * TM, 2 * DA_HEAD_W), F32),
                        pltpu.VMEM((2, 2 * TM, tk), F32)],
    )
    return pl.pallas_call(
        kern,
        grid_spec=grid_spec,
        out_shape=jax.ShapeDtypeStruct((B, LT, D), BF16),
        compiler_params=_cparams(3),
        name="diff_attn",
    )(lam, q, k, v, subln_w.reshape(1, DA_HEAD_W))


def _route_block(logits, cnt_sc):
    lane = lax.broadcasted_iota(jnp.int32, logits.shape, 1)
    lane_f = lane.astype(F32)
    work = jnp.where(lane < N_EXPERTS, logits, -jnp.inf)
    tops, idxs, hits = [], [], []
    for _ in range(TOP_K):
        mk = jnp.max(work, axis=1, keepdims=True)
        ik = jnp.min(jnp.where(work == mk, lane_f, float(LANES)), axis=1, keepdims=True)
        hit = lane_f == ik
        tops.append(mk)
        idxs.append(ik)
        hits.append(hit)
        work = jnp.where(hit, -jnp.inf, work)
    chosen = functools.reduce(jnp.logical_or, hits).astype(F32)
    n = logits.shape[0]
    row = lax.broadcasted_iota(jnp.int32, (n, n), 0)
    col = lax.broadcasted_iota(jnp.int32, (n, n), 1)
    before = jnp.dot((col < row).astype(BF16), chosen.astype(BF16), preferred_element_type=F32)
    rank_all = cnt_sc[...] + before
    cnt_sc[...] = cnt_sc[...] + jnp.sum(chosen, axis=0, keepdims=True)
    exps = [jnp.exp(t - tops[0]) for t in tops]
    denom = functools.reduce(jnp.add, exps)
    table = jnp.zeros(logits.shape, F32)
    for k in range(TOP_K):
        rk = jnp.sum(jnp.where(hits[k], rank_all, 0.0), axis=1, keepdims=True)
        table = jnp.where(lane == k, idxs[k], table)
        table = jnp.where(lane == TOP_K + k, rk, table)
        table = jnp.where(lane == 2 * TOP_K + k, exps[k] / denom, table)
    return table


def _post_mixer(pre, w_ref, x_ref, xt_ref, mod_ref, ln_ref, rw_ref, rb_ref, x1_ref, u_ref, rt_ref,
                cnt_ref, cnt_sc, n_main):
    @pl.when((pl.program_id(0) == 0) & (pl.program_id(1) == 0))
    def _():
        cnt_sc[...] = jnp.zeros(cnt_sc.shape, F32)

    m = mod_ref[0]
    y = jnp.dot(pre, w_ref[...], preferred_element_type=F32)
    x_res = jnp.where(pl.program_id(1) < n_main, x_ref[0], xt_ref[0])
    z = DEEPNORM_ALPHA * x_res + m[MOD_G1:MOD_G1 + 1] * y
    mu = jnp.mean(z, axis=-1, keepdims=True)
    zc = z - mu
    x1 = zc * lax.rsqrt(jnp.mean(zc * zc, axis=-1, keepdims=True) + NORM_EPS)
    x1 = x1 * ln_ref[0:1] + ln_ref[1:2]
    x1_ref[0] = x1
    u = x1 * m[MOD_SC2:MOD_SC2 + 1] + m[MOD_SH2:MOD_SH2 + 1]
    u_ref[0] = u
    u_hi = u.astype(BF16)
    u_lo = (u - u_hi.astype(F32)).astype(BF16)
    d_hi = jnp.dot(u_hi, rw_ref[...], preferred_element_type=F32)
    d_lo = jnp.dot(u_lo, rw_ref[:, :LANES], preferred_element_type=F32)
    logits = d_hi[:, :LANES] + d_hi[:, LANES:] + d_lo + rb_ref[...]
    rt_ref[0] = _route_block(logits, cnt_sc)
    cnt_ref[...] = jnp.broadcast_to(cnt_sc[...], cnt_ref.shape)


def _out_da_kernel(o_ref, w_ref, x_ref, xt_ref, mod_ref, ln_ref, rw_ref, rb_ref, x1_ref, u_ref, rt_ref,
                   cnt_ref, cnt_sc, *, n_main):
    _post_mixer(o_ref[0], w_ref, x_ref, xt_ref, mod_ref, ln_ref, rw_ref, rb_ref, x1_ref, u_ref, rt_ref,
                cnt_ref, cnt_sc, n_main)


def _out_gla_kernel(of_ref, ob_ref, r_ref, nw_ref, w_ref, x_ref, xt_ref, mod_ref, ln_ref, rw_ref, rb_ref,
                    x1_ref, u_ref, rt_ref, cnt_ref, cnt_sc, *, n_main):
    parts = []
    for h in range(GLA_HEADS):
        sl = slice(h * GLA_DV_HEAD, (h + 1) * GLA_DV_HEAD)
        o = of_ref[0, :, sl] + ob_ref[0, :, sl]
        o = o * lax.rsqrt(jnp.mean(o * o, axis=-1, keepdims=True) + NORM_EPS) * nw_ref[...]
        r = r_ref[0, :, sl]
        parts.append((o * (r * jax.nn.sigmoid(r))).astype(BF16))
    pre = jnp.concatenate(parts, axis=1)
    _post_mixer(pre, w_ref, x_ref, xt_ref, mod_ref, ln_ref, rw_ref, rb_ref, x1_ref, u_ref, rt_ref,
                cnt_ref, cnt_sc, n_main)


def _mixer_out(kind, acts, w_out, x_main, x_tail, nbl, tail_blk, mods, lnp, rw, rb, nb_out, norm_w=None):
    B, _, D = x_main.shape
    blk = pl.BlockSpec((1, TM, D), lambda b, i: (b, i, 0))
    common_specs = [pl.BlockSpec((D, D), lambda b, i: (0, 0)),
                    pl.BlockSpec((1, TM, D), lambda b, i: (b, jnp.minimum(i, nbl - 1), 0)),
                    pl.BlockSpec((1, TM, D), lambda b, i: (b, tail_blk, 0)),
                    pl.BlockSpec((1, 8, D), lambda b, i: (2 * b + i // nbl, 0, 0)),
                    pl.BlockSpec((2, D), lambda b, i: (0, 0)),
                    pl.BlockSpec((D, 2 * LANES), lambda b, i: (0, 0)),
                    pl.BlockSpec((1, LANES), lambda b, i: (0, 0))]
    lout = nb_out * TM
    out_shape = [jax.ShapeDtypeStruct((B, lout, D), F32),
                 jax.ShapeDtypeStruct((B, lout, D), F32),
                 jax.ShapeDtypeStruct((B, lout, LANES), F32),
                 jax.ShapeDtypeStruct((8, LANES), F32)]
    out_specs = [blk, blk, pl.BlockSpec((1, TM, LANES), lambda b, i: (b, i, 0)),
                 pl.BlockSpec((8, LANES), lambda b, i: (0, 0))]
    if kind == "da":
        kern = _out_da_kernel
        in_specs = [blk] + common_specs
        args = list(acts)
    else:
        kern = _out_gla_kernel
        in_specs = [blk, blk, blk, pl.BlockSpec((1, GLA_DV_HEAD), lambda b, i: (0, 0))] + common_specs
        args = list(acts) + [norm_w.reshape(1, GLA_DV_HEAD)]
    return pl.pallas_call(
        functools.partial(kern, n_main=nbl),
        grid=(B, nb_out),
        in_specs=in_specs,
        out_specs=out_specs,
        out_shape=out_shape,
        scratch_shapes=[pltpu.VMEM((1, LANES), F32)],
        compiler_params=_cparams(2),
        name="mixer_out_" + kind,
    )(*args, w_out, x_main, x_tail, mods, lnp, rw, rb)


def _proj_gla_kernel(x_ref, mod_ref, w_ref, wz_ref, wg_ref, bg_ref,
                     q_ref, k_ref, v_ref, r_ref, g_ref):
    m = mod_ref[0]
    t = (x_ref[0] * m[MOD_SC1:MOD_SC1 + 1] + m[MOD_SH1:MOD_SH1 + 1]).astype(BF16)
    c0, c1, c2, c3 = GLA_DK, 2 * GLA_DK, 2 * GLA_DK + GLA_DV, 2 * GLA_DK + 2 * GLA_DV
    q_ref[0] = jnp.dot(t, w_ref[:, :c0], preferred_element_type=F32) * (GLA_DK_HEAD ** -0.5)
    k_ref[0] = jnp.dot(t, w_ref[:, c0:c1], preferred_element_type=F32)
    v_ref[0] = jnp.dot(t, w_ref[:, c1:c2], preferred_element_type=F32).astype(BF16)
    r_ref[0] = jnp.dot(t, w_ref[:, c2:c3], preferred_element_type=F32)
    z = jnp.dot(t, wz_ref[...], preferred_element_type=F32)
    gl = jnp.dot(z, wg_ref[...], preferred_element_type=F32) + bg_ref[...]
    log_sig = jnp.minimum(gl, 0.0) - jnp.log1p(jnp.exp(-jnp.abs(gl)))
    g_ref[0] = log_sig * (1.0 / GLA_TAU)


def _proj_gla(xa, mods, w_main, wz, wg, bg):
    B, LT, D = xa.shape
    nb = LT // TM
    nbl = nb - 1
    blk = lambda w: pl.BlockSpec((1, TM, w), lambda b, i: (b, i, 0))
    return pl.pallas_call(
        _proj_gla_kernel,
        grid=(B, nb),
        in_specs=[blk(D),
                  pl.BlockSpec((1, 8, D), lambda b, i: (2 * b + i // nbl, 0, 0)),
                  pl.BlockSpec(w_main.shape, lambda b, i: (0, 0)),
                  pl.BlockSpec(wz.shape, lambda b, i: (0, 0)),
                  pl.BlockSpec(wg.shape, lambda b, i: (0, 0)),
                  pl.BlockSpec(bg.shape, lambda b, i: (0, 0))],
        out_specs=[blk(GLA_DK), blk(GLA_DK), blk(GLA_DV), blk(GLA_DV), blk(2 * GLA_DK)],
        out_shape=[jax.ShapeDtypeStruct((B, LT, GLA_DK), F32),
                   jax.ShapeDtypeStruct((B, LT, GLA_DK), F32),
                   jax.ShapeDtypeStruct((B, LT, GLA_DV), BF16),
                   jax.ShapeDtypeStruct((B, LT, GLA_DV), F32),
                   jax.ShapeDtypeStruct((B, LT, 2 * GLA_DK), F32)],
        compiler_params=_cparams(2),
        name="gla_proj",
    )(xa, mods, w_main, wz, wg, bg)


def _gla_scan_kernel(q_ref, k_ref, v_ref, g_ref, o_ref, st_sc, *, reverse):
    j = pl.program_id(1)

    @pl.when(j == 0)
    def _():
        st_sc[...] = jnp.zeros(st_sc.shape, F32)

    C = GLA_CHUNK
    n_chunks = TM // C

    def causal(n):
        row = lax.broadcasted_iota(jnp.int32, (n, n), 0)
        col = lax.broadcasted_iota(jnp.int32, (n, n), 1)
        return (row // C == col // C) & ((col >= row) if reverse else (col <= row))

    keep = causal(C)
    tri = causal(TM).astype(BF16)
    g = g_ref[0]
    g_hi = g.astype(BF16)
    rem = g - g_hi.astype(F32)
    g_mid = rem.astype(BF16)
    g_lo = (rem - g_mid.astype(F32)).astype(BF16)
    b_all = (jnp.dot(tri, g_hi, preferred_element_type=F32)
             + jnp.dot(tri, g_mid, preferred_element_type=F32)
             + jnp.dot(tri, g_lo, preferred_element_type=F32))

    states = [st_sc[h] for h in range(GLA_HEADS)]
    order = range(n_chunks - 1, -1, -1) if reverse else range(n_chunks)
    for c in order:
        rows = slice(c * C, (c + 1) * C)
        for h in range(GLA_HEADS):
            ks = slice(h * GLA_DK_HEAD, (h + 1) * GLA_DK_HEAD)
            vs = slice(h * GLA_DV_HEAD, (h + 1) * GLA_DV_HEAD)
            b = b_all[rows, ks]
            tot = b[0:1] if reverse else b[C - 1:C]
            q = q_ref[0, rows, ks]
            k = k_ref[0, rows, ks]
            q_in = (q * jnp.exp(b)).astype(BF16)
            k_in = (k * jnp.exp(-b)).astype(BF16)
            k_st = (k * jnp.exp(tot - b)).astype(BF16)
            att = lax.dot_general(q_in, k_in, (((1,), (1,)), ((), ())), preferred_element_type=F32)
            att = jnp.where(keep, att, 0.0).astype(BF16)
            v = v_ref[0, rows, vs]
            st = states[h]
            o = jnp.dot(att, v, preferred_element_type=F32)
            o = o + lax.dot_general(q_in, st.astype(BF16), (((1,), (1,)), ((), ())),
                                    preferred_element_type=F32)
            o_ref[0, rows, vs] = o
            ds = lax.dot_general(v, k_st, (((0,), (0,)), ((), ())), preferred_element_type=F32)
            states[h] = st * jnp.exp(tot) + ds
    for h in range(GLA_HEADS):
        st_sc[h] = states[h]


def _gla_scan(q, k, v, g, reverse):
    B, LT, _ = q.shape
    nb = LT // TM
    ctx_blk = nb - 1
    if reverse:
        order = lambda j: jnp.where(j == 0, ctx_blk, ctx_blk - j)
    else:
        order = lambda j: jnp.where(j == 0, ctx_blk, j - 1)
    gcol = 1 if reverse else 0
    return pl.pallas_call(
        functools.partial(_gla_scan_kernel, reverse=reverse),
        grid=(B, nb),
        in_specs=[pl.BlockSpec((1, TM, GLA_DK), lambda b, j: (b, order(j), 0)),
                  pl.BlockSpec((1, TM, GLA_DK), lambda b, j: (b, order(j), 0)),
                  pl.BlockSpec((1, TM, GLA_DV), lambda b, j: (b, order(j), 0)),
                  pl.BlockSpec((1, TM, GLA_DK), lambda b, j: (b, order(j), gcol))],
        out_specs=pl.BlockSpec((1, TM, GLA_DV), lambda b, j: (b, order(j), 0)),
        out_shape=jax.ShapeDtypeStruct((B, LT, GLA_DV), F32),
        scratch_shapes=[pltpu.VMEM((GLA_HEADS, GLA_DV_HEAD, GLA_DK_HEAD), F32)],
        compiler_params=_cparams(2),
        name="gla_scan_bwd" if reverse else "gla_scan_fwd",
    )(q, k, v, g)


def _moe_kernel(pe_ref, nu_ref, tok_ref, tokn_ref, u_hbm, wgu_ref, bgu_ref, wd_ref, bd_ref, y_ref,
                wgu_sc, wd_sc, x0, x1, x2, x3, gsem):
    j = pl.program_id(0)
    n_used = nu_ref[0]
    H = MOE_BLOCK
    xbuf = (x0, x1, x2, x3)

    def gather(tab_ref, half, buf):
        for r in range(H):
            t = tab_ref[0, 0, half * H + r]
            pltpu.make_async_copy(u_hbm.at[pl.ds(t, 1), :], xbuf[buf].at[pl.ds(r, 1), :],
                                  gsem.at[buf]).start(priority=r % 2)

    def gather_wait(buf):
        pltpu.make_async_copy(u_hbm.at[pl.ds(0, H), :], xbuf[buf], gsem.at[buf]).wait()

    def ffn(half, buf):
        gu = jnp.dot(xbuf[buf][...].astype(BF16), wgu_sc[...], preferred_element_type=F32)
        gu = gu + bgu_ref[0, 0]
        mid = gu.shape[1] // 2
        glu = jnp.minimum(gu[:, :mid], SWIGLU_LIMIT)
        lin = jnp.clip(gu[:, mid:], -SWIGLU_LIMIT, SWIGLU_LIMIT)
        act = glu * jax.nn.sigmoid(SWIGLU_ALPHA * glu) * (lin + 1.0)
        y = jnp.dot(act.astype(BF16), wd_sc[...], preferred_element_type=F32) + bd_ref[0, 0]
        y_ref[half * H:(half + 1) * H, :] = y.astype(y_ref.dtype)

    def pair_step(cur, nxt):
        gather_wait(cur)
        gather(tokn_ref, 0, nxt)
        ffn(0, cur)
        gather_wait(cur + 1)
        gather(tokn_ref, 1, nxt + 1)
        ffn(1, cur + 1)

        @pl.when(j == n_used - 1)
        def _():
            gather_wait(nxt)
            gather_wait(nxt + 1)

    @pl.when(j < n_used)
    def _():
        @pl.when(j == 0)
        def _():
            gather(tok_ref, 0, 0)
            gather(tok_ref, 1, 1)

        @pl.when((j == 0) | (pe_ref[j] != pe_ref[jnp.maximum(j - 1, 0)]))
        def _():
            wgu_sc[...] = wgu_ref[0, 0].astype(BF16)
            wd_sc[...] = wd_ref[0, 0].astype(BF16)

        @pl.when(j % 2 == 0)
        def _():
            pair_step(0, 2)

        @pl.when(j % 2 == 1)
        def _():
            pair_step(2, 0)

    @pl.when(j >= n_used)
    def _():
        y_ref[...] = jnp.zeros(y_ref.shape, y_ref.dtype)


def _moe_experts(layer, pair_expert, n_used, tok_tab, u, w_gu, b_gu, w_down, b_down):
    n_pairs = tok_tab.shape[0]
    D = u.shape[1]
    pair = 2 * MOE_BLOCK
    tab = lambda f: pl.BlockSpec((1, 1, pair), f, memory_space=pltpu.SMEM)
    grid_spec = pltpu.PrefetchScalarGridSpec(
        num_scalar_prefetch=2,
        grid=(n_pairs,),
        in_specs=[tab(lambda j, pe, nu: (j, 0, 0)),
                  tab(lambda j, pe, nu: (jnp.minimum(j + 1, n_pairs - 1), 0, 0)),
                  pl.BlockSpec(memory_space=pl.ANY),
                  pl.BlockSpec((1, 1, D, 2 * D), lambda j, pe, nu: (layer, pe[j], 0, 0)),
                  pl.BlockSpec((1, 1, 1, 2 * D), lambda j, pe, nu: (layer, pe[j], 0, 0)),
                  pl.BlockSpec((1, 1, D, D), lambda j, pe, nu: (layer, pe[j], 0, 0)),
                  pl.BlockSpec((1, 1, 1, D), lambda j, pe, nu: (layer, pe[j], 0, 0))],
        out_specs=pl.BlockSpec((pair, D), lambda j, pe, nu: (j, 0)),
        scratch_shapes=[pltpu.VMEM((D, 2 * D), BF16), pltpu.VMEM((D, D), BF16),
                        *([pltpu.VMEM((MOE_BLOCK, D), F32)] * 4),
                        pltpu.SemaphoreType.DMA((4,))],
    )
    return pl.pallas_call(
        _moe_kernel,
        grid_spec=grid_spec,
        out_shape=jax.ShapeDtypeStruct((n_pairs * pair, D), BF16),
        compiler_params=pltpu.CompilerParams(dimension_semantics=("arbitrary",),
                                             vmem_limit_bytes=MOE_VMEM_LIMIT),
        name="moe_experts",
    )(pair_expert, n_used, tok_tab, tok_tab, u, w_gu, b_gu.reshape(DEPTH, N_EXPERTS, 1, 2 * D),
      w_down, b_down.reshape(DEPTH, N_EXPERTS, 1, D))


def _moe_ffn(layer, u, route, counts, w_gu, b_gu, w_down, b_down):
    T, D = u.shape
    pair = 2 * MOE_BLOCK
    n_assign = T * TOP_K
    expert = route[:, :TOP_K].astype(jnp.int32)
    rank = route[:, TOP_K:2 * TOP_K].astype(jnp.int32)
    cnt = counts[0, :N_EXPERTS].astype(jnp.int32)
    padded = (cnt + pair - 1) // pair * pair
    padded_end = jnp.cumsum(padded)
    padded_start = padded_end - padded
    lanes_e = jnp.arange(N_EXPERTS, dtype=jnp.int32)
    start_of = jnp.sum(jnp.where(expert[..., None] == lanes_e, padded_start, 0), axis=-1)
    pos = (start_of + rank).T
    n_pairs = -(-n_assign // pair) + N_EXPERTS
    pair_start = jnp.arange(n_pairs, dtype=jnp.int32) * pair
    pair_expert = jnp.minimum(jnp.sum(padded_end[None, :] <= pair_start[:, None], axis=1),
                              N_EXPERTS - 1).astype(jnp.int32)
    n_used = (padded_end[-1:] // pair).astype(jnp.int32)
    tok = jnp.tile(jnp.arange(T, dtype=jnp.int32), TOP_K)
    _, sorted_tok = lax.sort((pos.reshape(-1), tok), num_keys=1)
    cnt_p = cnt[pair_expert][:, None]
    rank_p = (pair_start - padded_start[pair_expert])[:, None] + jnp.arange(pair, dtype=jnp.int32)
    compact = (jnp.cumsum(cnt) - cnt)[pair_expert][:, None] + rank_p
    tok_tab = jnp.where(rank_p < cnt_p, sorted_tok[jnp.minimum(compact, n_assign - 1)], 0)
    yb = _moe_experts(layer, pair_expert, n_used, tok_tab.reshape(n_pairs, 1, pair), u, w_gu, b_gu,
                      w_down, b_down)
    return yb[pos.reshape(-1)].reshape(TOP_K, T, D)


def _final_ln_kernel(x_ref, y_ref, rt_ref, mod_ref, ln_ref, o_ref):
    m = mod_ref[0]
    rt = rt_ref[0]
    f = rt[:, 2 * TOP_K:2 * TOP_K + 1] * y_ref[0].astype(F32)
    for k in range(1, TOP_K):
        f = f + rt[:, 2 * TOP_K + k:2 * TOP_K + k + 1] * y_ref[k].astype(F32)
    z = DEEPNORM_ALPHA * x_ref[0] + m[MOD_G2:MOD_G2 + 1] * f
    mu = jnp.mean(z, axis=-1, keepdims=True)
    zc = z - mu
    y = zc * lax.rsqrt(jnp.mean(zc * zc, axis=-1, keepdims=True) + NORM_EPS)
    o_ref[0] = y * ln_ref[0:1] + ln_ref[1:2]


def _final_ln(x1, yg, route, mods, lnp, nbl):
    B, L, D = x1.shape
    nblk = L // TM
    blk = pl.BlockSpec((1, TM, D), lambda b, i: (b, i, 0))
    return pl.pallas_call(
        _final_ln_kernel,
        grid=(B, nblk),
        in_specs=[blk,
                  pl.BlockSpec((TOP_K, TM, D), lambda b, i: (0, b * nblk + i, 0)),
                  pl.BlockSpec((1, TM, LANES), lambda b, i: (b, i, 0)),
                  pl.BlockSpec((1, 8, D), lambda b, i: (2 * b + i // nbl, 0, 0)),
                  pl.BlockSpec((2, D), lambda b, i: (0, 0))],
        out_specs=blk,
        out_shape=jax.ShapeDtypeStruct((B, L, D), F32),
        compiler_params=_cparams(2),
        name="final_ln",
    )(x1, yg, route, mods, lnp)


def _rope_tables(S, n_ctx):
    rows = S // GRID_W
    row = jnp.repeat(jnp.arange(rows), GRID_W).astype(F32)
    col = jnp.tile(jnp.arange(GRID_W), rows).astype(F32)
    inv = ROPE_BASE ** (-jnp.arange(ROPE_PAIRS_AXIS, dtype=F32) / ROPE_PAIRS_AXIS)
    ang = jnp.concatenate([row[:, None] * inv, col[:, None] * inv], -1)
    cos, sin = jnp.cos(ang), jnp.sin(ang)
    cos = jnp.concatenate([cos, jnp.ones((n_ctx, cos.shape[1]), F32)], axis=0)
    sin = jnp.concatenate([sin, jnp.zeros((n_ctx, sin.shape[1]), F32)], axis=0)
    return (jnp.concatenate([cos, cos, cos, cos], axis=1),
            jnp.concatenate([-sin, -sin, sin, sin], axis=1))


def _qk_column_perm():
    lane = np.arange(DA_HEAD_W)
    half, mp, jj = lane // (2 * ROPE_HALF), (lane % (2 * ROPE_HALF)) // ROPE_HALF, lane % ROPE_HALF
    src = mp * DA_HEAD_DIM + half * ROPE_HALF + jj
    head = np.arange(DA_HEADS)[:, None] * DA_HEAD_W
    perm = (head + src[None, :]).reshape(-1)
    return np.concatenate([perm, D_MODEL + perm])


def _split_router_w(w):
    w = jnp.pad(w, ((0, 0), (0, LANES - N_EXPERTS)))
    hi = w.astype(BF16)
    lo = (w - hi.astype(F32)).astype(BF16)
    return jnp.concatenate([hi, lo], axis=1)


def _layer_mods(mod_rows, B):
    D = D_MODEL
    parts = mod_rows.reshape(8, 6, D)
    sh1, sc1, g1, sh2, sc2, g2 = (parts[:, n] for n in range(6))
    tab = jnp.stack([1.0 + sc1, sh1, g1, 1.0 + sc2, sh2, g2, jnp.zeros_like(g1),
                     jnp.zeros_like(g1)], axis=1)
    lat = tab[:B]
    ctx = jnp.broadcast_to(tab[B:B + 1], (B, 8, D))
    return jnp.stack([lat, ctx], axis=1).reshape(2 * B, 8, D)


def kernel(x, c, ctx, c_ctx, ada_w, ada_b, ln_g, ln_b, da_w_in, da_w_out, da_lambda, da_subln_w,
           gla_w_in, gla_w_gate, gla_b_gate, gla_norm_w, gla_w_out, router_w, router_b,
           moe_w_gu, moe_b_gu, moe_w_down, moe_b_down):
    B, S, D = x.shape
    n_ctx = ctx.shape[1]
    assert D == D_MODEL and n_ctx == TM and S % TM == 0 and S % GRID_W == 0 and B + 1 <= 8
    nbl = S // TM
    nb = nbl + 1

    cc = jnp.concatenate([c, c_ctx[None, :], jnp.zeros((8 - B - 1, D), F32)], axis=0)
    mod_all = _ada_mods(cc, ada_w, ada_b)

    rw = [_split_router_w(router_w[i]) for i in range(DEPTH)]
    rb = [jnp.pad(router_b[i], (0, LANES - N_EXPERTS)).reshape(1, LANES) for i in range(DEPTH)]
    lnp = [[jnp.stack([ln_g[i, n], ln_b[i, n]]) for n in range(2)] for i in range(DEPTH)]

    mods = _layer_mods(mod_all[0], B)
    w_in = da_w_in[0]
    wqk = w_in[:, _qk_column_perm()].astype(BF16)
    wv = w_in[:, 2 * D:].astype(BF16)
    cos, sin = _rope_tables(S, n_ctx)
    q, k, v = _proj_da(x, ctx, mods, cos, sin, wqk, wv)
    lam_init = _lambda_init(0)
    lv = da_lambda[0].astype(F32)
    lam = (jnp.exp(jnp.sum(lv[0] * lv[1])) - jnp.exp(jnp.sum(lv[2] * lv[3])) + lam_init).reshape(1)
    o = _diff_attention(lam, q, k, v, da_subln_w[0], lam_init)
    x1, u, route, counts = _mixer_out("da", [o], da_w_out[0].astype(BF16), x, ctx, nbl, 0, mods,
                                      lnp[0][0], rw[0], rb[0], nb)
    yg = _moe_ffn(0, u.reshape(B * nb * TM, D), route.reshape(B * nb * TM, LANES), counts,
                  moe_w_gu, moe_b_gu, moe_w_down, moe_b_down)
    xa = _final_ln(x1, yg, route, mods, lnp[0][1], nbl)

    mods = _layer_mods(mod_all[1], B)
    gw = gla_w_in[0]
    c3 = 2 * GLA_DK + 2 * GLA_DV
    w_main = gw[:, :c3].astype(BF16)
    wz = jnp.pad(gw[:, c3:], ((0, 0), (0, LANES - 2 * GLA_GATE_RANK))).astype(BF16)
    wg = jnp.zeros((LANES, 2 * GLA_DK), F32)
    wg = wg.at[:GLA_GATE_RANK, :GLA_DK].set(gla_w_gate[0, 0])
    wg = wg.at[GLA_GATE_RANK:2 * GLA_GATE_RANK, GLA_DK:].set(gla_w_gate[0, 1])
    bg = gla_b_gate[0].reshape(1, 2 * GLA_DK)
    gq, gk, gv, gr, gg = _proj_gla(xa, mods, w_main, wz, wg, bg)
    of = _gla_scan(gq, gk, gv, gg, reverse=False)
    ob = _gla_scan(gq, gk, gv, gg, reverse=True)
    x1, u, route, counts = _mixer_out("gla", [of, ob, gr], gla_w_out[0].astype(BF16), xa, xa, nbl, nbl,
                                      mods, lnp[1][0], rw[1], rb[1], nbl, norm_w=gla_norm_w[0])
    yg = _moe_ffn(1, u.reshape(B * S, D), route.reshape(B * S, LANES), counts,
                  moe_w_gu, moe_b_gu, moe_w_down, moe_b_down)
    return _final_ln(x1, yg, route, mods, lnp[1][1], nbl)
```
